```python
import math
import functools
import jax
import jax.numpy as jnp
from jax import lax
import numpy as np

D_MODEL = 1024
BATCH = 4
SEQ = 4096
DEPTH = 2
DEC_BATCH = 32
DEC_SEQ = 1
PAST_LEN = 16384
PAGE_SIZE = 128

HEAD_DIM = 64
N_MIXERS = 4
MIX_W = D_MODEL
GROUP_W = MIX_W // N_MIXERS
N_HEADS = GROUP_W // HEAD_DIM
DA_QK = HEAD_DIM // 2
Q_BLOCK = 128
CONV_WIDTH = 31
RW_W_RANK = 64
RW_A_RANK = 64
RW_G_RANK = 128
RW_OFF_W = 3 * GROUP_W
RW_OFF_A = RW_OFF_W + RW_W_RANK
RW_OFF_G = RW_OFF_A + RW_A_RANK
RW_COLS = RW_OFF_G + RW_G_RANK
DN_CONV = 4
DN_CHUNK = 64
DN_QKV = 3 * GROUP_W
DN_COLS = DN_QKV + 2 * N_HEADS + GROUP_W
DA_COLS = 3 * GROUP_W
CV_COLS = 2 * GROUP_W
OFF_CV = DA_COLS
OFF_RW = OFF_CV + CV_COLS
OFF_DN = OFF_RW + RW_COLS
N_IN = OFF_DN + DN_COLS
FFN_HIDDEN = -((-8 * D_MODEL) // (3 * 256)) * 256
RMS_EPS = 1e-6
LN_EPS = 1e-5
RW_GN_EPS = 64e-5
NEG_INF = -1e30

kernel_name = 'hybrid_diffattn_conformer_rwkv7_gdn_step'


def rms_norm(x, g, eps=RMS_EPS):
    xf = x.astype(jnp.float32)
    y = xf * lax.rsqrt(jnp.mean(xf * xf, axis=-1, keepdims=True) + eps)
    return (y * g.astype(jnp.float32)).astype(x.dtype)


def layer_norm(x, g, b, eps):
    xf = x.astype(jnp.float32)
    mu = jnp.mean(xf, axis=-1, keepdims=True)
    var = jnp.mean(jnp.square(xf - mu), axis=-1, keepdims=True)
    return (xf - mu) * lax.rsqrt(var + eps) * g + b


def l2_normalize(x, eps=1e-6):
    return x * lax.rsqrt(jnp.sum(x * x, axis=-1, keepdims=True) + eps)


def causal_depthwise_conv(x_ext, w):
    c = x_ext.shape[-1]
    return lax.conv_general_dilated(x_ext, w[:, None, :].astype(x_ext.dtype), (1,), 'VALID',
                                    dimension_numbers=('NWC', 'WIO', 'NWC'), feature_group_count=c)


def alibi_slopes():
    return 2.0 ** (-8.0 * jnp.arange(1, N_HEADS + 1, dtype=jnp.float32) / N_HEADS)


def diff_attn_core(q, k, v, q_pos, k_pos, lam):
    qf = q.astype(jnp.float32) * (DA_QK ** -0.5)
    kf = k.astype(jnp.float32)
    dist = (q_pos[:, None] - k_pos[None, :]).astype(jnp.float32)
    bias = jnp.where(dist[None] >= 0, -alibi_slopes()[:, None, None] * dist[None], NEG_INF)
    s1 = jnp.einsum('bqhd,bkhd->bhqk', qf[..., :DA_QK], kf[..., :DA_QK]) + bias
    s2 = jnp.einsum('bqhd,bkhd->bhqk', qf[..., DA_QK:], kf[..., DA_QK:]) + bias
    a = jax.nn.softmax(s1, axis=-1) - lam * jax.nn.softmax(s2, axis=-1)
    return jnp.einsum('bhqk,bkhd->bqhd', a, v.astype(jnp.float32))


def diff_attn_prompt(q, k, v, lam):
    b, t = q.shape[:2]
    nb = t // Q_BLOCK
    pos = jnp.arange(t)
    qb = jnp.moveaxis(q.reshape(b, nb, Q_BLOCK, N_HEADS, HEAD_DIM), 1, 0)
    pb = pos.reshape(nb, Q_BLOCK)
    ob = lax.map(lambda a: diff_attn_core(a[0], k, v, a[1], pos, lam), (qb, pb))
    return jnp.moveaxis(ob, 0, 1).reshape(b, t, N_HEADS, HEAD_DIM)


def diff_attn_sample(q, k, v, lam, past_k, past_v):
    past = past_k.shape[1]
    t = q.shape[1]
    k_all = jnp.concatenate([past_k.astype(k.dtype), k], axis=1)
    v_all = jnp.concatenate([past_v.astype(v.dtype), v], axis=1)
    return diff_attn_core(q, k_all, v_all, past + jnp.arange(t), jnp.arange(past + t), lam)


def gather_pages(cache_l, page_table):
    pages = cache_l[page_table]
    return pages.reshape(page_table.shape[0], -1, *cache_l.shape[2:])


def conformer_conv_mix(p, buf, conv_w, conv_b, ln_g, ln_b):
    glu = p[..., :GROUP_W] * jax.nn.sigmoid(p[..., GROUP_W:])
    ext = jnp.concatenate([buf.astype(glu.dtype), glu], axis=1)
    h = causal_depthwise_conv(ext, conv_w) + conv_b
    h = jax.nn.silu(layer_norm(h, ln_g, ln_b, LN_EPS))
    return h, ext[:, -(CONV_WIDTH - 1):]


def rwkv7_mix(p, shift_prev, S0, mu, w0, w_up, a0, a_up, g_up, k_k, k_a, r_k, gn_g, gn_b):
    b, t, _ = p.shape
    pf = p.astype(jnp.float32)
    prev = jnp.concatenate([shift_prev[:, None].astype(jnp.float32), pf[:, :-1]], axis=1)
    xm = pf + (prev - pf) * mu
    r = xm[..., :GROUP_W]
    k = xm[..., GROUP_W:2 * GROUP_W]
    v = xm[..., 2 * GROUP_W:RW_OFF_W]
    wd = xm[..., RW_OFF_W:RW_OFF_A]
    ad = xm[..., RW_OFF_A:RW_OFF_G]
    gd = xm[..., RW_OFF_G:RW_COLS]
    w_log = -jax.nn.softplus(-(w0 + jnp.tanh(wd) @ w_up)) - 0.5
    decay = jnp.exp(-jnp.exp(w_log))
    a = jax.nn.sigmoid(a0 + ad @ a_up)
    g = jax.nn.sigmoid(gd) @ g_up
    hs = lambda z: z.reshape(b, t, N_HEADS, HEAD_DIM)
    kk = l2_normalize(hs(k * k_k))
    k = k * (1.0 + (a - 1.0) * k_a)
    r_h, k_h, v_h, w_h, a_h = hs(r), hs(k), hs(v), hs(decay), hs(a)

    def step(S, inp):
        r_t, w_t, k_t, v_t, kk_t, a_t = inp
        sa = jnp.einsum('bhij,bhj->bhi', S, -kk_t)
        S = (S * w_t[:, :, None, :] + sa[..., None] * (kk_t * a_t)[:, :, None, :]
             + v_t[..., None] * k_t[:, :, None, :])
        return S, jnp.einsum('bhij,bhj->bhi', S, r_t)

    tm = lambda z: jnp.moveaxis(z, 1, 0)
    S, y = lax.scan(step, S0.astype(jnp.float32),
                    (tm(r_h), tm(w_h), tm(k_h), tm(v_h), tm(kk), tm(a_h)))
    y = jnp.moveaxis(y, 0, 1)
    ym = jnp.mean(y, axis=-1, keepdims=True)
    yv = jnp.mean(jnp.square(y - ym), axis=-1, keepdims=True)
    y = ((y - ym) * lax.rsqrt(yv + RW_GN_EPS)).reshape(b, t, GROUP_W) * gn_g + gn_b
    bonus = jnp.sum(r_h * k_h * r_k, axis=-1, keepdims=True) * v_h
    y = (y + bonus.reshape(b, t, GROUP_W)) * g
    return y, pf[:, -1], S


def chunk_gated_delta_rule(q, k, v, g, beta, S0):
    b, t = q.shape[:2]
    pad = (-t) % DN_CHUNK
    n = (t + pad) // DN_CHUNK

    def blocks(z):
        z = jnp.pad(z, [(0, 0), (0, pad)] + [(0, 0)] * (z.ndim - 2))
        z = z.reshape(b, n, DN_CHUNK, *z.shape[2:])
        return jnp.moveaxis(z, 3, 2)

    q, k, v, g, beta = map(blocks, (q, k, v, g, beta))
    gc = jnp.cumsum(g, axis=-1)
    idx = jnp.arange(DN_CHUNK)
    incl = idx[:, None] >= idx[None, :]
    strict = idx[:, None] > idx[None, :]
    decay = jnp.exp(jnp.where(incl, gc[..., :, None] - gc[..., None, :], NEG_INF))
    kb = k * beta[..., None]
    m = jnp.where(strict, jnp.einsum('bnhid,bnhjd->bnhij', kb, k) * decay, 0.0)
    a_mat = m + jnp.eye(DN_CHUNK, dtype=m.dtype)
    rhs = jnp.concatenate([v * beta[..., None], kb * jnp.exp(gc)[..., None]], axis=-1)
    sol = lax.linalg.triangular_solve(a_mat, rhs, left_side=True, lower=True, unit_diagonal=True)
    dv = v.shape[-1]
    u, w = sol[..., :dv], sol[..., dv:]
    qk = jnp.einsum('bnhid,bnhjd->bnhij', q, k) * decay

    def step(S, inp):
        q_c, k_c, u_c, w_c, g_c, qk_c = inp
        v_new = u_c - jnp.einsum('bhcd,bhde->bhce', w_c, S)
        o = (jnp.einsum('bhcd,bhde->bhce', q_c * jnp.exp(g_c)[..., None], S)
             + jnp.einsum('bhij,bhje->bhie', qk_c, v_new))
        g_last = g_c[..., -1:]
        S = S * jnp.exp(g_last)[..., None] + jnp.einsum(
            'bhcd,bhce->bhde', k_c * jnp.exp(g_last - g_c)[..., None], v_new)
        return S, o

    nm = lambda z: jnp.moveaxis(z, 1, 0)
    S, o = lax.scan(step, S0, (nm(q), nm(k), nm(u), nm(w), nm(gc), nm(qk)))
    o = jnp.moveaxis(o, 0, 1)
    o = jnp.moveaxis(o, 3, 2).reshape(b, n * DN_CHUNK, N_HEADS, dv)[:, :t]
    return o, S


def gated_deltanet_mix(p, conv_prev, S0, conv_w, A_log, dt_bias, norm_g):
    b, t, _ = p.shape
    ext = jnp.concatenate([conv_prev.astype(p.dtype), p[..., :DN_QKV]], axis=1)
    qkv = jax.nn.silu(causal_depthwise_conv(ext, conv_w)).astype(jnp.float32)
    hs = lambda z: z.reshape(b, t, N_HEADS, HEAD_DIM)
    q = l2_normalize(hs(qkv[..., :GROUP_W])) * (HEAD_DIM ** -0.5)
    k = l2_normalize(hs(qkv[..., GROUP_W:2 * GROUP_W]))
    v = hs(qkv[..., 2 * GROUP_W:])
    a_raw = p[..., DN_QKV:DN_QKV + N_HEADS].astype(jnp.float32)
    b_raw = p[..., DN_QKV + N_HEADS:DN_QKV + 2 * N_HEADS].astype(jnp.float32)
    z = hs(p[..., DN_QKV + 2 * N_HEADS:].astype(jnp.float32))
    g = -jnp.exp(A_log) * jax.nn.softplus(a_raw + dt_bias)
    beta = jax.nn.sigmoid(b_raw)
    o, S = chunk_gated_delta_rule(q, k, v, g, beta, S0.astype(jnp.float32))
    o = rms_norm(o, norm_g) * jax.nn.silu(z)
    return o.reshape(b, t, GROUP_W), ext[:, -(DN_CONV - 1):], S


def trunk_layer(l, x, attn_fn, cv_buf, rw_shift, rw_S, dn_buf, dn_S, P):
    b, t, _ = x.shape
    hn = rms_norm(x, P['g_pre_mix'][l])
    p = jnp.einsum('btd,dn->btn', hn, P['w_in'][l])
    hs = lambda z: z.reshape(b, t, N_HEADS, HEAD_DIM)
    q = hs(p[..., :GROUP_W])
    k = hs(p[..., GROUP_W:2 * GROUP_W])
    v = hs(p[..., 2 * GROUP_W:DA_COLS])
    lam_init = 0.8 - 0.6 * math.exp(-0.3 * l)
    lam = (jnp.exp(jnp.sum(P['da_lq1'][l] * P['da_lk1'][l]))
           - jnp.exp(jnp.sum(P['da_lq2'][l] * P['da_lk2'][l])) + lam_init).astype(jnp.float32)
    o_a = attn_fn(q, k, v, lam)
    o_a = (rms_norm(o_a, P['da_norm_g'][l]) * (1.0 - lam_init)).reshape(b, t, GROUP_W)
    o_b, cv_new = conformer_conv_mix(p[..., OFF_CV:OFF_RW], cv_buf, P['cv_w'][l], P['cv_b'][l],
                                     P['cv_ln_g'][l], P['cv_ln_b'][l])
    o_c, rw_shift_new, rw_S_new = rwkv7_mix(
        p[..., OFF_RW:OFF_DN], rw_shift, rw_S, P['rw_mu'][l], P['rw_w0'][l], P['rw_w_up'][l],
        P['rw_a0'][l], P['rw_a_up'][l], P['rw_g_up'][l], P['rw_k_k'][l], P['rw_k_a'][l],
        P['rw_r_k'][l], P['rw_gn_g'][l], P['rw_gn_b'][l])
    o_d, dn_buf_new, dn_S_new = gated_deltanet_mix(
        p[..., OFF_DN:N_IN], dn_buf, dn_S, P['dn_conv_w'][l], P['dn_A_log'][l],
        P['dn_dt_bias'][l], P['dn_norm_g'][l])
    mix = jnp.concatenate([o_a.astype(x.dtype), o_b.astype(x.dtype), o_c.astype(x.dtype),
                           o_d.astype(x.dtype)], axis=-1)
    x = x + rms_norm(jnp.einsum('btm,md->btd', mix, P['w_out'][l]), P['g_post_mix'][l])
    hn = rms_norm(x, P['g_pre_ffn'][l])
    f = jax.nn.silu(hn @ P['ffn_w_gate'][l]) * (hn @ P['ffn_w_up'][l])
    x = x + rms_norm(f @ P['ffn_w_down'][l], P['g_post_ffn'][l])
    return x, (k, v, cv_new, rw_shift_new, rw_S_new, dn_buf_new, dn_S_new)


def setup_inputs(seed: int = 0) -> dict:
    key = jax.random.key(seed)
    ks = iter(jax.random.split(key, 64))
    f32 = jnp.float32

    def nrm(shape, scale):
        return jax.random.normal(next(ks), shape, f32) * scale

    def gain(shape):
        return 1.0 + nrm(shape, 0.02)

    def unif(shape, lo, hi):
        return jax.random.uniform(next(ks), shape, f32, lo, hi)

    n_pages = PAST_LEN // PAGE_SIZE
    n_used = DEC_BATCH * n_pages
    n_pool = n_used + max(1, n_used // 4)
    kv_shape = (DEPTH, n_pool, PAGE_SIZE, N_HEADS, HEAD_DIM)
    page_table = jax.random.permutation(next(ks), n_pool)[:n_used].reshape(DEC_BATCH, n_pages).astype(jnp.int32)
    dt = jnp.exp(unif((DEPTH, N_HEADS), math.log(1e-3), math.log(1e-1)))
    return {
        'x_prompt': nrm((BATCH, SEQ, D_MODEL), 1.0),
        'x_sample': nrm((DEC_BATCH, DEC_SEQ, D_MODEL), 1.0),
        'cache_k': nrm(kv_shape, 1.0),
        'cache_v': nrm(kv_shape, 1.0),
        'page_table': page_table,
        'state_conv': nrm((DEPTH, DEC_BATCH, CONV_WIDTH - 1, GROUP_W), 1.0),
        'state_rw_shift': nrm((DEPTH, DEC_BATCH, RW_COLS), 1.0),
        'state_rw_wkv': nrm((DEPTH, DEC_BATCH, N_HEADS, HEAD_DIM, HEAD_DIM), 0.5),
        'state_dn_conv': nrm((DEPTH, DEC_BATCH, DN_CONV - 1, DN_QKV), 1.0),
        'state_dn_ssm': nrm((DEPTH, DEC_BATCH, N_HEADS, HEAD_DIM, HEAD_DIM), 0.5),
        'g_pre_mix': gain((DEPTH, D_MODEL)),
        'g_post_mix': gain((DEPTH, D_MODEL)),
        'g_pre_ffn': gain((DEPTH, D_MODEL)),
        'g_post_ffn': gain((DEPTH, D_MODEL)),
        'w_in': nrm((DEPTH, D_MODEL, N_IN), D_MODEL ** -0.5),
        'w_out': nrm((DEPTH, MIX_W, D_MODEL), MIX_W ** -0.5),
        'da_lq1': nrm((DEPTH, DA_QK), 0.1),
        'da_lk1': nrm((DEPTH, DA_QK), 0.1),
        'da_lq2': nrm((DEPTH, DA_QK), 0.1),
        'da_lk2': nrm((DEPTH, DA_QK), 0.1),
        'da_norm_g': gain((DEPTH, HEAD_DIM)),
        'cv_w': nrm((DEPTH, CONV_WIDTH, GROUP_W), CONV_WIDTH ** -0.5),
        'cv_b': nrm((DEPTH, GROUP_W), 0.02),
        'cv_ln_g': gain((DEPTH, GROUP_W)),
        'cv_ln_b': nrm((DEPTH, GROUP_W), 0.02),
        'rw_mu': unif((DEPTH, RW_COLS), 0.0, 1.0),
        'rw_w0': unif((DEPTH, GROUP_W), -6.0, -1.0),
        'rw_w_up': nrm((DEPTH, RW_W_RANK, GROUP_W), 0.1),
        'rw_a0': nrm((DEPTH, GROUP_W), 0.1),
        'rw_a_up': nrm((DEPTH, RW_A_RANK, GROUP_W), 0.1),
        'rw_g_up': nrm((DEPTH, RW_G_RANK, GROUP_W), RW_G_RANK ** -0.5),
        'rw_k_k': 0.85 + nrm((DEPTH, GROUP_W), 0.02),
        'rw_k_a': 1.0 + nrm((DEPTH, GROUP_W), 0.02),
        'rw_r_k': nrm((DEPTH, N_HEADS, HEAD_DIM), 0.1),
        'rw_gn_g': gain((DEPTH, GROUP_W)),
        'rw_gn_b': nrm((DEPTH, GROUP_W), 0.02),
        'dn_conv_w': nrm((DEPTH, DN_CONV, DN_QKV), DN_CONV ** -0.5),
        'dn_A_log': jnp.log(unif((DEPTH, N_HEADS), 1.0, 16.0)),
        'dn_dt_bias': dt + jnp.log(-jnp.expm1(-dt)),
        'dn_norm_g': gain((DEPTH, HEAD_DIM)),
        'ffn_w_gate': nrm((DEPTH, D_MODEL, FFN_HIDDEN), D_MODEL ** -0.5),
        'ffn_w_up': nrm((DEPTH, D_MODEL, FFN_HIDDEN), D_MODEL ** -0.5),
        'ffn_w_down': nrm((DEPTH, FFN_HIDDEN, D_MODEL), FFN_HIDDEN ** -0.5),
    }


def reference(x_prompt, x_sample, cache_k, cache_v, page_table, state_conv, state_rw_shift,
              state_rw_wkv, state_dn_conv, state_dn_ssm, g_pre_mix, g_post_mix, g_pre_ffn,
              g_post_ffn, w_in, w_out, da_lq1, da_lk1, da_lq2, da_lk2, da_norm_g, cv_w, cv_b,
              cv_ln_g, cv_ln_b, rw_mu, rw_w0, rw_w_up, rw_a0, rw_a_up, rw_g_up, rw_k_k, rw_k_a,
              rw_r_k, rw_gn_g, rw_gn_b, dn_conv_w, dn_A_log, dn_dt_bias, dn_norm_g,
              ffn_w_gate, ffn_w_up, ffn_w_down):
    P = dict(g_pre_mix=g_pre_mix, g_post_mix=g_post_mix, g_pre_ffn=g_pre_ffn, g_post_ffn=g_post_ffn,
             w_in=w_in, w_out=w_out, da_lq1=da_lq1, da_lk1=da_lk1, da_lq2=da_lq2, da_lk2=da_lk2,
             da_norm_g=da_norm_g, cv_w=cv_w, cv_b=cv_b, cv_ln_g=cv_ln_g, cv_ln_b=cv_ln_b,
             rw_mu=rw_mu, rw_w0=rw_w0, rw_w_up=rw_w_up, rw_a0=rw_a0, rw_a_up=rw_a_up,
             rw_g_up=rw_g_up, rw_k_k=rw_k_k, rw_k_a=rw_k_a, rw_r_k=rw_r_k, rw_gn_g=rw_gn_g,
             rw_gn_b=rw_gn_b, dn_conv_w=dn_conv_w, dn_A_log=dn_A_log, dn_dt_bias=dn_dt_bias,
             dn_norm_g=dn_norm_g, ffn_w_gate=ffn_w_gate, ffn_w_up=ffn_w_up, ffn_w_down=ffn_w_down)
    bp = x_prompt.shape[0]
    dtp = x_prompt.dtype
    yp = x_prompt
    ys = x_sample
    outs_p = []
    outs_s = []
    for l in range(DEPTH):
        yp, st_p = trunk_layer(
            l, yp, diff_attn_prompt,
            jnp.zeros((bp, CONV_WIDTH - 1, GROUP_W), dtp), jnp.zeros((bp, RW_COLS), dtp),
            jnp.zeros((bp, N_HEADS, HEAD_DIM, HEAD_DIM), jnp.float32),
            jnp.zeros((bp, DN_CONV - 1, DN_QKV), dtp),
            jnp.zeros((bp, N_HEADS, HEAD_DIM, HEAD_DIM), jnp.float32), P)
        attn_s = functools.partial(diff_attn_sample, past_k=gather_pages(cache_k[l], page_table),
                                   past_v=gather_pages(cache_v[l], page_table))
        ys, st_s = trunk_layer(l, ys, attn_s, state_conv[l], state_rw_shift[l], state_rw_wkv[l],
                               state_dn_conv[l], state_dn_ssm[l], P)
        outs_p.append(st_p)
        outs_s.append(st_s)
    stk = lambda outs, i: jnp.stack([o[i] for o in outs])
    new_k_prompt = stk(outs_p, 0)
    new_v_prompt = stk(outs_p, 1)
    new_k_sample = stk(outs_s, 0)
    new_v_sample = stk(outs_s, 1)
    new_conv_prompt = stk(outs_p, 2)
    new_conv_sample = stk(outs_s, 2)
    new_rw_shift_prompt = stk(outs_p, 3)
    new_rw_shift_sample = stk(outs_s, 3)
    new_rw_wkv_prompt = stk(outs_p, 4)
    new_rw_wkv_sample = stk(outs_s, 4)
    new_dn_conv_prompt = stk(outs_p, 5)
    new_dn_conv_sample = stk(outs_s, 5)
    new_dn_ssm_prompt = stk(outs_p, 6)
    new_dn_ssm_sample = stk(outs_s, 6)
    return (yp, ys, new_k_prompt, new_v_prompt, new_k_sample, new_v_sample,
            new_conv_prompt, new_conv_sample, new_rw_shift_prompt, new_rw_shift_sample,
            new_rw_wkv_prompt, new_rw_wkv_sample, new_dn_conv_prompt, new_dn_conv_sample,
            new_dn_ssm_prompt, new_dn_ssm_sample)
```

```python
import functools
import math

import jax
import jax.numpy as jnp
from jax import lax
from jax.experimental import pallas as pl
from jax.experimental.pallas import tpu as pltpu

D_MODEL = 1024
HEAD_DIM = 64
N_HEADS = 4
GROUP_W = 256
DA_QK = HEAD_DIM // 2
Q_BLOCK = 128
CONV_WIDTH = 31
RW_W_RANK = 64
RW_A_RANK = 64
RW_G_RANK = 128
RW_OFF_W = 3 * GROUP_W
RW_OFF_A = RW_OFF_W + RW_W_RANK
RW_OFF_G = RW_OFF_A + RW_A_RANK
RW_COLS = RW_OFF_G + RW_G_RANK
DN_CONV = 4
DN_CHUNK = 64
DN_QKV = 3 * GROUP_W
DN_COLS = DN_QKV + 2 * N_HEADS + GROUP_W
DA_COLS = 3 * GROUP_W
CV_COLS = 2 * GROUP_W
OFF_CV = DA_COLS
OFF_RW = OFF_CV + CV_COLS
OFF_DN = OFF_RW + RW_COLS
N_IN = OFF_DN + DN_COLS
FFN_HIDDEN = 2816
RMS_EPS = 1e-6
LN_EPS = 1e-5
RW_GN_EPS = 64e-5
NEG_INF = -1e30

LANES = 128
DN_PAD = DN_QKV + GROUP_W + LANES
VMEM_LIMIT = 56 * 1024 * 1024

F32 = jnp.float32
BF16 = jnp.bfloat16


def _cparams(*sem):
    return pltpu.CompilerParams(dimension_semantics=sem, vmem_limit_bytes=VMEM_LIMIT)


def _rms(x, g):
    return x * lax.rsqrt(jnp.mean(x * x, axis=-1, keepdims=True) + RMS_EPS) * g


def _row_tile(n, target):
    return target if n % target == 0 else n


def _in_proj_body(x_ref, g_ref, w_ref, q_ref, k_ref, v_ref, cv_ref, rw_ref, dn_ref):
    hn = _rms(x_ref[...], g_ref[...]).astype(BF16)
    off = 0
    for ref in (q_ref, k_ref, v_ref, cv_ref, rw_ref, dn_ref):
        w = ref.shape[-1]
        ref[...] = jnp.dot(hn, w_ref[:, off:off + w], preferred_element_type=F32)
        off += w


def _prep_w_in(w):
    dn = w[:, OFF_DN:]
    dn = jnp.concatenate([dn[:, :DN_QKV], dn[:, DN_QKV + 2 * N_HEADS:], dn[:, DN_QKV:DN_QKV + 2 * N_HEADS],
                          jnp.zeros((w.shape[0], LANES - 2 * N_HEADS), w.dtype)], axis=1)
    return jnp.concatenate([w[:, :OFF_DN], dn], axis=1).astype(BF16)


def in_proj(x, g, w):
    n = x.shape[0]
    tm = _row_tile(n, 512)
    widths = (GROUP_W, GROUP_W, GROUP_W, CV_COLS, RW_COLS, DN_PAD)
    row = lambda wd: pl.BlockSpec((tm, wd), lambda i: (i, 0))
    return pl.pallas_call(
        _in_proj_body,
        grid=(n // tm,),
        in_specs=[row(D_MODEL), pl.BlockSpec((1, D_MODEL), lambda i: (0, 0)),
                  pl.BlockSpec(w.shape, lambda i: (0, 0))],
        out_specs=[row(wd) for wd in widths],
        out_shape=[jax.ShapeDtypeStruct((n, wd), F32) for wd in widths],
        compiler_params=_cparams("parallel"),
        name="in_proj",
    )(x, g.reshape(1, -1), w)


def _out_proj_body(x_ref, a_ref, b_ref, c_ref, d_ref, w_ref, g_ref, o_ref):
    acc = None
    for i, ref in enumerate((a_ref, b_ref, c_ref, d_ref)):
        part = jnp.dot(ref[...].astype(BF16), w_ref[i * GROUP_W:(i + 1) * GROUP_W, :],
                       preferred_element_type=F32)
        acc = part if acc is None else acc + part
    o_ref[...] = x_ref[...] + _rms(acc, g_ref[...])


def out_proj(x, oa, ob, oc, od, w, g):
    n = x.shape[0]
    tm = _row_tile(n, 512)
    row = lambda wd: pl.BlockSpec((tm, wd), lambda i: (i, 0))
    return pl.pallas_call(
        _out_proj_body,
        grid=(n // tm,),
        in_specs=[row(D_MODEL), row(GROUP_W), row(GROUP_W), row(GROUP_W), row(GROUP_W),
                  pl.BlockSpec(w.shape, lambda i: (0, 0)), pl.BlockSpec((1, D_MODEL), lambda i: (0, 0))],
        out_specs=row(D_MODEL),
        out_shape=jax.ShapeDtypeStruct((n, D_MODEL), F32),
        compiler_params=_cparams("parallel"),
        name="out_proj",
    )(x, oa, ob, oc, od, w, g.reshape(1, -1))


def _ffn_body(x_ref, g1_ref, wg_ref, wu_ref, wd_ref, g2_ref, o_ref, hn_ref, acc_ref):
    j = pl.program_id(1)

    @pl.when(j == 0)
    def _():
        hn_ref[...] = _rms(x_ref[...], g1_ref[...]).astype(BF16)
        acc_ref[...] = jnp.zeros_like(acc_ref)

    hn = hn_ref[...]
    gate = jnp.dot(hn, wg_ref[...], preferred_element_type=F32)
    up = jnp.dot(hn, wu_ref[...], preferred_element_type=F32)
    f = (gate * jax.nn.sigmoid(gate) * up).astype(BF16)
    acc_ref[...] += jnp.dot(f, wd_ref[...], preferred_element_type=F32)

    @pl.when(j == pl.num_programs(1) - 1)
    def _():
        o_ref[...] = x_ref[...] + _rms(acc_ref[...], g2_ref[...])


def ffn(x, g1, wg, wu, wd, g2):
    n = x.shape[0]
    tm = _row_tile(n, 512)
    th = FFN_HIDDEN // 2
    return pl.pallas_call(
        _ffn_body,
        grid=(n // tm, FFN_HIDDEN // th),
        in_specs=[pl.BlockSpec((tm, D_MODEL), lambda i, j: (i, 0)),
                  pl.BlockSpec((1, D_MODEL), lambda i, j: (0, 0)),
                  pl.BlockSpec((D_MODEL, th), lambda i, j: (0, j)),
                  pl.BlockSpec((D_MODEL, th), lambda i, j: (0, j)),
                  pl.BlockSpec((th, D_MODEL), lambda i, j: (j, 0)),
                  pl.BlockSpec((1, D_MODEL), lambda i, j: (0, 0))],
        out_specs=pl.BlockSpec((tm, D_MODEL), lambda i, j: (i, 0)),
        out_shape=jax.ShapeDtypeStruct((n, D_MODEL), F32),
        scratch_shapes=[pltpu.VMEM((tm, D_MODEL), BF16), pltpu.VMEM((tm, D_MODEL), F32)],
        compiler_params=_cparams("parallel", "arbitrary"),
        name="ffn",
    )(x, g1.reshape(1, -1), wg, wu, wd, g2.reshape(1, -1))


def rms_norm(x, g, eps=RMS_EPS):
    xf = x.astype(jnp.float32)
    y = xf * lax.rsqrt(jnp.mean(xf * xf, axis=-1, keepdims=True) + eps)
    return (y * g.astype(jnp.float32)).astype(x.dtype)


def layer_norm(x, g, b, eps):
    xf = x.astype(jnp.float32)
    mu = jnp.mean(xf, axis=-1, keepdims=True)
    var = jnp.mean(jnp.square(xf - mu), axis=-1, keepdims=True)
    return (xf - mu) * lax.rsqrt(var + eps) * g + b


def l2_normalize(x, eps=1e-6):
    return x * lax.rsqrt(jnp.sum(x * x, axis=-1, keepdims=True) + eps)


def causal_depthwise_conv(x_ext, w):
    c = x_ext.shape[-1]
    return lax.conv_general_dilated(x_ext, w[:, None, :].astype(x_ext.dtype), (1,), 'VALID',
                                    dimension_numbers=('NWC', 'WIO', 'NWC'), feature_group_count=c)


def alibi_slopes():
    return 2.0 ** (-8.0 * jnp.arange(1, N_HEADS + 1, dtype=jnp.float32) / N_HEADS)


def diff_attn_core(q, k, v, q_pos, k_pos, lam):
    qf = q.astype(jnp.float32) * (DA_QK ** -0.5)
    kf = k.astype(jnp.float32)
    dist = (q_pos[:, None] - k_pos[None, :]).astype(jnp.float32)
    bias = jnp.where(dist[None] >= 0, -alibi_slopes()[:, None, None] * dist[None], NEG_INF)
    s1 = jnp.einsum('bqhd,bkhd->bhqk', qf[..., :DA_QK], kf[..., :DA_QK]) + bias
    s2 = jnp.einsum('bqhd,bkhd->bhqk', qf[..., DA_QK:], kf[..., DA_QK:]) + bias
    a = jax.nn.softmax(s1, axis=-1) - lam * jax.nn.softmax(s2, axis=-1)
    return jnp.einsum('bhqk,bkhd->bqhd', a, v.astype(jnp.float32))


def diff_attn_prompt(q, k, v, lam):
    b, t = q.shape[:2]
    nb = t // Q_BLOCK
    pos = jnp.arange(t)
    qb = jnp.moveaxis(q.reshape(b, nb, Q_BLOCK, N_HEADS, HEAD_DIM), 1, 0)
    pb = pos.reshape(nb, Q_BLOCK)
    ob = lax.map(lambda a: diff_attn_core(a[0], k, v, a[1], pos, lam), (qb, pb))
    return jnp.moveaxis(ob, 0, 1).reshape(b, t, N_HEADS, HEAD_DIM)


def diff_attn_sample(q, k, v, lam, past_k, past_v):
    past = past_k.shape[1]
    t = q.shape[1]
    k_all = jnp.concatenate([past_k.astype(k.dtype), k], axis=1)
    v_all = jnp.concatenate([past_v.astype(v.dtype), v], axis=1)
    return diff_attn_core(q, k_all, v_all, past + jnp.arange(t), jnp.arange(past + t), lam)


def gather_pages(cache_l, page_table):
    pages = cache_l[page_table]
    return pages.reshape(page_table.shape[0], -1, *cache_l.shape[2:])


def conformer_conv_mix(p, buf, conv_w, conv_b, ln_g, ln_b):
    glu = p[..., :GROUP_W] * jax.nn.sigmoid(p[..., GROUP_W:])
    ext = jnp.concatenate([buf.astype(glu.dtype), glu], axis=1)
    h = causal_depthwise_conv(ext, conv_w) + conv_b
    h = jax.nn.silu(layer_norm(h, ln_g, ln_b, LN_EPS))
    return h, ext[:, -(CONV_WIDTH - 1):]


def rwkv7_mix(p, shift_prev, S0, mu, w0, w_up, a0, a_up, g_up, k_k, k_a, r_k, gn_g, gn_b):
    b, t, _ = p.shape
    pf = p.astype(jnp.float32)
    prev = jnp.concatenate([shift_prev[:, None].astype(jnp.float32), pf[:, :-1]], axis=1)
    xm = pf + (prev - pf) * mu
    r = xm[..., :GROUP_W]
    k = xm[..., GROUP_W:2 * GROUP_W]
    v = xm[..., 2 * GROUP_W:RW_OFF_W]
    wd = xm[..., RW_OFF_W:RW_OFF_A]
    ad = xm[..., RW_OFF_A:RW_OFF_G]
    gd = xm[..., RW_OFF_G:RW_COLS]
    w_log = -jax.nn.softplus(-(w0 + jnp.tanh(wd) @ w_up)) - 0.5
    decay = jnp.exp(-jnp.exp(w_log))
    a = jax.nn.sigmoid(a0 + ad @ a_up)
    g = jax.nn.sigmoid(gd) @ g_up
    hs = lambda z: z.reshape(b, t, N_HEADS, HEAD_DIM)
    kk = l2_normalize(hs(k * k_k))
    k = k * (1.0 + (a - 1.0) * k_a)
    r_h, k_h, v_h, w_h, a_h = hs(r), hs(k), hs(v), hs(decay), hs(a)

    def step(S, inp):
        r_t, w_t, k_t, v_t, kk_t, a_t = inp
        sa = jnp.einsum('bhij,bhj->bhi', S, -kk_t)
        S = (S * w_t[:, :, None, :] + sa[..., None] * (kk_t * a_t)[:, :, None, :]
             + v_t[..., None] * k_t[:, :, None, :])
        return S, jnp.einsum('bhij,bhj->bhi', S, r_t)

    tm = lambda z: jnp.moveaxis(z, 1, 0)
    S, y = lax.scan(step, S0.astype(jnp.float32),
                    (tm(r_h), tm(w_h), tm(k_h), tm(v_h), tm(kk), tm(a_h)))
    y = jnp.moveaxis(y, 0, 1)
    ym = jnp.mean(y, axis=-1, keepdims=True)
    yv = jnp.mean(jnp.square(y - ym), axis=-1, keepdims=True)
    y = ((y - ym) * lax.rsqrt(yv + RW_GN_EPS)).reshape(b, t, GROUP_W) * gn_g + gn_b
    bonus = jnp.sum(r_h * k_h * r_k, axis=-1, keepdims=True) * v_h
    y = (y + bonus.reshape(b, t, GROUP_W)) * g
    return y, pf[:, -1], S


def chunk_gated_delta_rule(q, k, v, g, beta, S0):
    b, t = q.shape[:2]
    pad = (-t) % DN_CHUNK
    n = (t + pad) // DN_CHUNK

    def blocks(z):
        z = jnp.pad(z, [(0, 0), (0, pad)] + [(0, 0)] * (z.ndim - 2))
        z = z.reshape(b, n, DN_CHUNK, *z.shape[2:])
        return jnp.moveaxis(z, 3, 2)

    q, k, v, g, beta = map(blocks, (q, k, v, g, beta))
    gc = jnp.cumsum(g, axis=-1)
    idx = jnp.arange(DN_CHUNK)
    incl = idx[:, None] >= idx[None, :]
    strict = idx[:, None] > idx[None, :]
    decay = jnp.exp(jnp.where(incl, gc[..., :, None] - gc[..., None, :], NEG_INF))
    kb = k * beta[..., None]
    m = jnp.where(strict, jnp.einsum('bnhid,bnhjd->bnhij', kb, k) * decay, 0.0)
    a_mat = m + jnp.eye(DN_CHUNK, dtype=m.dtype)
    rhs = jnp.concatenate([v * beta[..., None], kb * jnp.exp(gc)[..., None]], axis=-1)
    sol = lax.linalg.triangular_solve(a_mat, rhs, left_side=True, lower=True, unit_diagonal=True)
    dv = v.shape[-1]
    u, w = sol[..., :dv], sol[..., dv:]
    qk = jnp.einsum('bnhid,bnhjd->bnhij', q, k) * decay

    def step(S, inp):
        q_c, k_c, u_c, w_c, g_c, qk_c = inp
        v_new = u_c - jnp.einsum('bhcd,bhde->bhce', w_c, S)
        o = (jnp.einsum('bhcd,bhde->bhce', q_c * jnp.exp(g_c)[..., None], S)
             + jnp.einsum('bhij,bhje->bhie', qk_c, v_new))
        g_last = g_c[..., -1:]
        S = S * jnp.exp(g_last)[..., None] + jnp.einsum(
            'bhcd,bhce->bhde', k_c * jnp.exp(g_last - g_c)[..., None], v_new)
        return S, o

    nm = lambda z: jnp.moveaxis(z, 1, 0)
    S, o = lax.scan(step, S0, (nm(q), nm(k), nm(u), nm(w), nm(gc), nm(qk)))
    o = jnp.moveaxis(o, 0, 1)
    o = jnp.moveaxis(o, 3, 2).reshape(b, n * DN_CHUNK, N_HEADS, dv)[:, :t]
    return o, S


def gated_deltanet_mix(p, conv_prev, S0, conv_w, A_log, dt_bias, norm_g):
    b, t, _ = p.shape
    ext = jnp.concatenate([conv_prev.astype(p.dtype), p[..., :DN_QKV]], axis=1)
    qkv = jax.nn.silu(causal_depthwise_conv(ext, conv_w)).astype(jnp.float32)
    hs = lambda z: z.reshape(b, t, N_HEADS, HEAD_DIM)
    q = l2_normalize(hs(qkv[..., :GROUP_W])) * (HEAD_DIM ** -0.5)
    k = l2_normalize(hs(qkv[..., GROUP_W:2 * GROUP_W]))
    v = hs(qkv[..., 2 * GROUP_W:])
    a_raw = p[..., DN_QKV:DN_QKV + N_HEADS].astype(jnp.float32)
    b_raw = p[..., DN_QKV + N_HEADS:DN_QKV + 2 * N_HEADS].astype(jnp.float32)
    z = hs(p[..., DN_QKV + 2 * N_HEADS:].astype(jnp.float32))
    g = -jnp.exp(A_log) * jax.nn.softplus(a_raw + dt_bias)
    beta = jax.nn.sigmoid(b_raw)
    o, S = chunk_gated_delta_rule(q, k, v, g, beta, S0.astype(jnp.float32))
    o = rms_norm(o, norm_g) * jax.nn.silu(z)
    return o.reshape(b, t, GROUP_W), ext[:, -(DN_CONV - 1):], S


def trunk_layer(l, x, attn_fn, cv_buf, rw_shift, rw_S, dn_buf, dn_S, P, W):
    b, t, _ = x.shape
    n = b * t
    x2 = x.reshape(n, D_MODEL)
    q, k, v, p_cv, p_rw, p_dn = in_proj(x2, P['g_pre_mix'][l], W['w_in'][l])
    hs = lambda z: z.reshape(b, t, N_HEADS, HEAD_DIM)
    q, k, v = hs(q), hs(k), hs(v)
    lam_init = 0.8 - 0.6 * math.exp(-0.3 * l)
    lam = (jnp.exp(jnp.sum(P['da_lq1'][l] * P['da_lk1'][l]))
           - jnp.exp(jnp.sum(P['da_lq2'][l] * P['da_lk2'][l])) + lam_init).astype(jnp.float32)
    o_a = attn_fn(q, k, v, lam)
    o_a = (rms_norm(o_a, P['da_norm_g'][l]) * (1.0 - lam_init)).reshape(n, GROUP_W)
    o_b, cv_new = conformer_conv_mix(p_cv.reshape(b, t, CV_COLS), cv_buf, P['cv_w'][l], P['cv_b'][l],
                                     P['cv_ln_g'][l], P['cv_ln_b'][l])
    o_c, rw_shift_new, rw_S_new = rwkv7_mix(
        p_rw.reshape(b, t, RW_COLS), rw_shift, rw_S, P['rw_mu'][l], P['rw_w0'][l], P['rw_w_up'][l],
        P['rw_a0'][l], P['rw_a_up'][l], P['rw_g_up'][l], P['rw_k_k'][l], P['rw_k_a'][l],
        P['rw_r_k'][l], P['rw_gn_g'][l], P['rw_gn_b'][l])
    p_dn = p_dn.reshape(b, t, DN_PAD)
    p_dn_ref = jnp.concatenate([p_dn[..., :DN_QKV], p_dn[..., DN_QKV + GROUP_W:DN_QKV + GROUP_W + 2 * N_HEADS],
                                p_dn[..., DN_QKV:DN_QKV + GROUP_W]], axis=-1)
    o_d, dn_buf_new, dn_S_new = gated_deltanet_mix(
        p_dn_ref, dn_buf, dn_S, P['dn_conv_w'][l], P['dn_A_log'][l],
        P['dn_dt_bias'][l], P['dn_norm_g'][l])
    x2 = out_proj(x2, o_a, o_b.reshape(n, GROUP_W), o_c.reshape(n, GROUP_W), o_d.reshape(n, GROUP_W),
                  W['w_out'][l], P['g_post_mix'][l])
    x2 = ffn(x2, P['g_pre_ffn'][l], W['ffn_w_gate'][l], W['ffn_w_up'][l], W['ffn_w_down'][l],
             P['g_post_ffn'][l])
    return x2.reshape(b, t, D_MODEL), (k, v, cv_new, rw_shift_new, rw_S_new, dn_buf_new, dn_S_new)


def kernel(x_prompt, x_sample, cache_k, cache_v, page_table, state_conv, state_rw_shift,
           state_rw_wkv, state_dn_conv, state_dn_ssm, g_pre_mix, g_post_mix, g_pre_ffn,
           g_post_ffn, w_in, w_out, da_lq1, da_lk1, da_lq2, da_lk2, da_norm_g, cv_w, cv_b,
           cv_ln_g, cv_ln_b, rw_mu, rw_w0, rw_w_up, rw_a0, rw_a_up, rw_g_up, rw_k_k, rw_k_a,
           rw_r_k, rw_gn_g, rw_gn_b, dn_conv_w, dn_A_log, dn_dt_bias, dn_norm_g,
           ffn_w_gate, ffn_w_up, ffn_w_down):
    P = dict(g_pre_mix=g_pre_mix, g_post_mix=g_post_mix, g_pre_ffn=g_pre_ffn, g_post_ffn=g_post_ffn,
             da_lq1=da_lq1, da_lk1=da_lk1, da_lq2=da_lq2, da_lk2=da_lk2,
             da_norm_g=da_norm_g, cv_w=cv_w, cv_b=cv_b, cv_ln_g=cv_ln_g, cv_ln_b=cv_ln_b,
             rw_mu=rw_mu, rw_w0=rw_w0, rw_w_up=rw_w_up, rw_a0=rw_a0, rw_a_up=rw_a_up,
             rw_g_up=rw_g_up, rw_k_k=rw_k_k, rw_k_a=rw_k_a, rw_r_k=rw_r_k, rw_gn_g=rw_gn_g,
             rw_gn_b=rw_gn_b, dn_conv_w=dn_conv_w, dn_A_log=dn_A_log, dn_dt_bias=dn_dt_bias,
             dn_norm_g=dn_norm_g)
    depth = w_in.shape[0]
    W = dict(w_in=[_prep_w_in(w_in[l]) for l in range(depth)],
             w_out=[w_out[l].astype(BF16) for l in range(depth)],
             ffn_w_gate=[ffn_w_gate[l].astype(BF16) for l in range(depth)],
             ffn_w_up=[ffn_w_up[l].astype(BF16) for l in range(depth)],
             ffn_w_down=[ffn_w_down[l].astype(BF16) for l in range(depth)])
    bp = x_prompt.shape[0]
    dtp = x_prompt.dtype
    yp = x_prompt
    ys = x_sample
    outs_p = []
    outs_s = []
    for l in range(depth):
        yp, st_p = trunk_layer(
            l, yp, diff_attn_prompt,
            jnp.zeros((bp, CONV_WIDTH - 1, GROUP_W), dtp), jnp.zeros((bp, RW_COLS), dtp),
            jnp.zeros((bp, N_HEADS, HEAD_DIM, HEAD_DIM), jnp.float32),
            jnp.zeros((bp, DN_CONV - 1, DN_QKV), dtp),
            jnp.zeros((bp, N_HEADS, HEAD_DIM, HEAD_DIM), jnp.float32), P, W)
        attn_s = functools.partial(diff_attn_sample, past_k=gather_pages(cache_k[l], page_table),
                                   past_v=gather_pages(cache_v[l], page_table))
        ys, st_s = trunk_layer(l, ys, attn_s, state_conv[l], state_rw_shift[l], state_rw_wkv[l],
                               state_dn_conv[l], state_dn_ssm[l], P, W)
        outs_p.append(st_p)
        outs_s.append(st_s)
    stk = lambda outs, i: jnp.stack([o[i] for o in outs])
    return (yp, ys, stk(outs_p, 0), stk(outs_p, 1), stk(outs_s, 0), stk(outs_s, 1),
            stk(outs_p, 2), stk(outs_s, 2), stk(outs_p, 3), stk(outs_s, 3),
            stk(outs_p, 4), stk(outs_s, 4), stk(outs_p, 5), stk(outs_s, 5),
            stk(outs_p, 6), stk(outs_s, 6))
```

```python
import functools
import math

import jax
import jax.numpy as jnp
from jax import lax
from jax.experimental import pallas as pl
from jax.experimental.pallas import tpu as pltpu

D_MODEL = 1024
HEAD_DIM = 64
N_HEADS = 4
GROUP_W = 256
DA_QK = HEAD_DIM // 2
Q_BLOCK = 128
CONV_WIDTH = 31
RW_W_RANK = 64
RW_A_RANK = 64
RW_G_RANK = 128
RW_OFF_W = 3 * GROUP_W
RW_OFF_A = RW_OFF_W + RW_W_RANK
RW_OFF_G = RW_OFF_A + RW_A_RANK
RW_COLS = RW_OFF_G + RW_G_RANK
DN_CONV = 4
DN_CHUNK = 64
DN_QKV = 3 * GROUP_W
DN_COLS = DN_QKV + 2 * N_HEADS + GROUP_W
DA_COLS = 3 * GROUP_W
CV_COLS = 2 * GROUP_W
OFF_CV = DA_COLS
OFF_RW = OFF_CV + CV_COLS
OFF_DN = OFF_RW + RW_COLS
N_IN = OFF_DN + DN_COLS
FFN_HIDDEN = 2816
RMS_EPS = 1e-6
LN_EPS = 1e-5
RW_GN_EPS = 64e-5
NEG_INF = -1e30

LANES = 128
SUBLANES = 8
DN_PAD = DN_QKV + GROUP_W + LANES
DN_HALO = 8
VMEM_LIMIT = 56 * 1024 * 1024

F32 = jnp.float32
BF16 = jnp.bfloat16


def _cparams(*sem):
    return pltpu.CompilerParams(dimension_semantics=sem, vmem_limit_bytes=VMEM_LIMIT)


def _rms(x, g):
    return x * lax.rsqrt(jnp.mean(x * x, axis=-1, keepdims=True) + RMS_EPS) * g


def _row_tile(n, target):
    return target if n % target == 0 else n


def _in_proj_body(x_ref, g_ref, w_ref, q_ref, k_ref, v_ref, cv_ref, rw_ref, dn_ref):
    hn = _rms(x_ref[...], g_ref[...]).astype(BF16)
    off = 0
    for ref in (q_ref, k_ref, v_ref, cv_ref, rw_ref, dn_ref):
        w = ref.shape[-1]
        ref[...] = jnp.dot(hn, w_ref[:, off:off + w], preferred_element_type=F32)
        off += w


def _prep_w_in(w):
    dn = w[:, OFF_DN:]
    dn = jnp.concatenate([dn[:, :DN_QKV], dn[:, DN_QKV + 2 * N_HEADS:], dn[:, DN_QKV:DN_QKV + 2 * N_HEADS],
                          jnp.zeros((w.shape[0], LANES - 2 * N_HEADS), w.dtype)], axis=1)
    return jnp.concatenate([w[:, :OFF_DN], dn], axis=1).astype(BF16)


def in_proj(x, g, w):
    n = x.shape[0]
    tm = _row_tile(n, 512)
    widths = (GROUP_W, GROUP_W, GROUP_W, CV_COLS, RW_COLS, DN_PAD)
    row = lambda wd: pl.BlockSpec((tm, wd), lambda i: (i, 0))
    return pl.pallas_call(
        _in_proj_body,
        grid=(n // tm,),
        in_specs=[row(D_MODEL), pl.BlockSpec((1, D_MODEL), lambda i: (0, 0)),
                  pl.BlockSpec(w.shape, lambda i: (0, 0))],
        out_specs=[row(wd) for wd in widths],
        out_shape=[jax.ShapeDtypeStruct((n, wd), F32) for wd in widths],
        compiler_params=_cparams("parallel"),
        name="in_proj",
    )(x, g.reshape(1, -1), w)


def _out_proj_body(x_ref, a_ref, b_ref, c_ref, d_ref, w_ref, g_ref, o_ref):
    acc = None
    for i, ref in enumerate((a_ref, b_ref, c_ref, d_ref)):
        part = jnp.dot(ref[...].astype(BF16), w_ref[i * GROUP_W:(i + 1) * GROUP_W, :],
                       preferred_element_type=F32)
        acc = part if acc is None else acc + part
    o_ref[...] = x_ref[...] + _rms(acc, g_ref[...])


def out_proj(x, oa, ob, oc, od, w, g):
    n = x.shape[0]
    tm = _row_tile(n, 512)
    row = lambda wd: pl.BlockSpec((tm, wd), lambda i: (i, 0))
    return pl.pallas_call(
        _out_proj_body,
        grid=(n // tm,),
        in_specs=[row(D_MODEL), row(GROUP_W), row(GROUP_W), row(GROUP_W), row(GROUP_W),
                  pl.BlockSpec(w.shape, lambda i: (0, 0)), pl.BlockSpec((1, D_MODEL), lambda i: (0, 0))],
        out_specs=row(D_MODEL),
        out_shape=jax.ShapeDtypeStruct((n, D_MODEL), F32),
        compiler_params=_cparams("parallel"),
        name="out_proj",
    )(x, oa, ob, oc, od, w, g.reshape(1, -1))


def _ffn_body(x_ref, g1_ref, wg_ref, wu_ref, wd_ref, g2_ref, o_ref, hn_ref, acc_ref):
    j = pl.program_id(1)

    @pl.when(j == 0)
    def _():
        hn_ref[...] = _rms(x_ref[...], g1_ref[...]).astype(BF16)
        acc_ref[...] = jnp.zeros_like(acc_ref)

    hn = hn_ref[...]
    gate = jnp.dot(hn, wg_ref[...], preferred_element_type=F32)
    up = jnp.dot(hn, wu_ref[...], preferred_element_type=F32)
    f = (gate * jax.nn.sigmoid(gate) * up).astype(BF16)
    acc_ref[...] += jnp.dot(f, wd_ref[...], preferred_element_type=F32)

    @pl.when(j == pl.num_programs(1) - 1)
    def _():
        o_ref[...] = x_ref[...] + _rms(acc_ref[...], g2_ref[...])


def ffn(x, g1, wg, wu, wd, g2):
    n = x.shape[0]
    tm = _row_tile(n, 512)
    th = FFN_HIDDEN // 2
    return pl.pallas_call(
        _ffn_body,
        grid=(n // tm, FFN_HIDDEN // th),
        in_specs=[pl.BlockSpec((tm, D_MODEL), lambda i, j: (i, 0)),
                  pl.BlockSpec((1, D_MODEL), lambda i, j: (0, 0)),
                  pl.BlockSpec((D_MODEL, th), lambda i, j: (0, j)),
                  pl.BlockSpec((D_MODEL, th), lambda i, j: (0, j)),
                  pl.BlockSpec((th, D_MODEL), lambda i, j: (j, 0)),
                  pl.BlockSpec((1, D_MODEL), lambda i, j: (0, 0))],
        out_specs=pl.BlockSpec((tm, D_MODEL), lambda i, j: (i, 0)),
        out_shape=jax.ShapeDtypeStruct((n, D_MODEL), F32),
        scratch_shapes=[pltpu.VMEM((tm, D_MODEL), BF16), pltpu.VMEM((tm, D_MODEL), F32)],
        compiler_params=_cparams("parallel", "arbitrary"),
        name="ffn",
    )(x, g1.reshape(1, -1), wg, wu, wd, g2.reshape(1, -1))


def _mm(a, b):
    return jnp.dot(a.astype(BF16), b.astype(BF16), preferred_element_type=F32)


def _mm_nt(a, b):
    return lax.dot_general(a.astype(BF16), b.astype(BF16), (((1,), (1,)), ((), ())),
                           preferred_element_type=F32)


def _mm_tn(a, b):
    return lax.dot_general(a.astype(BF16), b.astype(BF16), (((0,), (0,)), ((), ())),
                           preferred_element_type=F32)


def _split3(x):
    hi = x.astype(BF16)
    r1 = x - hi.astype(F32)
    mid = r1.astype(BF16)
    lo = (r1 - mid.astype(F32)).astype(BF16)
    return hi, mid, lo


def _mm_exact_rhs(a01, x):
    a = a01.astype(BF16)
    hi, mid, lo = _split3(x)
    d = lambda y: jnp.dot(a, y, preferred_element_type=F32)
    return d(hi) + (d(mid) + d(lo))


def _mm_exact_lhs(x, a01):
    a = a01.astype(BF16)
    hi, mid, lo = _split3(x)
    d = lambda y: jnp.dot(y, a, preferred_element_type=F32)
    return d(hi) + (d(mid) + d(lo))


def _iota2(n, m):
    return lax.broadcasted_iota(jnp.int32, (n, m), 0), lax.broadcasted_iota(jnp.int32, (n, m), 1)


def _same_block(r, c, blk):
    s = int(math.log2(blk))
    return (r >> s) == (c >> s)


def _unit_lower_inverse(a, r, c, chunk):
    eye = (r == c).astype(F32)
    base = min(8, chunk)
    a8 = jnp.where(_same_block(r, c, base), a, 0.0)
    d = eye + a8
    pw = a8
    for _ in range(max(int(math.log2(base)) - 1, 0)):
        pw = _mm(pw, pw)
        d = d + _mm(d, pw)
    size = base * 2
    while size <= chunk:
        off = jnp.where(_same_block(r, c, size) & ~_same_block(r, c, size // 2), a, 0.0)
        d = d + _mm(d, _mm(off, d))
        size *= 2
    return d


def _softplus(z):
    return jnp.maximum(z, 0.0) + jnp.log(1.0 + jnp.exp(-jnp.abs(z)))


def _sigmoid(z):
    return 1.0 / (1.0 + jnp.exp(-z))


def _head_stackers(chunk):
    hw = N_HEADS * chunk
    lr, lc = _iota2(hw, GROUP_W)
    own = (lr >> int(math.log2(chunk))) == (lc >> int(math.log2(HEAD_DIM)))

    def stack_bd(z):
        return jnp.where(own, jnp.concatenate([z] * N_HEADS, axis=0), 0.0)

    def stack_heads(z):
        return jnp.concatenate([z[:, h * HEAD_DIM:(h + 1) * HEAD_DIM] for h in range(N_HEADS)], axis=0)

    return stack_bd, stack_heads


def _row_to_col(row_vals, r, c):
    n = r.shape[0]
    return jnp.sum(jnp.where(r == c, jnp.broadcast_to(row_vals, (n, n)), 0.0), axis=1, keepdims=True)


def _rwkv_body(p_ref, shift_ref, s0_ref, mu_ref, w0_ref, wup_ref, a0_ref, aup_ref, gup_ref, kk_ref, ka_ref,
               rk_ref, gng_ref, gnb_ref, y_ref, s_out_ref, st_ref, prev_ref, *, chunk, t_valid):
    tb = p_ref.shape[1]
    ti = pl.program_id(1)
    hw = N_HEADS * chunk

    @pl.when(ti == 0)
    def _():
        st_ref[...] = s0_ref[0]
        prev_ref[...] = shift_ref[0]

    x = p_ref[0]
    row = lax.broadcasted_iota(jnp.int32, (tb, 1), 0)
    prev = jnp.where(row == 0, prev_ref[...], pltpu.roll(x, 1, axis=0))
    prev_ref[...] = x[tb - 1:tb, :]
    xm = x + (prev - x) * mu_ref[...]
    r = xm[:, :GROUP_W]
    k = xm[:, GROUP_W:2 * GROUP_W]
    v = xm[:, 2 * GROUP_W:RW_OFF_W]
    wd = xm[:, RW_OFF_W:RW_OFF_A]
    ad = xm[:, RW_OFF_A:RW_OFF_G]
    gd = xm[:, RW_OFF_G:RW_COLS]
    w_log = -_softplus(-(w0_ref[...] + _mm(jnp.tanh(wd), wup_ref[...]))) - 0.5
    logw = -jnp.exp(w_log)
    a = _sigmoid(a0_ref[...] + _mm(ad, aup_ref[...]))
    g = _mm(_sigmoid(gd), gup_ref[...])
    gr, gcl = _iota2(GROUP_W, GROUP_W)
    head_ones = _same_block(gr, gcl, HEAD_DIM).astype(F32)
    kk = k * kk_ref[...]
    kk = kk * lax.rsqrt(_mm_exact_lhs(kk * kk, head_ones) + 1e-6)
    k2 = k * (1.0 + (a - 1.0) * ka_ref[...])
    bonus = _mm_exact_lhs(r * k2 * rk_ref[...], head_ones) * v
    if t_valid < tb:
        ok = row < t_valid
        logw = jnp.where(ok, logw, 0.0)
        kk = jnp.where(ok, kk, 0.0)
        k2 = jnp.where(ok, k2, 0.0)
        v = jnp.where(ok, v, 0.0)
    alpha = -kk
    beta = kk * a
    tr, tc = _iota2(tb, tb)
    cum = (_same_block(tr, tc, chunk) & (tr >= tc)).astype(F32)
    gcum = _mm_exact_rhs(cum, logw)

    sr, sc = _iota2(hw, hw)
    head_blk = _same_block(sr, sc, chunk)
    low_incl = head_blk & (sr >= sc)
    low_strict = head_blk & (sr > sc)
    stack_bd, stack_heads = _head_stackers(chunk)

    st = st_ref[...]
    for c0 in range(0, tb, chunk):
        sl = slice(c0, c0 + chunk)
        gc = gcum[sl]
        g_last = gc[chunk - 1:chunk, :]
        e_neg = jnp.exp(-gc)
        e_rem = jnp.exp(g_last - gc)
        r_t = stack_bd(r[sl] * jnp.exp(gc))
        a_t = stack_bd(alpha[sl] * jnp.exp(gc - logw[sl]))
        k_t = stack_bd(k2[sl] * e_neg)
        b_t = stack_bd(beta[sl] * e_neg)
        vs = stack_heads(v[sl])
        a_ab = jnp.where(low_strict, _mm_nt(a_t, b_t), 0.0)
        a_ak = jnp.where(low_strict, _mm_nt(a_t, k_t), 0.0)
        a_rb = jnp.where(low_incl, _mm_nt(r_t, b_t), 0.0)
        a_rk = jnp.where(low_incl, _mm_nt(r_t, k_t), 0.0)
        tinv = _unit_lower_inverse(a_ab, sr, sc, chunk)
        w_bd = _mm(tinv, a_t)
        u = _mm(tinv, _mm(a_ak, vs)) + _mm(w_bd, st)
        y = _mm(r_t, st) + _mm(a_rk, vs) + _mm(a_rb, u)
        st = (st * _row_to_col(jnp.exp(g_last), gr, gcl)
              + _mm_tn(stack_bd(k2[sl] * e_rem), vs) + _mm_tn(stack_bd(beta[sl] * e_rem), u))
        yc = y - jnp.mean(y, axis=-1, keepdims=True)
        yn = yc * lax.rsqrt(jnp.mean(yc * yc, axis=-1, keepdims=True) + RW_GN_EPS)
        for h in range(N_HEADS):
            y_ref[0, sl, h * HEAD_DIM:(h + 1) * HEAD_DIM] = yn[h * chunk:(h + 1) * chunk, :]
    st_ref[...] = st
    y_ref[0] = (y_ref[0] * gng_ref[...] + gnb_ref[...] + bonus) * g

    @pl.when(ti == pl.num_programs(1) - 1)
    def _():
        s_out_ref[0] = st


def rwkv_mix(p, shift_prev, s0, mu, w0, w_up, a0, a_up, g_up, k_k, k_a, r_k, gn_g, gn_b, *, t_valid):
    b, t, _ = p.shape
    chunk = min(HEAD_DIM, t)
    tb = min(256, t)
    row = lambda z: z.reshape(1, -1)
    st0 = jnp.swapaxes(s0, 2, 3).reshape(b, GROUP_W, HEAD_DIM)
    const = lambda shape: pl.BlockSpec(shape, lambda i, j: (0,) * len(shape))
    y, st = pl.pallas_call(
        functools.partial(_rwkv_body, chunk=chunk, t_valid=t_valid),
        grid=(b, t // tb),
        in_specs=[pl.BlockSpec((1, tb, RW_COLS), lambda i, j: (i, j, 0)),
                  pl.BlockSpec((1, 1, RW_COLS), lambda i, j: (i, 0, 0)),
                  pl.BlockSpec((1, GROUP_W, HEAD_DIM), lambda i, j: (i, 0, 0)),
                  const((1, RW_COLS)), const((1, GROUP_W)), const((RW_W_RANK, GROUP_W)),
                  const((1, GROUP_W)), const((RW_A_RANK, GROUP_W)), const((RW_G_RANK, GROUP_W)),
                  const((1, GROUP_W)), const((1, GROUP_W)), const((1, GROUP_W)),
                  const((1, GROUP_W)), const((1, GROUP_W))],
        out_specs=[pl.BlockSpec((1, tb, GROUP_W), lambda i, j: (i, j, 0)),
                   pl.BlockSpec((1, GROUP_W, HEAD_DIM), lambda i, j: (i, 0, 0))],
        out_shape=[jax.ShapeDtypeStruct((b, t, GROUP_W), F32),
                   jax.ShapeDtypeStruct((b, GROUP_W, HEAD_DIM), F32)],
        scratch_shapes=[pltpu.VMEM((GROUP_W, HEAD_DIM), F32), pltpu.VMEM((1, RW_COLS), F32)],
        compiler_params=_cparams("parallel", "arbitrary"),
        name="rwkv_mix",
    )(p, shift_prev.reshape(b, 1, RW_COLS), st0, row(mu), row(w0), w_up, row(a0), a_up, g_up,
      row(k_k), row(k_a), row(r_k), row(gn_g), row(gn_b))
    return y, jnp.swapaxes(st.reshape(b, N_HEADS, HEAD_DIM, HEAD_DIM), 2, 3)


def _dn_body(p_ref, cprev_ref, s0_ref, cw_ref, alog_ref, dtb_ref, ng_ref, y_ref, s_out_ref, st_ref, hist_ref,
             *, chunk, t_valid):
    tb = p_ref.shape[1]
    ti = pl.program_id(1)
    hw = N_HEADS * chunk

    @pl.when(ti == 0)
    def _():
        st_ref[...] = s0_ref[0]
        hist_ref[...] = cprev_ref[0]

    x = p_ref[0]
    ext = jnp.concatenate([hist_ref[...], x[:, :DN_QKV]], axis=0)
    hist_ref[...] = ext[tb:tb + DN_HALO, :]
    conv = None
    for j in range(DN_CONV):
        lo = DN_HALO - (DN_CONV - 1) + j
        term = ext[lo:lo + tb, :] * cw_ref[j:j + 1, :]
        conv = term if conv is None else conv + term
    qkv = conv * _sigmoid(conv)
    gr, gcl = _iota2(GROUP_W, GROUP_W)
    head_ones = _same_block(gr, gcl, HEAD_DIM).astype(F32)
    q = qkv[:, :GROUP_W]
    k = qkv[:, GROUP_W:2 * GROUP_W]
    v = qkv[:, 2 * GROUP_W:]
    q = q * lax.rsqrt(_mm_exact_lhs(q * q, head_ones) + 1e-6) * (HEAD_DIM ** -0.5)
    k = k * lax.rsqrt(_mm_exact_lhs(k * k, head_ones) + 1e-6)
    z = x[:, DN_QKV:DN_QKV + GROUP_W]
    ab = x[:, DN_QKV + GROUP_W:]
    g_small = -jnp.exp(alog_ref[...]) * _softplus(ab + dtb_ref[...])
    b_small = _sigmoid(ab)
    er, ec = _iota2(LANES, GROUP_W)
    g_w = _mm_exact_lhs(g_small, (er == (ec >> 6)).astype(F32))
    beta = _mm_exact_lhs(b_small, (er == (ec >> 6) + N_HEADS).astype(F32))
    if t_valid < tb:
        ok = lax.broadcasted_iota(jnp.int32, (tb, 1), 0) < t_valid
        g_w = jnp.where(ok, g_w, 0.0)
        beta = jnp.where(ok, beta, 0.0)
        k = jnp.where(ok, k, 0.0)
        v = jnp.where(ok, v, 0.0)
    tr, tc = _iota2(tb, tb)
    cum = (_same_block(tr, tc, chunk) & (tr >= tc)).astype(F32)
    gcum = _mm_exact_rhs(cum, g_w)
    kb = k * beta
    vb = v * beta

    sr, sc = _iota2(hw, hw)
    head_blk = _same_block(sr, sc, chunk)
    low_incl = head_blk & (sr >= sc)
    low_strict = head_blk & (sr > sc)
    eye_hw = sr == sc
    stack_bd, stack_heads = _head_stackers(chunk)

    st = st_ref[...]
    for c0 in range(0, tb, chunk):
        sl = slice(c0, c0 + chunk)
        gc = gcum[sl]
        g_last = gc[chunk - 1:chunk, :]
        gi = jnp.broadcast_to(stack_heads(gc)[:, :1], (hw, hw))
        gj = jnp.sum(jnp.where(eye_hw, gi, 0.0), axis=0, keepdims=True)
        decay = jnp.exp(jnp.where(low_incl, gi - gj, NEG_INF))
        k_bd = stack_bd(k[sl])
        m = jnp.where(low_strict, _mm_nt(stack_bd(kb[sl]), k_bd) * decay, 0.0)
        tinv = _unit_lower_inverse(-m, sr, sc, chunk)
        w_bd = _mm(tinv, stack_bd(kb[sl] * jnp.exp(gc)))
        qk = jnp.where(low_incl, _mm_nt(stack_bd(q[sl]), k_bd) * decay, 0.0)
        v_new = _mm(tinv, stack_heads(vb[sl])) - _mm(w_bd, st)
        o = _mm(stack_bd(q[sl] * jnp.exp(gc)), st) + _mm(qk, v_new)
        st = (st * _row_to_col(jnp.exp(g_last), gr, gcl)
              + _mm_tn(stack_bd(k[sl] * jnp.exp(g_last - gc)), v_new))
        on = o * lax.rsqrt(jnp.mean(o * o, axis=-1, keepdims=True) + RMS_EPS) * ng_ref[...]
        for h in range(N_HEADS):
            y_ref[0, sl, h * HEAD_DIM:(h + 1) * HEAD_DIM] = on[h * chunk:(h + 1) * chunk, :]
    st_ref[...] = st
    y_ref[0] = y_ref[0] * (z * _sigmoid(z))

    @pl.when(ti == pl.num_programs(1) - 1)
    def _():
        s_out_ref[0] = st


def dn_mix(p, conv_prev, s0, conv_w, a_log, dt_bias, norm_g, *, t_valid):
    b, t, _ = p.shape
    chunk = min(DN_CHUNK, t)
    tb = min(256, t)
    hist = jnp.pad(conv_prev, ((0, 0), (DN_HALO - (DN_CONV - 1), 0), (0, 0)))
    lane_pad = lambda z: jnp.pad(z, (0, LANES - z.shape[0])).reshape(1, LANES)
    const = lambda shape: pl.BlockSpec(shape, lambda i, j: (0,) * len(shape))
    y, st = pl.pallas_call(
        functools.partial(_dn_body, chunk=chunk, t_valid=t_valid),
        grid=(b, t // tb),
        in_specs=[pl.BlockSpec((1, tb, DN_PAD), lambda i, j: (i, j, 0)),
                  pl.BlockSpec((1, DN_HALO, DN_QKV), lambda i, j: (i, 0, 0)),
                  pl.BlockSpec((1, GROUP_W, HEAD_DIM), lambda i, j: (i, 0, 0)),
                  const((DN_CONV, DN_QKV)), const((1, LANES)), const((1, LANES)), const((1, HEAD_DIM))],
        out_specs=[pl.BlockSpec((1, tb, GROUP_W), lambda i, j: (i, j, 0)),
                   pl.BlockSpec((1, GROUP_W, HEAD_DIM), lambda i, j: (i, 0, 0))],
        out_shape=[jax.ShapeDtypeStruct((b, t, GROUP_W), F32),
                   jax.ShapeDtypeStruct((b, GROUP_W, HEAD_DIM), F32)],
        scratch_shapes=[pltpu.VMEM((GROUP_W, HEAD_DIM), F32), pltpu.VMEM((DN_HALO, DN_QKV), F32)],
        compiler_params=_cparams("parallel", "arbitrary"),
        name="dn_mix",
    )(p, hist, s0.reshape(b, GROUP_W, HEAD_DIM), conv_w, lane_pad(a_log), lane_pad(dt_bias),
      norm_g.reshape(1, HEAD_DIM))
    return y, st.reshape(b, N_HEADS, HEAD_DIM, HEAD_DIM)


def rms_norm(x, g, eps=RMS_EPS):
    xf = x.astype(jnp.float32)
    y = xf * lax.rsqrt(jnp.mean(xf * xf, axis=-1, keepdims=True) + eps)
    return (y * g.astype(jnp.float32)).astype(x.dtype)


def layer_norm(x, g, b, eps):
    xf = x.astype(jnp.float32)
    mu = jnp.mean(xf, axis=-1, keepdims=True)
    var = jnp.mean(jnp.square(xf - mu), axis=-1, keepdims=True)
    return (xf - mu) * lax.rsqrt(var + eps) * g + b


def causal_depthwise_conv(x_ext, w):
    c = x_ext.shape[-1]
    return lax.conv_general_dilated(x_ext, w[:, None, :].astype(x_ext.dtype), (1,), 'VALID',
                                    dimension_numbers=('NWC', 'WIO', 'NWC'), feature_group_count=c)


def alibi_slopes():
    return 2.0 ** (-8.0 * jnp.arange(1, N_HEADS + 1, dtype=jnp.float32) / N_HEADS)


def diff_attn_core(q, k, v, q_pos, k_pos, lam):
    qf = q.astype(jnp.float32) * (DA_QK ** -0.5)
    kf = k.astype(jnp.float32)
    dist = (q_pos[:, None] - k_pos[None, :]).astype(jnp.float32)
    bias = jnp.where(dist[None] >= 0, -alibi_slopes()[:, None, None] * dist[None], NEG_INF)
    s1 = jnp.einsum('bqhd,bkhd->bhqk', qf[..., :DA_QK], kf[..., :DA_QK]) + bias
    s2 = jnp.einsum('bqhd,bkhd->bhqk', qf[..., DA_QK:], kf[..., DA_QK:]) + bias
    a = jax.nn.softmax(s1, axis=-1) - lam * jax.nn.softmax(s2, axis=-1)
    return jnp.einsum('bhqk,bkhd->bqhd', a, v.astype(jnp.float32))


def diff_attn_prompt(q, k, v, lam):
    b, t = q.shape[:2]
    nb = t // Q_BLOCK
    pos = jnp.arange(t)
    qb = jnp.moveaxis(q.reshape(b, nb, Q_BLOCK, N_HEADS, HEAD_DIM), 1, 0)
    pb = pos.reshape(nb, Q_BLOCK)
    ob = lax.map(lambda a: diff_attn_core(a[0], k, v, a[1], pos, lam), (qb, pb))
    return jnp.moveaxis(ob, 0, 1).reshape(b, t, N_HEADS, HEAD_DIM)


def diff_attn_sample(q, k, v, lam, past_k, past_v):
    past = past_k.shape[1]
    t = q.shape[1]
    k_all = jnp.concatenate([past_k.astype(k.dtype), k], axis=1)
    v_all = jnp.concatenate([past_v.astype(v.dtype), v], axis=1)
    return diff_attn_core(q, k_all, v_all, past + jnp.arange(t), jnp.arange(past + t), lam)


def gather_pages(cache_l, page_table):
    pages = cache_l[page_table]
    return pages.reshape(page_table.shape[0], -1, *cache_l.shape[2:])


def conformer_conv_mix(p, buf, conv_w, conv_b, ln_g, ln_b):
    glu = p[..., :GROUP_W] * jax.nn.sigmoid(p[..., GROUP_W:])
    ext = jnp.concatenate([buf.astype(glu.dtype), glu], axis=1)
    h = causal_depthwise_conv(ext, conv_w) + conv_b
    h = jax.nn.silu(layer_norm(h, ln_g, ln_b, LN_EPS))
    return h, ext[:, -(CONV_WIDTH - 1):]


def trunk_layer(l, x, attn_fn, cv_buf, rw_shift, rw_S, dn_buf, dn_S, P, W):
    b, t, _ = x.shape
    n = b * t
    x2 = x.reshape(n, D_MODEL)
    q, k, v, p_cv, p_rw, p_dn = in_proj(x2, P['g_pre_mix'][l], W['w_in'][l])
    hs = lambda z: z.reshape(b, t, N_HEADS, HEAD_DIM)
    q, k, v = hs(q), hs(k), hs(v)
    lam_init = 0.8 - 0.6 * math.exp(-0.3 * l)
    lam = (jnp.exp(jnp.sum(P['da_lq1'][l] * P['da_lk1'][l]))
           - jnp.exp(jnp.sum(P['da_lq2'][l] * P['da_lk2'][l])) + lam_init).astype(jnp.float32)
    o_a = attn_fn(q, k, v, lam)
    o_a = (rms_norm(o_a, P['da_norm_g'][l]) * (1.0 - lam_init)).reshape(n, GROUP_W)
    o_b, cv_new = conformer_conv_mix(p_cv.reshape(b, t, CV_COLS), cv_buf, P['cv_w'][l], P['cv_b'][l],
                                     P['cv_ln_g'][l], P['cv_ln_b'][l])
    t_pad = -(-t // SUBLANES) * SUBLANES
    pad_t = lambda z: z if t_pad == t else jnp.pad(z, ((0, 0), (0, t_pad - t), (0, 0)))
    p_rw = p_rw.reshape(b, t, RW_COLS)
    o_c, rw_S_new = rwkv_mix(
        pad_t(p_rw), rw_shift, rw_S, P['rw_mu'][l], P['rw_w0'][l], P['rw_w_up'][l],
        P['rw_a0'][l], P['rw_a_up'][l], P['rw_g_up'][l], P['rw_k_k'][l], P['rw_k_a'][l],
        P['rw_r_k'][l].reshape(-1), P['rw_gn_g'][l], P['rw_gn_b'][l], t_valid=t)
    o_c = o_c[:, :t]
    rw_shift_new = p_rw[:, -1]
    p_dn = p_dn.reshape(b, t, DN_PAD)
    o_d, dn_S_new = dn_mix(pad_t(p_dn), dn_buf, dn_S, P['dn_conv_w'][l], P['dn_A_log'][l],
                           P['dn_dt_bias'][l], P['dn_norm_g'][l], t_valid=t)
    o_d = o_d[:, :t]
    keep = DN_CONV - 1
    dn_buf_new = (p_dn[:, t - keep:, :DN_QKV] if t >= keep else
                  jnp.concatenate([dn_buf, p_dn[..., :DN_QKV]], axis=1)[:, -keep:])
    x2 = out_proj(x2, o_a, o_b.reshape(n, GROUP_W), o_c.reshape(n, GROUP_W), o_d.reshape(n, GROUP_W),
                  W['w_out'][l], P['g_post_mix'][l])
    x2 = ffn(x2, P['g_pre_ffn'][l], W['ffn_w_gate'][l], W['ffn_w_up'][l], W['ffn_w_down'][l],
             P['g_post_ffn'][l])
    return x2.reshape(b, t, D_MODEL), (k, v, cv_new, rw_shift_new, rw_S_new, dn_buf_new, dn_S_new)


def kernel(x_prompt, x_sample, cache_k, cache_v, page_table, state_conv, state_rw_shift,
           state_rw_wkv, state_dn_conv, state_dn_ssm, g_pre_mix, g_post_mix, g_pre_ffn,
           g_post_ffn, w_in, w_out, da_lq1, da_lk1, da_lq2, da_lk2, da_norm_g, cv_w, cv_b,
           cv_ln_g, cv_ln_b, rw_mu, rw_w0, rw_w_up, rw_a0, rw_a_up, rw_g_up, rw_k_k, rw_k_a,
           rw_r_k, rw_gn_g, rw_gn_b, dn_conv_w, dn_A_log, dn_dt_bias, dn_norm_g,
           ffn_w_gate, ffn_w_up, ffn_w_down):
    P = dict(g_pre_mix=g_pre_mix, g_post_mix=g_post_mix, g_pre_ffn=g_pre_ffn, g_post_ffn=g_post_ffn,
             da_lq1=da_lq1, da_lk1=da_lk1, da_lq2=da_lq2, da_lk2=da_lk2,
             da_norm_g=da_norm_g, cv_w=cv_w, cv_b=cv_b, cv_ln_g=cv_ln_g, cv_ln_b=cv_ln_b,
             rw_mu=rw_mu, rw_w0=rw_w0, rw_w_up=rw_w_up, rw_a0=rw_a0, rw_a_up=rw_a_up,
             rw_g_up=rw_g_up, rw_k_k=rw_k_k, rw_k_a=rw_k_a, rw_r_k=rw_r_k, rw_gn_g=rw_gn_g,
             rw_gn_b=rw_gn_b, dn_conv_w=dn_conv_w, dn_A_log=dn_A_log, dn_dt_bias=dn_dt_bias,
             dn_norm_g=dn_norm_g)
    depth = w_in.shape[0]
    W = dict(w_in=[_prep_w_in(w_in[l]) for l in range(depth)],
             w_out=[w_out[l].astype(BF16) for l in range(depth)],
             ffn_w_gate=[ffn_w_gate[l].astype(BF16) for l in range(depth)],
             ffn_w_up=[ffn_w_up[l].astype(BF16) for l in range(depth)],
             ffn_w_down=[ffn_w_down[l].astype(BF16) for l in range(depth)])
    bp = x_prompt.shape[0]
    dtp = x_prompt.dtype
    yp = x_prompt
    ys = x_sample
    outs_p = []
    outs_s = []
    for l in range(depth):
        yp, st_p = trunk_layer(
            l, yp, diff_attn_prompt,
            jnp.zeros((bp, CONV_WIDTH - 1, GROUP_W), dtp), jnp.zeros((bp, RW_COLS), dtp),
            jnp.zeros((bp, N_HEADS, HEAD_DIM, HEAD_DIM), jnp.float32),
            jnp.zeros((bp, DN_CONV - 1, DN_QKV), dtp),
            jnp.zeros((bp, N_HEADS, HEAD_DIM, HEAD_DIM), jnp.float32), P, W)
        attn_s = functools.partial(diff_attn_sample, past_k=gather_pages(cache_k[l], page_table),
                                   past_v=gather_pages(cache_v[l], page_table))
        ys, st_s = trunk_layer(l, ys, attn_s, state_conv[l], state_rw_shift[l], state_rw_wkv[l],
                               state_dn_conv[l], state_dn_ssm[l], P, W)
        outs_p.append(st_p)
        outs_s.append(st_s)
    stk = lambda outs, i: jnp.stack([o[i] for o in outs])
    return (yp, ys, stk(outs_p, 0), stk(outs_p, 1), stk(outs_s, 0), stk(outs_s, 1),
            stk(outs_p, 2), stk(outs_s, 2), stk(outs_p, 3), stk(outs_s, 3),
            stk(outs_p, 4), stk(outs_s, 4), stk(outs_p, 5), stk(outs_s, 5),
            stk(outs_p, 6), stk(outs_s, 6))
```

```python
import functools
import math

import jax
import jax.numpy as jnp
from jax import lax
from jax.experimental import pallas as pl
from jax.experimental.pallas import tpu as pltpu

D_MODEL = 1024
HEAD_DIM = 64
N_HEADS = 4
GROUP_W = 256
DA_QK = HEAD_DIM // 2
Q_BLOCK = 128
CONV_WIDTH = 31
RW_W_RANK = 64
RW_A_RANK = 64
RW_G_RANK = 128
RW_OFF_W = 3 * GROUP_W
RW_OFF_A = RW_OFF_W + RW_W_RANK
RW_OFF_G = RW_OFF_A + RW_A_RANK
RW_COLS = RW_OFF_G + RW_G_RANK
DN_CONV = 4
DN_CHUNK = 64
DN_QKV = 3 * GROUP_W
DN_COLS = DN_QKV + 2 * N_HEADS + GROUP_W
DA_COLS = 3 * GROUP_W
CV_COLS = 2 * GROUP_W
OFF_CV = DA_COLS
OFF_RW = OFF_CV + CV_COLS
OFF_DN = OFF_RW + RW_COLS
N_IN = OFF_DN + DN_COLS
FFN_HIDDEN = 2816
RMS_EPS = 1e-6
LN_EPS = 1e-5
RW_GN_EPS = 64e-5
NEG_INF = -1e30
LOG2E = math.log2(math.e)

LANES = 128
SUBLANES = 8
DN_PAD = DN_QKV + GROUP_W + LANES
DN_HALO = 8
CV_HALO = 32
N_BR = 2 * N_HEADS
AT_TQ = 256
AT_TK = 256
PG_PAGES = 16
VMEM_LIMIT = 56 * 1024 * 1024

F32 = jnp.float32
BF16 = jnp.bfloat16


def _cparams(*sem):
    return pltpu.CompilerParams(dimension_semantics=sem, vmem_limit_bytes=VMEM_LIMIT)


def _rms(x, g):
    return x * lax.rsqrt(jnp.mean(x * x, axis=-1, keepdims=True) + RMS_EPS) * g


def _row_tile(n, target):
    return target if n % target == 0 else n


def _in_proj_body(x_ref, g_ref, w_ref, q_ref, k_ref, v_ref, cv_ref, rw_ref, dn_ref):
    hn = _rms(x_ref[...], g_ref[...]).astype(BF16)
    off = 0
    for ref in (q_ref, k_ref, v_ref, cv_ref, rw_ref, dn_ref):
        w = ref.shape[-1]
        ref[...] = jnp.dot(hn, w_ref[:, off:off + w], preferred_element_type=F32)
        off += w


def _prep_w_in(w):
    dn = w[:, OFF_DN:]
    dn = jnp.concatenate([dn[:, :DN_QKV], dn[:, DN_QKV + 2 * N_HEADS:], dn[:, DN_QKV:DN_QKV + 2 * N_HEADS],
                          jnp.zeros((w.shape[0], LANES - 2 * N_HEADS), w.dtype)], axis=1)
    return jnp.concatenate([w[:, :OFF_DN], dn], axis=1).astype(BF16)


def in_proj(x, g, w):
    n = x.shape[0]
    tm = _row_tile(n, 512)
    widths = (GROUP_W, GROUP_W, GROUP_W, CV_COLS, RW_COLS, DN_PAD)
    row = lambda wd: pl.BlockSpec((tm, wd), lambda i: (i, 0))
    return pl.pallas_call(
        _in_proj_body,
        grid=(n // tm,),
        in_specs=[row(D_MODEL), pl.BlockSpec((1, D_MODEL), lambda i: (0, 0)),
                  pl.BlockSpec(w.shape, lambda i: (0, 0))],
        out_specs=[row(wd) for wd in widths],
        out_shape=[jax.ShapeDtypeStruct((n, wd), F32) for wd in widths],
        compiler_params=_cparams("parallel"),
        name="in_proj",
    )(x, g.reshape(1, -1), w)


def _out_proj_body(x_ref, a_ref, b_ref, c_ref, d_ref, w_ref, g_ref, o_ref):
    acc = None
    for i, ref in enumerate((a_ref, b_ref, c_ref, d_ref)):
        part = jnp.dot(ref[...].astype(BF16), w_ref[i * GROUP_W:(i + 1) * GROUP_W, :],
                       preferred_element_type=F32)
        acc = part if acc is None else acc + part
    o_ref[...] = x_ref[...] + _rms(acc, g_ref[...])


def out_proj(x, oa, ob, oc, od, w, g):
    n = x.shape[0]
    tm = _row_tile(n, 512)
    row = lambda wd: pl.BlockSpec((tm, wd), lambda i: (i, 0))
    return pl.pallas_call(
        _out_proj_body,
        grid=(n // tm,),
        in_specs=[row(D_MODEL), row(GROUP_W), row(GROUP_W), row(GROUP_W), row(GROUP_W),
                  pl.BlockSpec(w.shape, lambda i: (0, 0)), pl.BlockSpec((1, D_MODEL), lambda i: (0, 0))],
        out_specs=row(D_MODEL),
        out_shape=jax.ShapeDtypeStruct((n, D_MODEL), F32),
        compiler_params=_cparams("parallel"),
        name="out_proj",
    )(x, oa, ob, oc, od, w, g.reshape(1, -1))


def _ffn_body(x_ref, g1_ref, wg_ref, wu_ref, wd_ref, g2_ref, o_ref, hn_ref, acc_ref):
    j = pl.program_id(1)

    @pl.when(j == 0)
    def _():
        hn_ref[...] = _rms(x_ref[...], g1_ref[...]).astype(BF16)
        acc_ref[...] = jnp.zeros_like(acc_ref)

    hn = hn_ref[...]
    gate = jnp.dot(hn, wg_ref[...], preferred_element_type=F32)
    up = jnp.dot(hn, wu_ref[...], preferred_element_type=F32)
    f = (gate * jax.nn.sigmoid(gate) * up).astype(BF16)
    acc_ref[...] += jnp.dot(f, wd_ref[...], preferred_element_type=F32)

    @pl.when(j == pl.num_programs(1) - 1)
    def _():
        o_ref[...] = x_ref[...] + _rms(acc_ref[...], g2_ref[...])


def ffn(x, g1, wg, wu, wd, g2):
    n = x.shape[0]
    tm = _row_tile(n, 512)
    th = FFN_HIDDEN // 2
    return pl.pallas_call(
        _ffn_body,
        grid=(n // tm, FFN_HIDDEN // th),
        in_specs=[pl.BlockSpec((tm, D_MODEL), lambda i, j: (i, 0)),
                  pl.BlockSpec((1, D_MODEL), lambda i, j: (0, 0)),
                  pl.BlockSpec((D_MODEL, th), lambda i, j: (0, j)),
                  pl.BlockSpec((D_MODEL, th), lambda i, j: (0, j)),
                  pl.BlockSpec((th, D_MODEL), lambda i, j: (j, 0)),
                  pl.BlockSpec((1, D_MODEL), lambda i, j: (0, 0))],
        out_specs=pl.BlockSpec((tm, D_MODEL), lambda i, j: (i, 0)),
        out_shape=jax.ShapeDtypeStruct((n, D_MODEL), F32),
        scratch_shapes=[pltpu.VMEM((tm, D_MODEL), BF16), pltpu.VMEM((tm, D_MODEL), F32)],
        compiler_params=_cparams("parallel", "arbitrary"),
        name="ffn",
    )(x, g1.reshape(1, -1), wg, wu, wd, g2.reshape(1, -1))


def _mm(a, b):
    return jnp.dot(a.astype(BF16), b.astype(BF16), preferred_element_type=F32)


def _mm_nt(a, b):
    return lax.dot_general(a.astype(BF16), b.astype(BF16), (((1,), (1,)), ((), ())),
                           preferred_element_type=F32)


def _mm_tn(a, b):
    return lax.dot_general(a.astype(BF16), b.astype(BF16), (((0,), (0,)), ((), ())),
                           preferred_element_type=F32)


def _split3(x):
    hi = x.astype(BF16)
    r1 = x - hi.astype(F32)
    mid = r1.astype(BF16)
    lo = (r1 - mid.astype(F32)).astype(BF16)
    return hi, mid, lo


def _mm_exact_rhs(a01, x):
    a = a01.astype(BF16)
    hi, mid, lo = _split3(x)
    d = lambda y: jnp.dot(a, y, preferred_element_type=F32)
    return d(hi) + (d(mid) + d(lo))


def _mm_exact_lhs(x, a01):
    a = a01.astype(BF16)
    hi, mid, lo = _split3(x)
    d = lambda y: jnp.dot(y, a, preferred_element_type=F32)
    return d(hi) + (d(mid) + d(lo))


def _iota2(n, m):
    return lax.broadcasted_iota(jnp.int32, (n, m), 0), lax.broadcasted_iota(jnp.int32, (n, m), 1)


def _same_block(r, c, blk):
    s = int(math.log2(blk))
    return (r >> s) == (c >> s)


def _unit_lower_inverse(mats, r, c, chunk):
    eye = (r == c).astype(F32)
    base = min(8, chunk)
    blk = _same_block(r, c, base)
    pw = [jnp.where(blk, a, 0.0) for a in mats]
    d = [eye + p for p in pw]
    for _ in range(max(int(math.log2(base)) - 1, 0)):
        pw = [_mm(p, p) for p in pw]
        d = [x + _mm(x, p) for x, p in zip(d, pw)]
    size = base * 2
    while size <= chunk:
        ring = _same_block(r, c, size) & ~_same_block(r, c, size // 2)
        t = [_mm(jnp.where(ring, a, 0.0), x) for a, x in zip(mats, d)]
        d = [x + _mm(x, y) for x, y in zip(d, t)]
        size *= 2
    return d


def _softplus(z):
    return jnp.maximum(z, 0.0) + jnp.log(1.0 + jnp.exp(-jnp.abs(z)))


def _sigmoid(z):
    return 1.0 / (1.0 + jnp.exp(-z))


def _head_stackers(chunk):
    hw = N_HEADS * chunk
    lr, lc = _iota2(hw, GROUP_W)
    own = (lr >> int(math.log2(chunk))) == (lc >> int(math.log2(HEAD_DIM)))

    def stack_bd(z):
        return jnp.where(own, jnp.concatenate([z] * N_HEADS, axis=0), 0.0)

    def stack_heads(z):
        return jnp.concatenate([z[:, h * HEAD_DIM:(h + 1) * HEAD_DIM] for h in range(N_HEADS)], axis=0)

    return stack_bd, stack_heads


def _row_to_col(row_vals, r, c):
    n = r.shape[0]
    return jnp.sum(jnp.where(r == c, jnp.broadcast_to(row_vals, (n, n)), 0.0), axis=1, keepdims=True)


def _rwkv_body(p_ref, shift_ref, s0_ref, mu_ref, w0_ref, wup_ref, a0_ref, aup_ref, gup_ref, kk_ref, ka_ref,
               rk_ref, gng_ref, gnb_ref, y_ref, s_out_ref, st_ref, prev_ref, *, chunk, t_valid):
    tb = p_ref.shape[1]
    ti = pl.program_id(1)
    hw = N_HEADS * chunk

    @pl.when(ti == 0)
    def _():
        st_ref[...] = s0_ref[0]
        prev_ref[...] = shift_ref[0]

    x = p_ref[0]
    row = lax.broadcasted_iota(jnp.int32, (tb, 1), 0)
    prev = jnp.where(row == 0, prev_ref[...], pltpu.roll(x, 1, axis=0))
    prev_ref[...] = x[tb - 1:tb, :]
    xm = x + (prev - x) * mu_ref[...]
    r = xm[:, :GROUP_W]
    k = xm[:, GROUP_W:2 * GROUP_W]
    v = xm[:, 2 * GROUP_W:RW_OFF_W]
    wd = xm[:, RW_OFF_W:RW_OFF_A]
    ad = xm[:, RW_OFF_A:RW_OFF_G]
    gd = xm[:, RW_OFF_G:RW_COLS]
    w_log = -_softplus(-(w0_ref[...] + _mm(jnp.tanh(wd), wup_ref[...]))) - 0.5
    logw = -jnp.exp(w_log)
    a = _sigmoid(a0_ref[...] + _mm(ad, aup_ref[...]))
    g = _mm(_sigmoid(gd), gup_ref[...])
    gr, gcl = _iota2(GROUP_W, GROUP_W)
    head_ones = _same_block(gr, gcl, HEAD_DIM).astype(F32)
    kk = k * kk_ref[...]
    kk = kk * lax.rsqrt(_mm_exact_lhs(kk * kk, head_ones) + 1e-6)
    k2 = k * (1.0 + (a - 1.0) * ka_ref[...])
    bonus = _mm_exact_lhs(r * k2 * rk_ref[...], head_ones) * v
    if t_valid < tb:
        ok = row < t_valid
        logw = jnp.where(ok, logw, 0.0)
        kk = jnp.where(ok, kk, 0.0)
        k2 = jnp.where(ok, k2, 0.0)
        v = jnp.where(ok, v, 0.0)
    alpha = -kk
    beta = kk * a
    tr, tc = _iota2(tb, tb)
    cum = (_same_block(tr, tc, chunk) & (tr >= tc)).astype(F32)
    gcum = _mm_exact_rhs(cum, logw)

    sr, sc = _iota2(hw, hw)
    head_blk = _same_block(sr, sc, chunk)
    low_incl = head_blk & (sr >= sc)
    low_strict = head_blk & (sr > sc)
    stack_bd, stack_heads = _head_stackers(chunk)

    slices = [slice(c0, c0 + chunk) for c0 in range(0, tb, chunk)]
    pre = []
    for sl in slices:
        gc = gcum[sl]
        g_last = gc[chunk - 1:chunk, :]
        e_neg = jnp.exp(-gc)
        e_rem = jnp.exp(g_last - gc)
        r_t = stack_bd(r[sl] * jnp.exp(gc))
        a_t = stack_bd(alpha[sl] * jnp.exp(gc - logw[sl]))
        k_t = stack_bd(k2[sl] * e_neg)
        b_t = stack_bd(beta[sl] * e_neg)
        vs = stack_heads(v[sl])
        pre.append(dict(
            r_t=r_t, a_t=a_t, vs=vs,
            a_ab=jnp.where(low_strict, _mm_nt(a_t, b_t), 0.0),
            a_ak_v=_mm(jnp.where(low_strict, _mm_nt(a_t, k_t), 0.0), vs),
            a_rb=jnp.where(low_incl, _mm_nt(r_t, b_t), 0.0),
            a_rk_v=_mm(jnp.where(low_incl, _mm_nt(r_t, k_t), 0.0), vs),
            kv=_mm_tn(stack_bd(k2[sl] * e_rem), vs), b_hat=stack_bd(beta[sl] * e_rem),
            p_col=_row_to_col(jnp.exp(g_last), gr, gcl)))
    tinvs = _unit_lower_inverse([c["a_ab"] for c in pre], sr, sc, chunk)
    for c, tinv in zip(pre, tinvs):
        c["w_bd"] = _mm(tinv, c["a_t"])
        c["u_loc"] = _mm(tinv, c["a_ak_v"])
    st = st_ref[...]
    for sl, c in zip(slices, pre):
        u = c["u_loc"] + _mm(c["w_bd"], st)
        y = _mm(c["r_t"], st) + c["a_rk_v"] + _mm(c["a_rb"], u)
        st = st * c["p_col"] + c["kv"] + _mm_tn(c["b_hat"], u)
        yc = y - jnp.mean(y, axis=-1, keepdims=True)
        yn = yc * lax.rsqrt(jnp.mean(yc * yc, axis=-1, keepdims=True) + RW_GN_EPS)
        for h in range(N_HEADS):
            y_ref[0, sl, h * HEAD_DIM:(h + 1) * HEAD_DIM] = yn[h * chunk:(h + 1) * chunk, :]
    st_ref[...] = st
    y_ref[0] = (y_ref[0] * gng_ref[...] + gnb_ref[...] + bonus) * g

    @pl.when(ti == pl.num_programs(1) - 1)
    def _():
        s_out_ref[0] = st


def rwkv_mix(p, shift_prev, s0, mu, w0, w_up, a0, a_up, g_up, k_k, k_a, r_k, gn_g, gn_b, *, t_valid):
    b, t, _ = p.shape
    chunk = min(HEAD_DIM, t)
    tb = min(256, t)
    row = lambda z: z.reshape(1, -1)
    st0 = jnp.swapaxes(s0, 2, 3).reshape(b, GROUP_W, HEAD_DIM)
    const = lambda shape: pl.BlockSpec(shape, lambda i, j: (0,) * len(shape))
    y, st = pl.pallas_call(
        functools.partial(_rwkv_body, chunk=chunk, t_valid=t_valid),
        grid=(b, t // tb),
        in_specs=[pl.BlockSpec((1, tb, RW_COLS), lambda i, j: (i, j, 0)),
                  pl.BlockSpec((1, 1, RW_COLS), lambda i, j: (i, 0, 0)),
                  pl.BlockSpec((1, GROUP_W, HEAD_DIM), lambda i, j: (i, 0, 0)),
                  const((1, RW_COLS)), const((1, GROUP_W)), const((RW_W_RANK, GROUP_W)),
                  const((1, GROUP_W)), const((RW_A_RANK, GROUP_W)), const((RW_G_RANK, GROUP_W)),
                  const((1, GROUP_W)), const((1, GROUP_W)), const((1, GROUP_W)),
                  const((1, GROUP_W)), const((1, GROUP_W))],
        out_specs=[pl.BlockSpec((1, tb, GROUP_W), lambda i, j: (i, j, 0)),
                   pl.BlockSpec((1, GROUP_W, HEAD_DIM), lambda i, j: (i, 0, 0))],
        out_shape=[jax.ShapeDtypeStruct((b, t, GROUP_W), F32),
                   jax.ShapeDtypeStruct((b, GROUP_W, HEAD_DIM), F32)],
        scratch_shapes=[pltpu.VMEM((GROUP_W, HEAD_DIM), F32), pltpu.VMEM((1, RW_COLS), F32)],
        compiler_params=_cparams("parallel", "arbitrary"),
        name="rwkv_mix",
    )(p, shift_prev.reshape(b, 1, RW_COLS), st0, row(mu), row(w0), w_up, row(a0), a_up, g_up,
      row(k_k), row(k_a), row(r_k), row(gn_g), row(gn_b))
    return y, jnp.swapaxes(st.reshape(b, N_HEADS, HEAD_DIM, HEAD_DIM), 2, 3)


def _dn_body(p_ref, cprev_ref, s0_ref, cw_ref, alog_ref, dtb_ref, ng_ref, y_ref, s_out_ref, st_ref, hist_ref,
             *, chunk, t_valid):
    tb = p_ref.shape[1]
    ti = pl.program_id(1)
    hw = N_HEADS * chunk

    @pl.when(ti == 0)
    def _():
        st_ref[...] = s0_ref[0]
        hist_ref[...] = cprev_ref[0]

    x = p_ref[0]
    ext = jnp.concatenate([hist_ref[...], x[:, :DN_QKV]], axis=0)
    hist_ref[...] = ext[tb:tb + DN_HALO, :]
    conv = None
    for j in range(DN_CONV):
        lo = DN_HALO - (DN_CONV - 1) + j
        term = ext[lo:lo + tb, :] * cw_ref[j:j + 1, :]
        conv = term if conv is None else conv + term
    qkv = conv * _sigmoid(conv)
    gr, gcl = _iota2(GROUP_W, GROUP_W)
    head_ones = _same_block(gr, gcl, HEAD_DIM).astype(F32)
    q = qkv[:, :GROUP_W]
    k = qkv[:, GROUP_W:2 * GROUP_W]
    v = qkv[:, 2 * GROUP_W:]
    q = q * lax.rsqrt(_mm_exact_lhs(q * q, head_ones) + 1e-6) * (HEAD_DIM ** -0.5)
    k = k * lax.rsqrt(_mm_exact_lhs(k * k, head_ones) + 1e-6)
    z = x[:, DN_QKV:DN_QKV + GROUP_W]
    ab = x[:, DN_QKV + GROUP_W:]
    g_small = -jnp.exp(alog_ref[...]) * _softplus(ab + dtb_ref[...])
    b_small = _sigmoid(ab)
    er, ec = _iota2(LANES, GROUP_W)
    g_w = _mm_exact_lhs(g_small, (er == (ec >> 6)).astype(F32))
    beta = _mm_exact_lhs(b_small, (er == (ec >> 6) + N_HEADS).astype(F32))
    if t_valid < tb:
        ok = lax.broadcasted_iota(jnp.int32, (tb, 1), 0) < t_valid
        g_w = jnp.where(ok, g_w, 0.0)
        beta = jnp.where(ok, beta, 0.0)
        k = jnp.where(ok, k, 0.0)
        v = jnp.where(ok, v, 0.0)
    tr, tc = _iota2(tb, tb)
    cum = (_same_block(tr, tc, chunk) & (tr >= tc)).astype(F32)
    gcum = _mm_exact_rhs(cum, g_w)
    kb = k * beta
    vb = v * beta

    sr, sc = _iota2(hw, hw)
    head_blk = _same_block(sr, sc, chunk)
    low_incl = head_blk & (sr >= sc)
    low_strict = head_blk & (sr > sc)
    eye_hw = sr == sc
    stack_bd, stack_heads = _head_stackers(chunk)

    slices = [slice(c0, c0 + chunk) for c0 in range(0, tb, chunk)]
    pre = []
    for sl in slices:
        gc = gcum[sl]
        g_last = gc[chunk - 1:chunk, :]
        gi = jnp.broadcast_to(stack_heads(gc)[:, :1], (hw, hw))
        gj = jnp.sum(jnp.where(eye_hw, gi, 0.0), axis=0, keepdims=True)
        decay = jnp.exp(jnp.where(low_incl, gi - gj, NEG_INF))
        k_bd = stack_bd(k[sl])
        pre.append(dict(
            m=jnp.where(low_strict, _mm_nt(stack_bd(kb[sl]), k_bd) * decay, 0.0),
            qk=jnp.where(low_incl, _mm_nt(stack_bd(q[sl]), k_bd) * decay, 0.0),
            kb_g=stack_bd(kb[sl] * jnp.exp(gc)), q_g=stack_bd(q[sl] * jnp.exp(gc)), vb=stack_heads(vb[sl]),
            k_rem=stack_bd(k[sl] * jnp.exp(g_last - gc)), p_col=_row_to_col(jnp.exp(g_last), gr, gcl)))
    tinvs = _unit_lower_inverse([-c["m"] for c in pre], sr, sc, chunk)
    for c, tinv in zip(pre, tinvs):
        c["w_bd"] = _mm(tinv, c["kb_g"])
        c["u_loc"] = _mm(tinv, c["vb"])
    st = st_ref[...]
    for sl, c in zip(slices, pre):
        v_new = c["u_loc"] - _mm(c["w_bd"], st)
        o = _mm(c["q_g"], st) + _mm(c["qk"], v_new)
        st = st * c["p_col"] + _mm_tn(c["k_rem"], v_new)
        on = o * lax.rsqrt(jnp.mean(o * o, axis=-1, keepdims=True) + RMS_EPS) * ng_ref[...]
        for h in range(N_HEADS):
            y_ref[0, sl, h * HEAD_DIM:(h + 1) * HEAD_DIM] = on[h * chunk:(h + 1) * chunk, :]
    st_ref[...] = st
    y_ref[0] = y_ref[0] * (z * _sigmoid(z))

    @pl.when(ti == pl.num_programs(1) - 1)
    def _():
        s_out_ref[0] = st


def dn_mix(p, conv_prev, s0, conv_w, a_log, dt_bias, norm_g, *, t_valid):
    b, t, _ = p.shape
    chunk = min(DN_CHUNK, t)
    tb = min(256, t)
    hist = jnp.pad(conv_prev, ((0, 0), (DN_HALO - (DN_CONV - 1), 0), (0, 0)))
    lane_pad = lambda z: jnp.pad(z, (0, LANES - z.shape[0])).reshape(1, LANES)
    const = lambda shape: pl.BlockSpec(shape, lambda i, j: (0,) * len(shape))
    y, st = pl.pallas_call(
        functools.partial(_dn_body, chunk=chunk, t_valid=t_valid),
        grid=(b, t // tb),
        in_specs=[pl.BlockSpec((1, tb, DN_PAD), lambda i, j: (i, j, 0)),
                  pl.BlockSpec((1, DN_HALO, DN_QKV), lambda i, j: (i, 0, 0)),
                  pl.BlockSpec((1, GROUP_W, HEAD_DIM), lambda i, j: (i, 0, 0)),
                  const((DN_CONV, DN_QKV)), const((1, LANES)), const((1, LANES)), const((1, HEAD_DIM))],
        out_specs=[pl.BlockSpec((1, tb, GROUP_W), lambda i, j: (i, j, 0)),
                   pl.BlockSpec((1, GROUP_W, HEAD_DIM), lambda i, j: (i, 0, 0))],
        out_shape=[jax.ShapeDtypeStruct((b, t, GROUP_W), F32),
                   jax.ShapeDtypeStruct((b, GROUP_W, HEAD_DIM), F32)],
        scratch_shapes=[pltpu.VMEM((GROUP_W, HEAD_DIM), F32), pltpu.VMEM((DN_HALO, DN_QKV), F32)],
        compiler_params=_cparams("parallel", "arbitrary"),
        name="dn_mix",
    )(p, hist, s0.reshape(b, GROUP_W, HEAD_DIM), conv_w, lane_pad(a_log), lane_pad(dt_bias),
      norm_g.reshape(1, HEAD_DIM))
    return y, st.reshape(b, N_HEADS, HEAD_DIM, HEAD_DIM)


def _diff_lambda(lq1_ref, lk1_ref, lq2_ref, lk2_ref, lam_init):
    return (jnp.exp(jnp.sum(lq1_ref[...] * lk1_ref[...], axis=-1, keepdims=True))
            - jnp.exp(jnp.sum(lq2_ref[...] * lk2_ref[...], axis=-1, keepdims=True)) + lam_init)


def _head_rms(o, g, scale):
    gr, gcl = _iota2(GROUP_W, GROUP_W)
    head_ones = _same_block(gr, gcl, HEAD_DIM).astype(F32)
    ms = _mm_exact_lhs(o * o, head_ones) * (1.0 / HEAD_DIM)
    return o * lax.rsqrt(ms + RMS_EPS) * g * scale


def _attn_body(q_ref, k_ref, v_ref, lq1_ref, lk1_ref, lq2_ref, lk2_ref, ng_ref, o_ref,
               kb_ref, vt_ref, bias_ref, biasd_ref, qs_ref, p_ref, m_ref, l_ref, acc_ref, *, lam_init):
    b = pl.program_id(0)
    i = pl.program_id(1)
    tq, tk = AT_TQ, AT_TK
    t = k_ref.shape[1]
    slope = [LOG2E * 2.0 ** (-2 * (h + 1)) for h in range(N_HEADS)]

    @pl.when((b == 0) & (i == 0))
    def _():
        jj, ii = _iota2(tk, tq)
        rel = (ii - jj).astype(F32)
        for h in range(N_HEADS):
            bias_ref[:, h * tq:(h + 1) * tq] = -slope[h] * rel
            biasd_ref[:, h * tq:(h + 1) * tq] = jnp.where(rel >= 0, -slope[h] * rel, NEG_INF)

    @pl.when(i == 0)
    def _():
        kb_ref[...] = k_ref[0].astype(BF16)
        for j in range(t // tk):
            vt_ref[:, j * tk:(j + 1) * tk] = v_ref[0, j * tk:(j + 1) * tk, :].T.astype(BF16)

    q = q_ref[0] * (DA_QK ** -0.5 * LOG2E)
    lane = lax.broadcasted_iota(jnp.int32, (1, GROUP_W), 1)
    for c in range(N_BR):
        own = (lane >> 5) == c
        qs_ref[c * tq:(c + 1) * tq, :] = jnp.where(own, q, 0.0).astype(BF16)
    m_ref[...] = jnp.full_like(m_ref, NEG_INF)
    l_ref[...] = jnp.zeros_like(l_ref)
    acc_ref[...] = jnp.zeros_like(acc_ref)
    row_head = lax.broadcasted_iota(jnp.int32, (GROUP_W, 1), 0) >> 6

    def tile(j, b_ref):
        start = pl.multiple_of(j * tk, tk)
        kt = kb_ref[pl.ds(start, tk), :]
        dist = (tq * (i - j)).astype(F32)
        for c in range(N_BR):
            h, br = c >> 1, c & 1
            cs = slice(c * tq, (c + 1) * tq)
            hs = slice(h * HEAD_DIM, (h + 1) * HEAD_DIM)
            s = _mm_nt(kt, qs_ref[cs, :]) + b_ref[:, h * tq:(h + 1) * tq]
            ct = -slope[h] * dist
            m_old = m_ref[:, cs]
            m_new = jnp.maximum(m_old, jnp.max(s, axis=0, keepdims=True) + ct)
            alpha = jnp.exp2(m_old - m_new)
            p = jnp.exp2(s - (m_new - ct))
            l_ref[:, cs] = alpha * l_ref[:, cs] + jnp.sum(p, axis=0, keepdims=True)
            m_ref[:, cs] = m_new
            p_ref[br, h * tk:(h + 1) * tk, :] = p.astype(BF16)
            acc_ref[br, hs, :] = acc_ref[br, hs, :] * alpha
        vtt = vt_ref[:, pl.ds(start, tk)]
        v_lhs = jnp.concatenate([jnp.where(row_head == h, vtt, jnp.zeros_like(vtt)) for h in range(N_HEADS)],
                                axis=1)
        for br in range(2):
            acc_ref[br] += jnp.dot(v_lhs, p_ref[br], preferred_element_type=F32)

    def body(j, carry):
        tile(j, bias_ref)
        return carry

    lax.fori_loop(0, i, body, 0)
    tile(i, biasd_ref)

    def rows_of(stat, br):
        return jnp.concatenate(
            [jnp.broadcast_to(stat[:, (2 * h + br) * tq:(2 * h + br + 1) * tq], (HEAD_DIM, tq))
             for h in range(N_HEADS)], axis=0)

    lam = _diff_lambda(lq1_ref, lk1_ref, lq2_ref, lk2_ref, lam_init)
    linv = 1.0 / l_ref[...]
    o = (acc_ref[0] * rows_of(linv, 0) - lam * (acc_ref[1] * rows_of(linv, 1))).T
    o_ref[0] = _head_rms(o, ng_ref[...], 1.0 - lam_init)


def attn_prompt(q, k, v, lq1, lk1, lq2, lk2, norm_g, lam_init):
    b, t, _ = q.shape
    tq, tk = AT_TQ, AT_TK
    row = lambda z: z.reshape(1, -1)
    const = lambda shape: pl.BlockSpec(shape, lambda i, j: (0,) * len(shape))
    return pl.pallas_call(
        functools.partial(_attn_body, lam_init=lam_init),
        grid=(b, t // tq),
        in_specs=[pl.BlockSpec((1, tq, GROUP_W), lambda i, j: (i, j, 0)),
                  pl.BlockSpec((1, t, GROUP_W), lambda i, j: (i, 0, 0)),
                  pl.BlockSpec((1, t, GROUP_W), lambda i, j: (i, 0, 0)),
                  const((1, DA_QK)), const((1, DA_QK)), const((1, DA_QK)), const((1, DA_QK)),
                  const((1, GROUP_W))],
        out_specs=pl.BlockSpec((1, tq, GROUP_W), lambda i, j: (i, j, 0)),
        out_shape=jax.ShapeDtypeStruct((b, t, GROUP_W), F32),
        scratch_shapes=[pltpu.VMEM((t, GROUP_W), BF16), pltpu.VMEM((GROUP_W, t), BF16),
                        pltpu.VMEM((tk, N_HEADS * tq), F32), pltpu.VMEM((tk, N_HEADS * tq), F32),
                        pltpu.VMEM((N_BR * tq, GROUP_W), BF16), pltpu.VMEM((2, N_HEADS * tk, tq), BF16),
                        pltpu.VMEM((1, N_BR * tq), F32), pltpu.VMEM((1, N_BR * tq), F32),
                        pltpu.VMEM((2, GROUP_W, tq), F32)],
        compiler_params=_cparams("arbitrary", "arbitrary"),
        name="attn_prompt",
    )(q, k, v, row(lq1), row(lk1), row(lq2), row(lk2), row(jnp.tile(norm_g, N_HEADS)))


def _paged_body(pt_ref, q_ref, kn_ref, vn_ref, lq1_ref, lk1_ref, lq2_ref, lk2_ref, ng_ref, *rest,
                lam_init, page, past):
    k_refs = rest[:PG_PAGES]
    v_refs = rest[PG_PAGES:2 * PG_PAGES]
    o_ref, qs_ref, base_ref, m_ref, l_ref, acc_ref = rest[2 * PG_PAGES:]
    s_id = pl.program_id(1)
    span = PG_PAGES * page
    rowc = lax.broadcasted_iota(jnp.int32, (N_BR, 1), 0)
    slope = jnp.exp2(-2.0 * ((rowc >> 1) + 1).astype(F32))

    @pl.when(s_id == 0)
    def _():
        lane = lax.broadcasted_iota(jnp.int32, (N_BR, GROUP_W), 1)
        own = (lane >> 5) == lax.broadcasted_iota(jnp.int32, (N_BR, GROUP_W), 0)
        qs_ref[...] = jnp.where(own, jnp.broadcast_to(q_ref[0] * (DA_QK ** -0.5), (N_BR, GROUP_W)), 0.0)
        col = lax.broadcasted_iota(jnp.int32, (N_BR, span), 1).astype(F32)
        base_ref[...] = slope * col
        m_ref[...] = jnp.full_like(m_ref, NEG_INF)
        l_ref[...] = jnp.zeros_like(l_ref)
        acc_ref[...] = jnp.zeros_like(acc_ref)

    qs = qs_ref[...]
    qb = qs.astype(BF16)
    s = jnp.concatenate(
        [jnp.dot(qb, k_refs[r][0, 0].astype(BF16), preferred_element_type=F32) for r in range(PG_PAGES)],
        axis=1)
    off = -slope * (past - s_id * span).astype(F32)
    s = s + base_ref[...]
    m_old = m_ref[...]
    m_new = jnp.maximum(m_old, jnp.max(s, axis=1, keepdims=True) + off)
    alpha = jnp.exp(m_old - m_new)
    p = jnp.exp(s - (m_new - off))
    l_new = alpha * l_ref[...] + jnp.sum(p, axis=1, keepdims=True)
    pb = p.astype(BF16)
    pv = None
    for r in range(PG_PAGES):
        d = _mm_nt(pb[:, r * page:(r + 1) * page], v_refs[r][0, 0])
        pv = d if pv is None else pv + d
    acc = acc_ref[...] * alpha + pv
    m_ref[...] = m_new
    l_ref[...] = l_new
    acc_ref[...] = acc

    @pl.when(s_id == pl.num_programs(1) - 1)
    def _():
        s_self = jnp.sum(qs * kn_ref[0], axis=1, keepdims=True)
        m_fin = jnp.maximum(m_new, s_self)
        a_fin = jnp.exp(m_new - m_fin)
        p_self = jnp.exp(s_self - m_fin)
        l_fin = a_fin * l_new + p_self
        out = (acc * a_fin + p_self * vn_ref[0]) / l_fin
        lam = _diff_lambda(lq1_ref, lk1_ref, lq2_ref, lk2_ref, lam_init)
        lane = lax.broadcasted_iota(jnp.int32, (N_BR, GROUP_W), 1)
        rown = lax.broadcasted_iota(jnp.int32, (N_BR, GROUP_W), 0)
        coef = jnp.where((rown & 1) == 0, 1.0, -lam)
        o = jnp.sum(jnp.where((lane >> 6) == (rown >> 1), out * coef, 0.0), axis=0, keepdims=True)
        o8 = jnp.broadcast_to(o, (SUBLANES, GROUP_W))
        o_ref[0] = _head_rms(o8, ng_ref[...], 1.0 - lam_init)[0:1]


def attn_sample(q, k_new, v_new, cache_k, cache_v, page_table, layer, lq1, lk1, lq2, lk2, norm_g, lam_init):
    b = q.shape[0]
    depth, n_pool, page = cache_k.shape[:3]
    n_pages = page_table.shape[1]
    assert n_pages % PG_PAGES == 0
    as_pages = lambda c: jnp.transpose(c, (0, 1, 3, 4, 2)).reshape(depth, n_pool, GROUP_W, page)
    ck, cv = as_pages(cache_k), as_pages(cache_v)
    row = lambda z: z.reshape(1, -1)
    const = lambda shape: pl.BlockSpec(shape, lambda i, j, pt: (0,) * len(shape))
    tok = pl.BlockSpec((1, 1, GROUP_W), lambda i, j, pt: (i, 0, 0))
    page_spec = lambda r: pl.BlockSpec((1, 1, GROUP_W, page),
                                       lambda i, j, pt, r=r: (layer, pt[i, j * PG_PAGES + r], 0, 0))
    span = PG_PAGES * page
    grid_spec = pltpu.PrefetchScalarGridSpec(
        num_scalar_prefetch=1,
        grid=(b, n_pages // PG_PAGES),
        in_specs=[tok, tok, tok, const((1, DA_QK)), const((1, DA_QK)), const((1, DA_QK)), const((1, DA_QK)),
                  const((1, GROUP_W))] + [page_spec(r) for r in range(PG_PAGES)] * 2,
        out_specs=pl.BlockSpec((1, 1, GROUP_W), lambda i, j, pt: (i, 0, 0)),
        scratch_shapes=[pltpu.VMEM((N_BR, GROUP_W), F32), pltpu.VMEM((N_BR, span), F32),
                        pltpu.VMEM((N_BR, 1), F32), pltpu.VMEM((N_BR, 1), F32), pltpu.VMEM((N_BR, GROUP_W), F32)])
    r3 = lambda z: z.reshape(b, 1, GROUP_W)
    out = pl.pallas_call(
        functools.partial(_paged_body, lam_init=lam_init, page=page, past=n_pages * page),
        grid_spec=grid_spec,
        out_shape=jax.ShapeDtypeStruct((b, 1, GROUP_W), F32),
        compiler_params=_cparams("arbitrary", "arbitrary"),
        name="attn_sample",
    )(page_table, r3(q), r3(k_new), r3(v_new), row(lq1), row(lk1), row(lq2), row(lk2),
      row(jnp.tile(norm_g, N_HEADS)), *([ck] * PG_PAGES), *([cv] * PG_PAGES))
    return out.reshape(b, GROUP_W)


def _conv_body(p_ref, buf_ref, w_ref, b_ref, lg_ref, lb_ref, y_ref, tail_ref, hist_ref, *, t_valid):
    tb = p_ref.shape[1]
    ti = pl.program_id(1)

    @pl.when(ti == 0)
    def _():
        hist_ref[...] = buf_ref[0]

    x = p_ref[0]
    glu = x[:, :GROUP_W] * _sigmoid(x[:, GROUP_W:])
    ext = jnp.concatenate([hist_ref[...], glu], axis=0)
    hist_new = ext[t_valid:t_valid + CV_HALO, :]
    hist_ref[...] = hist_new
    acc = None
    for j in range(CONV_WIDTH):
        lo = CV_HALO - (CONV_WIDTH - 1) + j
        term = ext[lo:lo + tb, :] * w_ref[j:j + 1, :]
        acc = term if acc is None else acc + term
    h = acc + b_ref[...]
    hc = h - jnp.mean(h, axis=-1, keepdims=True)
    var = jnp.mean(hc * hc, axis=-1, keepdims=True)
    hn = hc * lax.rsqrt(var + LN_EPS) * lg_ref[...] + lb_ref[...]
    y_ref[0] = hn * _sigmoid(hn)

    @pl.when(ti == pl.num_programs(1) - 1)
    def _():
        tail_ref[0] = hist_new


def conv_mix(p, buf, conv_w, conv_b, ln_g, ln_b, *, t_valid):
    b, t, _ = p.shape
    tb = min(512, t)
    assert t == tb or t_valid == t
    keep = CONV_WIDTH - 1
    hist = jnp.pad(buf, ((0, 0), (CV_HALO - keep, 0), (0, 0)))
    row = lambda z: z.reshape(1, -1)
    const = lambda shape: pl.BlockSpec(shape, lambda i, j: (0,) * len(shape))
    y, tail = pl.pallas_call(
        functools.partial(_conv_body, t_valid=min(t_valid, tb)),
        grid=(b, t // tb),
        in_specs=[pl.BlockSpec((1, tb, CV_COLS), lambda i, j: (i, j, 0)),
                  pl.BlockSpec((1, CV_HALO, GROUP_W), lambda i, j: (i, 0, 0)),
                  const((CONV_WIDTH, GROUP_W)), const((1, GROUP_W)), const((1, GROUP_W)), const((1, GROUP_W))],
        out_specs=[pl.BlockSpec((1, tb, GROUP_W), lambda i, j: (i, j, 0)),
                   pl.BlockSpec((1, CV_HALO, GROUP_W), lambda i, j: (i, 0, 0))],
        out_shape=[jax.ShapeDtypeStruct((b, t, GROUP_W), F32),
                   jax.ShapeDtypeStruct((b, CV_HALO, GROUP_W), F32)],
        scratch_shapes=[pltpu.VMEM((CV_HALO, GROUP_W), F32)],
        compiler_params=_cparams("parallel", "arbitrary"),
        name="conv_mix",
    )(p, hist, conv_w, row(conv_b), row(ln_g), row(ln_b))
    return y, tail[:, CV_HALO - keep:]


def trunk_layer(l, x, paged, cv_buf, rw_shift, rw_S, dn_buf, dn_S, P, W):
    b, t, _ = x.shape
    n = b * t
    x2 = x.reshape(n, D_MODEL)
    q, k, v, p_cv, p_rw, p_dn = in_proj(x2, P['g_pre_mix'][l], W['w_in'][l])
    lam_init = 0.8 - 0.6 * math.exp(-0.3 * l)
    lam_args = (P['da_lq1'][l], P['da_lk1'][l], P['da_lq2'][l], P['da_lk2'][l], P['da_norm_g'][l], lam_init)
    if paged is None:
        r3 = lambda z: z.reshape(b, t, GROUP_W)
        o_a = attn_prompt(r3(q), r3(k), r3(v), *lam_args).reshape(n, GROUP_W)
    else:
        assert t == 1
        o_a = attn_sample(q, k, v, *paged, l, *lam_args)
    hs = lambda z: z.reshape(b, t, N_HEADS, HEAD_DIM)
    k, v = hs(k), hs(v)
    t_pad = -(-t // SUBLANES) * SUBLANES
    pad_t = lambda z: z if t_pad == t else jnp.pad(z, ((0, 0), (0, t_pad - t), (0, 0)))
    o_b, cv_new = conv_mix(pad_t(p_cv.reshape(b, t, CV_COLS)), cv_buf, P['cv_w'][l], P['cv_b'][l],
                           P['cv_ln_g'][l], P['cv_ln_b'][l], t_valid=t)
    o_b = o_b[:, :t]
    p_rw = p_rw.reshape(b, t, RW_COLS)
    o_c, rw_S_new = rwkv_mix(
        pad_t(p_rw), rw_shift, rw_S, P['rw_mu'][l], P['rw_w0'][l], P['rw_w_up'][l],
        P['rw_a0'][l], P['rw_a_up'][l], P['rw_g_up'][l], P['rw_k_k'][l], P['rw_k_a'][l],
        P['rw_r_k'][l].reshape(-1), P['rw_gn_g'][l], P['rw_gn_b'][l], t_valid=t)
    o_c = o_c[:, :t]
    rw_shift_new = p_rw[:, -1]
    p_dn = p_dn.reshape(b, t, DN_PAD)
    o_d, dn_S_new = dn_mix(pad_t(p_dn), dn_buf, dn_S, P['dn_conv_w'][l], P['dn_A_log'][l],
                           P['dn_dt_bias'][l], P['dn_norm_g'][l], t_valid=t)
    o_d = o_d[:, :t]
    keep = DN_CONV - 1
    dn_buf_new = (p_dn[:, t - keep:, :DN_QKV] if t >= keep else
                  jnp.concatenate([dn_buf, p_dn[..., :DN_QKV]], axis=1)[:, -keep:])
    x2 = out_proj(x2, o_a, o_b.reshape(n, GROUP_W), o_c.reshape(n, GROUP_W), o_d.reshape(n, GROUP_W),
                  W['w_out'][l], P['g_post_mix'][l])
    x2 = ffn(x2, P['g_pre_ffn'][l], W['ffn_w_gate'][l], W['ffn_w_up'][l], W['ffn_w_down'][l],
             P['g_post_ffn'][l])
    return x2.reshape(b, t, D_MODEL), (k, v, cv_new, rw_shift_new, rw_S_new, dn_buf_new, dn_S_new)


def kernel(x_prompt, x_sample, cache_k, cache_v, page_table, state_conv, state_rw_shift,
           state_rw_wkv, state_dn_conv, state_dn_ssm, g_pre_mix, g_post_mix, g_pre_ffn,
           g_post_ffn, w_in, w_out, da_lq1, da_lk1, da_lq2, da_lk2, da_norm_g, cv_w, cv_b,
           cv_ln_g, cv_ln_b, rw_mu, rw_w0, rw_w_up, rw_a0, rw_a_up, rw_g_up, rw_k_k, rw_k_a,
           rw_r_k, rw_gn_g, rw_gn_b, dn_conv_w, dn_A_log, dn_dt_bias, dn_norm_g,
           ffn_w_gate, ffn_w_up, ffn_w_down):
    P = dict(g_pre_mix=g_pre_mix, g_post_mix=g_post_mix, g_pre_ffn=g_pre_ffn, g_post_ffn=g_post_ffn,
             da_lq1=da_lq1, da_lk1=da_lk1, da_lq2=da_lq2, da_lk2=da_lk2,
             da_norm_g=da_norm_g, cv_w=cv_w, cv_b=cv_b, cv_ln_g=cv_ln_g, cv_ln_b=cv_ln_b,
             rw_mu=rw_mu, rw_w0=rw_w0, rw_w_up=rw_w_up, rw_a0=rw_a0, rw_a_up=rw_a_up,
             rw_g_up=rw_g_up, rw_k_k=rw_k_k, rw_k_a=rw_k_a, rw_r_k=rw_r_k, rw_gn_g=rw_gn_g,
             rw_gn_b=rw_gn_b, dn_conv_w=dn_conv_w, dn_A_log=dn_A_log, dn_dt_bias=dn_dt_bias,
             dn_norm_g=dn_norm_g)
    depth = w_in.shape[0]
    W = dict(w_in=[_prep_w_in(w_in[l]) for l in range(depth)],
             w_out=[w_out[l].astype(BF16) for l in range(depth)],
             ffn_w_gate=[ffn_w_gate[l].astype(BF16) for l in range(depth)],
             ffn_w_up=[ffn_w_up[l].astype(BF16) for l in range(depth)],
             ffn_w_down=[ffn_w_down[l].astype(BF16) for l in range(depth)])
    bp = x_prompt.shape[0]
    dtp = x_prompt.dtype
    yp = x_prompt
    ys = x_sample
    outs_p = []
    outs_s = []
    for l in range(depth):
        yp, st_p = trunk_layer(
            l, yp, None,
            jnp.zeros((bp, CONV_WIDTH - 1, GROUP_W), dtp), jnp.zeros((bp, RW_COLS), dtp),
            jnp.zeros((bp, N_HEADS, HEAD_DIM, HEAD_DIM), jnp.float32),
            jnp.zeros((bp, DN_CONV - 1, DN_QKV), dtp),
            jnp.zeros((bp, N_HEADS, HEAD_DIM, HEAD_DIM), jnp.float32), P, W)
        ys, st_s = trunk_layer(l, ys, (cache_k, cache_v, page_table), state_conv[l], state_rw_shift[l],
                               state_rw_wkv[l], state_dn_conv[l], state_dn_ssm[l], P, W)
        outs_p.append(st_p)
        outs_s.append(st_s)
    stk = lambda outs, i: jnp.stack([o[i] for o in outs])
    return (yp, ys, stk(outs_p, 0), stk(outs_p, 1), stk(outs_s, 0), stk(outs_s, 1),
            stk(outs_p, 2), stk(outs_s, 2), stk(outs_p, 3), stk(outs_s, 3),
            stk(outs_p, 4), stk(outs_s, 4), stk(outs_p, 5), stk(outs_s, 5),
            stk(outs_p, 6), stk(outs_s, 6))
```

```python
import functools
import math

import jax
import jax.numpy as jnp
from jax import lax
from jax.experimental import pallas as pl
from jax.experimental.pallas import tpu as pltpu

D_MODEL = 1024
HEAD_DIM = 64
N_HEADS = 4
GROUP_W = 256
DA_QK = HEAD_DIM // 2
CONV_WIDTH = 31
RW_W_RANK = 64
RW_A_RANK = 64
RW_G_RANK = 128
RW_OFF_W = 3 * GROUP_W
RW_OFF_A = RW_OFF_W + RW_W_RANK
RW_OFF_G = RW_OFF_A + RW_A_RANK
RW_COLS = RW_OFF_G + RW_G_RANK
DN_CONV = 4
DN_CHUNK = 64
DN_QKV = 3 * GROUP_W
DN_COLS = DN_QKV + 2 * N_HEADS + GROUP_W
DA_COLS = 3 * GROUP_W
CV_COLS = 2 * GROUP_W
OFF_CV = DA_COLS
OFF_RW = OFF_CV + CV_COLS
OFF_DN = OFF_RW + RW_COLS
N_IN = OFF_DN + DN_COLS
FFN_HIDDEN = 2816
RMS_EPS = 1e-6
LN_EPS = 1e-5
RW_GN_EPS = 64e-5
NEG_INF = -1e30
LOG2E = math.log2(math.e)

LANES = 128
SUBLANES = 8
DN_PAD = DN_QKV + GROUP_W + LANES
DN_HALO = 8
CV_HALO = 32
N_BR = 2 * N_HEADS
AT_TQ = 512
AT_TK = 256
PG_PAGES = 16
VMEM_LIMIT = 56 * 1024 * 1024

F32 = jnp.float32
BF16 = jnp.bfloat16


def _cparams(*sem):
    return pltpu.CompilerParams(dimension_semantics=sem, vmem_limit_bytes=VMEM_LIMIT)


def _rms(x, g):
    return x * lax.rsqrt(jnp.mean(x * x, axis=-1, keepdims=True) + RMS_EPS) * g


def _row_tile(n, target):
    return target if n % target == 0 else n


def _in_proj_body(x_ref, g_ref, w_ref, q_ref, k_ref, v_ref, cv_ref, rw_ref, dn_ref):
    hn = _rms(x_ref[...], g_ref[...]).astype(BF16)
    off = 0
    for ref in (q_ref, k_ref, v_ref, cv_ref, rw_ref, dn_ref):
        w = ref.shape[-1]
        ref[...] = jnp.dot(hn, w_ref[:, off:off + w], preferred_element_type=F32)
        off += w


def _prep_w_in(w):
    dn = w[:, OFF_DN:]
    dn = jnp.concatenate([dn[:, :DN_QKV], dn[:, DN_QKV + 2 * N_HEADS:], dn[:, DN_QKV:DN_QKV + 2 * N_HEADS],
                          jnp.zeros((w.shape[0], LANES - 2 * N_HEADS), w.dtype)], axis=1)
    return jnp.concatenate([w[:, :OFF_DN], dn], axis=1).astype(BF16)


def in_proj(x, g, w):
    n = x.shape[0]
    tm = _row_tile(n, 512)
    widths = (GROUP_W, GROUP_W, GROUP_W, CV_COLS, RW_COLS, DN_PAD)
    row = lambda wd: pl.BlockSpec((tm, wd), lambda i: (i, 0))
    return pl.pallas_call(
        _in_proj_body,
        grid=(n // tm,),
        in_specs=[row(D_MODEL), pl.BlockSpec((1, D_MODEL), lambda i: (0, 0)),
                  pl.BlockSpec(w.shape, lambda i: (0, 0))],
        out_specs=[row(wd) for wd in widths],
        out_shape=[jax.ShapeDtypeStruct((n, wd), F32) for wd in widths],
        compiler_params=_cparams("parallel"),
        name="in_proj",
    )(x, g.reshape(1, -1), w)


def _out_proj_body(x_ref, a_ref, b_ref, c_ref, d_ref, w_ref, g_ref, o_ref):
    acc = None
    for i, ref in enumerate((a_ref, b_ref, c_ref, d_ref)):
        part = jnp.dot(ref[...].astype(BF16), w_ref[i * GROUP_W:(i + 1) * GROUP_W, :],
                       preferred_element_type=F32)
        acc = part if acc is None else acc + part
    o_ref[...] = x_ref[...] + _rms(acc, g_ref[...])


def out_proj(x, oa, ob, oc, od, w, g):
    n = x.shape[0]
    tm = _row_tile(n, 512)
    row = lambda wd: pl.BlockSpec((tm, wd), lambda i: (i, 0))
    return pl.pallas_call(
        _out_proj_body,
        grid=(n // tm,),
        in_specs=[row(D_MODEL), row(GROUP_W), row(GROUP_W), row(GROUP_W), row(GROUP_W),
                  pl.BlockSpec(w.shape, lambda i: (0, 0)), pl.BlockSpec((1, D_MODEL), lambda i: (0, 0))],
        out_specs=row(D_MODEL),
        out_shape=jax.ShapeDtypeStruct((n, D_MODEL), F32),
        compiler_params=_cparams("parallel"),
        name="out_proj",
    )(x, oa, ob, oc, od, w, g.reshape(1, -1))


def _ffn_body(x_ref, g1_ref, wg_ref, wu_ref, wd_ref, g2_ref, o_ref, hn_ref, acc_ref):
    j = pl.program_id(1)

    @pl.when(j == 0)
    def _():
        hn_ref[...] = _rms(x_ref[...], g1_ref[...]).astype(BF16)
        acc_ref[...] = jnp.zeros_like(acc_ref)

    hn = hn_ref[...]
    gate = jnp.dot(hn, wg_ref[...], preferred_element_type=F32)
    up = jnp.dot(hn, wu_ref[...], preferred_element_type=F32)
    f = (gate * jax.nn.sigmoid(gate) * up).astype(BF16)
    acc_ref[...] += jnp.dot(f, wd_ref[...], preferred_element_type=F32)

    @pl.when(j == pl.num_programs(1) - 1)
    def _():
        o_ref[...] = x_ref[...] + _rms(acc_ref[...], g2_ref[...])


def ffn(x, g1, wg, wu, wd, g2):
    n = x.shape[0]
    tm = _row_tile(n, 512)
    th = FFN_HIDDEN // 2
    return pl.pallas_call(
        _ffn_body,
        grid=(n // tm, FFN_HIDDEN // th),
        in_specs=[pl.BlockSpec((tm, D_MODEL), lambda i, j: (i, 0)),
                  pl.BlockSpec((1, D_MODEL), lambda i, j: (0, 0)),
                  pl.BlockSpec((D_MODEL, th), lambda i, j: (0, j)),
                  pl.BlockSpec((D_MODEL, th), lambda i, j: (0, j)),
                  pl.BlockSpec((th, D_MODEL), lambda i, j: (j, 0)),
                  pl.BlockSpec((1, D_MODEL), lambda i, j: (0, 0))],
        out_specs=pl.BlockSpec((tm, D_MODEL), lambda i, j: (i, 0)),
        out_shape=jax.ShapeDtypeStruct((n, D_MODEL), F32),
        scratch_shapes=[pltpu.VMEM((tm, D_MODEL), BF16), pltpu.VMEM((tm, D_MODEL), F32)],
        compiler_params=_cparams("parallel", "arbitrary"),
        name="ffn",
    )(x, g1.reshape(1, -1), wg, wu, wd, g2.reshape(1, -1))


def _mm(a, b):
    return jnp.dot(a.astype(BF16), b.astype(BF16), preferred_element_type=F32)


def _mm_nt(a, b):
    return lax.dot_general(a.astype(BF16), b.astype(BF16), (((1,), (1,)), ((), ())),
                           preferred_element_type=F32)


def _mm_tn(a, b):
    return lax.dot_general(a.astype(BF16), b.astype(BF16), (((0,), (0,)), ((), ())),
                           preferred_element_type=F32)


def _split3(x):
    hi = x.astype(BF16)
    r1 = x - hi.astype(F32)
    mid = r1.astype(BF16)
    lo = (r1 - mid.astype(F32)).astype(BF16)
    return hi, mid, lo


def _mm_exact_rhs(a01, x):
    a = a01.astype(BF16)
    hi, mid, lo = _split3(x)
    d = lambda y: jnp.dot(a, y, preferred_element_type=F32)
    return d(hi) + (d(mid) + d(lo))


def _mm_exact_lhs(x, a01):
    a = a01.astype(BF16)
    hi, mid, lo = _split3(x)
    d = lambda y: jnp.dot(y, a, preferred_element_type=F32)
    return d(hi) + (d(mid) + d(lo))


def _iota2(n, m):
    return lax.broadcasted_iota(jnp.int32, (n, m), 0), lax.broadcasted_iota(jnp.int32, (n, m), 1)


def _same_block(r, c, blk):
    s = int(math.log2(blk))
    return (r >> s) == (c >> s)


def _unit_lower_inverse(mats, r, c, chunk):
    eye = (r == c).astype(F32)
    base = min(8, chunk)
    blk = _same_block(r, c, base)
    pw = [jnp.where(blk, a, 0.0) for a in mats]
    d = [eye + p for p in pw]
    for _ in range(max(int(math.log2(base)) - 1, 0)):
        pw = [_mm(p, p) for p in pw]
        d = [x + _mm(x, p) for x, p in zip(d, pw)]
    size = base * 2
    while size <= chunk:
        ring = _same_block(r, c, size) & ~_same_block(r, c, size // 2)
        t = [_mm(jnp.where(ring, a, 0.0), x) for a, x in zip(mats, d)]
        d = [x + _mm(x, y) for x, y in zip(d, t)]
        size *= 2
    return d


def _softplus(z):
    return jnp.maximum(z, 0.0) + jnp.log(1.0 + jnp.exp(-jnp.abs(z)))


def _sigmoid(z):
    return 1.0 / (1.0 + jnp.exp(-z))


def _head_stackers(chunk):
    hw = N_HEADS * chunk
    lr, lc = _iota2(hw, GROUP_W)
    own = (lr >> int(math.log2(chunk))) == (lc >> int(math.log2(HEAD_DIM)))

    def stack_bd(z):
        return jnp.where(own, jnp.concatenate([z] * N_HEADS, axis=0), 0.0)

    def stack_heads(z):
        return jnp.concatenate([z[:, h * HEAD_DIM:(h + 1) * HEAD_DIM] for h in range(N_HEADS)], axis=0)

    return stack_bd, stack_heads


def _row_to_col(row_vals, r, c):
    n = r.shape[0]
    return jnp.sum(jnp.where(r == c, jnp.broadcast_to(row_vals, (n, n)), 0.0), axis=1, keepdims=True)


def _seqs_per_step(b, t):
    want = 2 if t >= HEAD_DIM else 8
    return want if b % want == 0 else 1


def _rwkv_body(p_ref, shift_ref, s0_ref, mu_ref, w0_ref, wup_ref, a0_ref, aup_ref, gup_ref, kk_ref, ka_ref,
               rk_ref, gng_ref, gnb_ref, y_ref, s_out_ref, st_ref, prev_ref, *, chunk, t_valid):
    nb, tb = p_ref.shape[0], p_ref.shape[1]
    ti = pl.program_id(1)
    hw = N_HEADS * chunk

    @pl.when(ti == 0)
    def _():
        st_ref[...] = s0_ref[...]
        prev_ref[...] = shift_ref[...]

    row = lax.broadcasted_iota(jnp.int32, (tb, 1), 0)
    gr, gcl = _iota2(GROUP_W, GROUP_W)
    head_ones = _same_block(gr, gcl, HEAD_DIM).astype(F32)
    tr, tc = _iota2(tb, tb)
    cum = (_same_block(tr, tc, chunk) & (tr >= tc)).astype(F32)
    sr, sc = _iota2(hw, hw)
    head_blk = _same_block(sr, sc, chunk)
    low_incl = head_blk & (sr >= sc)
    low_strict = head_blk & (sr > sc)
    stack_bd, stack_heads = _head_stackers(chunk)
    slices = [slice(c0, c0 + chunk) for c0 in range(0, tb, chunk)]

    def prepare(s):
        x = p_ref[s]
        prev = jnp.where(row == 0, prev_ref[s], pltpu.roll(x, 1, axis=0))
        prev_ref[s] = x[tb - 1:tb, :]
        xm = x + (prev - x) * mu_ref[...]
        r = xm[:, :GROUP_W]
        k = xm[:, GROUP_W:2 * GROUP_W]
        v = xm[:, 2 * GROUP_W:RW_OFF_W]
        wd = xm[:, RW_OFF_W:RW_OFF_A]
        ad = xm[:, RW_OFF_A:RW_OFF_G]
        gd = xm[:, RW_OFF_G:RW_COLS]
        w_log = -_softplus(-(w0_ref[...] + _mm(jnp.tanh(wd), wup_ref[...]))) - 0.5
        logw = -jnp.exp(w_log)
        a = _sigmoid(a0_ref[...] + _mm(ad, aup_ref[...]))
        g = _mm(_sigmoid(gd), gup_ref[...])
        kk = k * kk_ref[...]
        kk = kk * lax.rsqrt(_mm_exact_lhs(kk * kk, head_ones) + 1e-6)
        k2 = k * (1.0 + (a - 1.0) * ka_ref[...])
        bonus = _mm_exact_lhs(r * k2 * rk_ref[...], head_ones) * v
        if t_valid < tb:
            ok = row < t_valid
            logw = jnp.where(ok, logw, 0.0)
            kk = jnp.where(ok, kk, 0.0)
            k2 = jnp.where(ok, k2, 0.0)
            v = jnp.where(ok, v, 0.0)
        alpha = -kk
        beta = kk * a
        gcum = _mm_exact_rhs(cum, logw)
        pre = []
        for sl in slices:
            gc = gcum[sl]
            g_last = gc[chunk - 1:chunk, :]
            e_neg = jnp.exp(-gc)
            e_rem = jnp.exp(g_last - gc)
            r_t = stack_bd(r[sl] * jnp.exp(gc))
            a_t = stack_bd(alpha[sl] * jnp.exp(gc - logw[sl]))
            k_t = stack_bd(k2[sl] * e_neg)
            b_t = stack_bd(beta[sl] * e_neg)
            vs = stack_heads(v[sl])
            pre.append(dict(
                r_t=r_t, a_t=a_t, vs=vs,
                a_ab=jnp.where(low_strict, _mm_nt(a_t, b_t), 0.0),
                a_ak_v=_mm(jnp.where(low_strict, _mm_nt(a_t, k_t), 0.0), vs),
                a_rb=jnp.where(low_incl, _mm_nt(r_t, b_t), 0.0),
                a_rk_v=_mm(jnp.where(low_incl, _mm_nt(r_t, k_t), 0.0), vs),
                kv=_mm_tn(stack_bd(k2[sl] * e_rem), vs), b_hat=stack_bd(beta[sl] * e_rem),
                p_col=_row_to_col(jnp.exp(g_last), gr, gcl)))
        return pre, bonus, g

    seqs = [prepare(s) for s in range(nb)]
    flat = [c for pre, _, _ in seqs for c in pre]
    tinvs = _unit_lower_inverse([c["a_ab"] for c in flat], sr, sc, chunk)
    for c, tinv in zip(flat, tinvs):
        c["w_bd"] = _mm(tinv, c["a_t"])
        c["u_loc"] = _mm(tinv, c["a_ak_v"])
    sts = [st_ref[s] for s in range(nb)]
    for ci, sl in enumerate(slices):
        for s in range(nb):
            c, st = seqs[s][0][ci], sts[s]
            u = c["u_loc"] + _mm(c["w_bd"], st)
            y = _mm(c["r_t"], st) + c["a_rk_v"] + _mm(c["a_rb"], u)
            sts[s] = st * c["p_col"] + c["kv"] + _mm_tn(c["b_hat"], u)
            yc = y - jnp.mean(y, axis=-1, keepdims=True)
            yn = yc * lax.rsqrt(jnp.mean(yc * yc, axis=-1, keepdims=True) + RW_GN_EPS)
            for h in range(N_HEADS):
                y_ref[s, sl, h * HEAD_DIM:(h + 1) * HEAD_DIM] = yn[h * chunk:(h + 1) * chunk, :]
    for s in range(nb):
        st_ref[s] = sts[s]
        y_ref[s] = (y_ref[s] * gng_ref[...] + gnb_ref[...] + seqs[s][1]) * seqs[s][2]

    @pl.when(ti == pl.num_programs(1) - 1)
    def _():
        s_out_ref[...] = st_ref[...]


def rwkv_mix(p, shift_prev, s0, mu, w0, w_up, a0, a_up, g_up, k_k, k_a, r_k, gn_g, gn_b, *, t_valid):
    b, t, _ = p.shape
    chunk = min(HEAD_DIM, t)
    tb = min(256, t)
    nb = _seqs_per_step(b, t)
    row = lambda z: z.reshape(1, -1)
    st0 = jnp.swapaxes(s0, 2, 3).reshape(b, GROUP_W, HEAD_DIM)
    const = lambda shape: pl.BlockSpec(shape, lambda i, j: (0,) * len(shape))
    y, st = pl.pallas_call(
        functools.partial(_rwkv_body, chunk=chunk, t_valid=t_valid),
        grid=(b // nb, t // tb),
        in_specs=[pl.BlockSpec((nb, tb, RW_COLS), lambda i, j: (i, j, 0)),
                  pl.BlockSpec((nb, 1, RW_COLS), lambda i, j: (i, 0, 0)),
                  pl.BlockSpec((nb, GROUP_W, HEAD_DIM), lambda i, j: (i, 0, 0)),
                  const((1, RW_COLS)), const((1, GROUP_W)), const((RW_W_RANK, GROUP_W)),
                  const((1, GROUP_W)), const((RW_A_RANK, GROUP_W)), const((RW_G_RANK, GROUP_W)),
                  const((1, GROUP_W)), const((1, GROUP_W)), const((1, GROUP_W)),
                  const((1, GROUP_W)), const((1, GROUP_W))],
        out_specs=[pl.BlockSpec((nb, tb, GROUP_W), lambda i, j: (i, j, 0)),
                   pl.BlockSpec((nb, GROUP_W, HEAD_DIM), lambda i, j: (i, 0, 0))],
        out_shape=[jax.ShapeDtypeStruct((b, t, GROUP_W), F32),
                   jax.ShapeDtypeStruct((b, GROUP_W, HEAD_DIM), F32)],
        scratch_shapes=[pltpu.VMEM((nb, GROUP_W, HEAD_DIM), F32), pltpu.VMEM((nb, 1, RW_COLS), F32)],
        compiler_params=_cparams("parallel", "arbitrary"),
        name="rwkv_mix",
    )(p, shift_prev.reshape(b, 1, RW_COLS), st0, row(mu), row(w0), w_up, row(a0), a_up, g_up,
      row(k_k), row(k_a), row(r_k), row(gn_g), row(gn_b))
    return y, jnp.swapaxes(st.reshape(b, N_HEADS, HEAD_DIM, HEAD_DIM), 2, 3)


def _dn_body(p_ref, cprev_ref, s0_ref, cw_ref, alog_ref, dtb_ref, ng_ref, y_ref, s_out_ref, st_ref, hist_ref,
             *, chunk, t_valid):
    nb, tb = p_ref.shape[0], p_ref.shape[1]
    ti = pl.program_id(1)
    hw = N_HEADS * chunk

    @pl.when(ti == 0)
    def _():
        st_ref[...] = s0_ref[...]
        hist_ref[...] = cprev_ref[...]

    gr, gcl = _iota2(GROUP_W, GROUP_W)
    head_ones = _same_block(gr, gcl, HEAD_DIM).astype(F32)
    er, ec = _iota2(LANES, GROUP_W)
    lanes_a = (er == (ec >> 6)).astype(F32)
    lanes_b = (er == (ec >> 6) + N_HEADS).astype(F32)
    tr, tc = _iota2(tb, tb)
    cum = (_same_block(tr, tc, chunk) & (tr >= tc)).astype(F32)
    sr, sc = _iota2(hw, hw)
    head_blk = _same_block(sr, sc, chunk)
    low_incl = head_blk & (sr >= sc)
    low_strict = head_blk & (sr > sc)
    eye_hw = sr == sc
    stack_bd, stack_heads = _head_stackers(chunk)
    slices = [slice(c0, c0 + chunk) for c0 in range(0, tb, chunk)]

    def prepare(s):
        x = p_ref[s]
        ext = jnp.concatenate([hist_ref[s], x[:, :DN_QKV]], axis=0)
        hist_ref[s] = ext[tb:tb + DN_HALO, :]
        conv = None
        for j in range(DN_CONV):
            lo = DN_HALO - (DN_CONV - 1) + j
            term = ext[lo:lo + tb, :] * cw_ref[j:j + 1, :]
            conv = term if conv is None else conv + term
        qkv = conv * _sigmoid(conv)
        q = qkv[:, :GROUP_W]
        k = qkv[:, GROUP_W:2 * GROUP_W]
        v = qkv[:, 2 * GROUP_W:]
        q = q * lax.rsqrt(_mm_exact_lhs(q * q, head_ones) + 1e-6) * (HEAD_DIM ** -0.5)
        k = k * lax.rsqrt(_mm_exact_lhs(k * k, head_ones) + 1e-6)
        z = x[:, DN_QKV:DN_QKV + GROUP_W]
        ab = x[:, DN_QKV + GROUP_W:]
        g_w = _mm_exact_lhs(-jnp.exp(alog_ref[...]) * _softplus(ab + dtb_ref[...]), lanes_a)
        beta = _mm_exact_lhs(_sigmoid(ab), lanes_b)
        if t_valid < tb:
            ok = lax.broadcasted_iota(jnp.int32, (tb, 1), 0) < t_valid
            g_w = jnp.where(ok, g_w, 0.0)
            beta = jnp.where(ok, beta, 0.0)
            k = jnp.where(ok, k, 0.0)
            v = jnp.where(ok, v, 0.0)
        gcum = _mm_exact_rhs(cum, g_w)
        kb = k * beta
        vb = v * beta
        pre = []
        for sl in slices:
            gc = gcum[sl]
            g_last = gc[chunk - 1:chunk, :]
            gi = jnp.broadcast_to(stack_heads(gc)[:, :1], (hw, hw))
            gj = jnp.sum(jnp.where(eye_hw, gi, 0.0), axis=0, keepdims=True)
            decay = jnp.exp(jnp.where(low_incl, gi - gj, NEG_INF))
            k_bd = stack_bd(k[sl])
            pre.append(dict(
                m=jnp.where(low_strict, _mm_nt(stack_bd(kb[sl]), k_bd) * decay, 0.0),
                qk=jnp.where(low_incl, _mm_nt(stack_bd(q[sl]), k_bd) * decay, 0.0),
                kb_g=stack_bd(kb[sl] * jnp.exp(gc)), q_g=stack_bd(q[sl] * jnp.exp(gc)), vb=stack_heads(vb[sl]),
                k_rem=stack_bd(k[sl] * jnp.exp(g_last - gc)), p_col=_row_to_col(jnp.exp(g_last), gr, gcl)))
        return pre, z

    seqs = [prepare(s) for s in range(nb)]
    flat = [c for pre, _ in seqs for c in pre]
    tinvs = _unit_lower_inverse([-c["m"] for c in flat], sr, sc, chunk)
    for c, tinv in zip(flat, tinvs):
        c["w_bd"] = _mm(tinv, c["kb_g"])
        c["u_loc"] = _mm(tinv, c["vb"])
    sts = [st_ref[s] for s in range(nb)]
    for ci, sl in enumerate(slices):
        for s in range(nb):
            c, st = seqs[s][0][ci], sts[s]
            v_new = c["u_loc"] - _mm(c["w_bd"], st)
            o = _mm(c["q_g"], st) + _mm(c["qk"], v_new)
            sts[s] = st * c["p_col"] + _mm_tn(c["k_rem"], v_new)
            on = o * lax.rsqrt(jnp.mean(o * o, axis=-1, keepdims=True) + RMS_EPS) * ng_ref[...]
            for h in range(N_HEADS):
                y_ref[s, sl, h * HEAD_DIM:(h + 1) * HEAD_DIM] = on[h * chunk:(h + 1) * chunk, :]
    for s in range(nb):
        st_ref[s] = sts[s]
        z = seqs[s][1]
        y_ref[s] = y_ref[s] * (z * _sigmoid(z))

    @pl.when(ti == pl.num_programs(1) - 1)
    def _():
        s_out_ref[...] = st_ref[...]


def dn_mix(p, conv_prev, s0, conv_w, a_log, dt_bias, norm_g, *, t_valid):
    b, t, _ = p.shape
    chunk = min(DN_CHUNK, t)
    tb = min(256, t)
    nb = _seqs_per_step(b, t)
    hist = jnp.pad(conv_prev, ((0, 0), (DN_HALO - (DN_CONV - 1), 0), (0, 0)))
    lane_pad = lambda z: jnp.pad(z, (0, LANES - z.shape[0])).reshape(1, LANES)
    const = lambda shape: pl.BlockSpec(shape, lambda i, j: (0,) * len(shape))
    y, st = pl.pallas_call(
        functools.partial(_dn_body, chunk=chunk, t_valid=t_valid),
        grid=(b // nb, t // tb),
        in_specs=[pl.BlockSpec((nb, tb, DN_PAD), lambda i, j: (i, j, 0)),
                  pl.BlockSpec((nb, DN_HALO, DN_QKV), lambda i, j: (i, 0, 0)),
                  pl.BlockSpec((nb, GROUP_W, HEAD_DIM), lambda i, j: (i, 0, 0)),
                  const((DN_CONV, DN_QKV)), const((1, LANES)), const((1, LANES)), const((1, HEAD_DIM))],
        out_specs=[pl.BlockSpec((nb, tb, GROUP_W), lambda i, j: (i, j, 0)),
                   pl.BlockSpec((nb, GROUP_W, HEAD_DIM), lambda i, j: (i, 0, 0))],
        out_shape=[jax.ShapeDtypeStruct((b, t, GROUP_W), F32),
                   jax.ShapeDtypeStruct((b, GROUP_W, HEAD_DIM), F32)],
        scratch_shapes=[pltpu.VMEM((nb, GROUP_W, HEAD_DIM), F32), pltpu.VMEM((nb, DN_HALO, DN_QKV), F32)],
        compiler_params=_cparams("parallel", "arbitrary"),
        name="dn_mix",
    )(p, hist, s0.reshape(b, GROUP_W, HEAD_DIM), conv_w, lane_pad(a_log), lane_pad(dt_bias),
      norm_g.reshape(1, HEAD_DIM))
    return y, st.reshape(b, N_HEADS, HEAD_DIM, HEAD_DIM)


def _diff_lambda(lq1_ref, lk1_ref, lq2_ref, lk2_ref, lam_init):
    return (jnp.exp(jnp.sum(lq1_ref[...] * lk1_ref[...], axis=-1, keepdims=True))
            - jnp.exp(jnp.sum(lq2_ref[...] * lk2_ref[...], axis=-1, keepdims=True)) + lam_init)


def _head_rms(o, g, scale):
    gr, gcl = _iota2(GROUP_W, GROUP_W)
    head_ones = _same_block(gr, gcl, HEAD_DIM).astype(F32)
    ms = _mm_exact_lhs(o * o, head_ones) * (1.0 / HEAD_DIM)
    return o * lax.rsqrt(ms + RMS_EPS) * g * scale


def _attn_body(q_ref, k_ref, v_ref, lq1_ref, lk1_ref, lq2_ref, lk2_ref, ng_ref, o_ref,
               kb_ref, vt_ref, bias_ref, biasd_ref, qs_ref, *stats, lam_init):
    m_refs, l_refs, acc_refs = stats[:N_BR], stats[N_BR:2 * N_BR], stats[2 * N_BR:]
    b = pl.program_id(0)
    i = pl.program_id(1)
    tq, tk = AT_TQ, AT_TK
    t = k_ref.shape[1]
    n_sub = tq // tk
    slope = [LOG2E * 2.0 ** (-2 * (h + 1)) for h in range(N_HEADS)]

    @pl.when((b == 0) & (i == 0))
    def _():
        jj, ii = _iota2(tk, tq)
        rel = (ii - jj).astype(F32)
        for h in range(N_HEADS):
            bias_ref[:, h * tq:(h + 1) * tq] = -slope[h] * rel
            for d in range(n_sub):
                rd = rel - float(d * tk)
                biasd_ref[d, :, h * tq:(h + 1) * tq] = jnp.where(rd >= 0, -slope[h] * rel, NEG_INF)

    @pl.when(i == 0)
    def _():
        kb_ref[...] = k_ref[0].astype(BF16)
        for j in range(t // tk):
            vt_ref[:, j * tk:(j + 1) * tk] = v_ref[0, j * tk:(j + 1) * tk, :].T.astype(BF16)

    q = q_ref[0] * (DA_QK ** -0.5 * LOG2E)
    lane = lax.broadcasted_iota(jnp.int32, (1, GROUP_W), 1)
    for c in range(N_BR):
        own = (lane >> 5) == c
        qs_ref[c * tq:(c + 1) * tq, :] = jnp.where(own, q, 0.0).astype(BF16)
    for c in range(N_BR):
        m_refs[c][...] = jnp.full_like(m_refs[c], NEG_INF)
        l_refs[c][...] = jnp.zeros_like(l_refs[c])
        acc_refs[c][...] = jnp.zeros_like(acc_refs[c])

    def tile(j, b_ref):
        start = pl.multiple_of(j * tk, tk)
        kt = kb_ref[pl.ds(start, tk), :]
        dist = (tq * i - tk * j).astype(F32)
        ss = [_mm_nt(kt, qs_ref[c * tq:(c + 1) * tq, :]) for c in range(N_BR)]
        ps, alphas = [], []
        for c in range(N_BR):
            h = c >> 1
            s = ss[c] + b_ref[:, h * tq:(h + 1) * tq]
            ct = -slope[h] * dist
            m_old = m_refs[c][...]
            m_new = jnp.maximum(m_old, jnp.max(s, axis=0, keepdims=True) + ct)
            alpha = jnp.exp2(m_old - m_new)
            p = jnp.exp2(s - (m_new - ct))
            l_refs[c][...] = alpha * l_refs[c][...] + jnp.sum(p, axis=0, keepdims=True)
            m_refs[c][...] = m_new
            ps.append(p.astype(BF16))
            alphas.append(alpha)
        for c in range(N_BR):
            h = c >> 1
            pv = jnp.dot(vt_ref[h * HEAD_DIM:(h + 1) * HEAD_DIM, pl.ds(start, tk)], ps[c],
                         preferred_element_type=F32)
            acc_refs[c][...] = acc_refs[c][...] * alphas[c] + pv

    def body(j, carry):
        tile(j, bias_ref)
        return carry

    lax.fori_loop(0, i * n_sub, body, 0)
    for d in range(n_sub):
        tile(i * n_sub + d, biasd_ref.at[d])

    lam = _diff_lambda(lq1_ref, lk1_ref, lq2_ref, lk2_ref, lam_init)
    norm = [acc_refs[c][...] * (1.0 / l_refs[c][...]) for c in range(N_BR)]
    o = jnp.concatenate([norm[2 * h] - lam * norm[2 * h + 1] for h in range(N_HEADS)], axis=0).T
    o_ref[0] = _head_rms(o, ng_ref[...], 1.0 - lam_init)


def attn_prompt(q, k, v, lq1, lk1, lq2, lk2, norm_g, lam_init):
    b, t, _ = q.shape
    tq, tk = AT_TQ, AT_TK
    assert t % tq == 0 and tq % tk == 0
    row = lambda z: z.reshape(1, -1)
    const = lambda shape: pl.BlockSpec(shape, lambda i, j: (0,) * len(shape))
    return pl.pallas_call(
        functools.partial(_attn_body, lam_init=lam_init),
        grid=(b, t // tq),
        in_specs=[pl.BlockSpec((1, tq, GROUP_W), lambda i, j: (i, j, 0)),
                  pl.BlockSpec((1, t, GROUP_W), lambda i, j: (i, 0, 0)),
                  pl.BlockSpec((1, t, GROUP_W), lambda i, j: (i, 0, 0)),
                  const((1, DA_QK)), const((1, DA_QK)), const((1, DA_QK)), const((1, DA_QK)),
                  const((1, GROUP_W))],
        out_specs=pl.BlockSpec((1, tq, GROUP_W), lambda i, j: (i, j, 0)),
        out_shape=jax.ShapeDtypeStruct((b, t, GROUP_W), F32),
        scratch_shapes=[pltpu.VMEM((t, GROUP_W), BF16), pltpu.VMEM((GROUP_W, t), BF16),
                        pltpu.VMEM((tk, N_HEADS * tq), F32), pltpu.VMEM((tq // tk, tk, N_HEADS * tq), F32),
                        pltpu.VMEM((N_BR * tq, GROUP_W), BF16),
                        *([pltpu.VMEM((1, tq), F32)] * (2 * N_BR)), *([pltpu.VMEM((HEAD_DIM, tq), F32)] * N_BR)],
        compiler_params=_cparams("arbitrary", "arbitrary"),
        name="attn_prompt",
    )(q, k, v, row(lq1), row(lk1), row(lq2), row(lk2), row(jnp.tile(norm_g, N_HEADS)))


def _paged_body(pt_ref, q_ref, kn_ref, vn_ref, lq1_ref, lk1_ref, lq2_ref, lk2_ref, ng_ref, *rest,
                lam_init, page, past):
    k_refs = rest[:PG_PAGES]
    v_refs = rest[PG_PAGES:2 * PG_PAGES]
    o_ref, qs_ref, base_ref, m_ref, l_ref, acc_ref = rest[2 * PG_PAGES:]
    s_id = pl.program_id(1)
    span = PG_PAGES * page
    rowc = lax.broadcasted_iota(jnp.int32, (N_BR, 1), 0)
    slope = jnp.exp2(-2.0 * ((rowc >> 1) + 1).astype(F32))

    @pl.when(s_id == 0)
    def _():
        lane = lax.broadcasted_iota(jnp.int32, (N_BR, GROUP_W), 1)
        own = (lane >> 5) == lax.broadcasted_iota(jnp.int32, (N_BR, GROUP_W), 0)
        qs_ref[...] = jnp.where(own, jnp.broadcast_to(q_ref[0] * (DA_QK ** -0.5), (N_BR, GROUP_W)), 0.0)
        col = lax.broadcasted_iota(jnp.int32, (N_BR, span), 1).astype(F32)
        base_ref[...] = slope * col
        m_ref[...] = jnp.full_like(m_ref, NEG_INF)
        l_ref[...] = jnp.zeros_like(l_ref)
        acc_ref[...] = jnp.zeros_like(acc_ref)

    qs = qs_ref[...]
    qb = qs.astype(BF16)
    s = jnp.concatenate(
        [jnp.dot(qb, k_refs[r][0, 0].astype(BF16), preferred_element_type=F32) for r in range(PG_PAGES)],
        axis=1)
    off = -slope * (past - s_id * span).astype(F32)
    s = s + base_ref[...]
    m_old = m_ref[...]
    m_new = jnp.maximum(m_old, jnp.max(s, axis=1, keepdims=True) + off)
    alpha = jnp.exp(m_old - m_new)
    p = jnp.exp(s - (m_new - off))
    l_new = alpha * l_ref[...] + jnp.sum(p, axis=1, keepdims=True)
    pb = p.astype(BF16)
    pv = None
    for r in range(PG_PAGES):
        d = _mm_nt(pb[:, r * page:(r + 1) * page], v_refs[r][0, 0])
        pv = d if pv is None else pv + d
    acc = acc_ref[...] * alpha + pv
    m_ref[...] = m_new
    l_ref[...] = l_new
    acc_ref[...] = acc

    @pl.when(s_id == pl.num_programs(1) - 1)
    def _():
        s_self = jnp.sum(qs * kn_ref[0], axis=1, keepdims=True)
        m_fin = jnp.maximum(m_new, s_self)
        a_fin = jnp.exp(m_new - m_fin)
        p_self = jnp.exp(s_self - m_fin)
        l_fin = a_fin * l_new + p_self
        out = (acc * a_fin + p_self * vn_ref[0]) / l_fin
        lam = _diff_lambda(lq1_ref, lk1_ref, lq2_ref, lk2_ref, lam_init)
        lane = lax.broadcasted_iota(jnp.int32, (N_BR, GROUP_W), 1)
        rown = lax.broadcasted_iota(jnp.int32, (N_BR, GROUP_W), 0)
        coef = jnp.where((rown & 1) == 0, 1.0, -lam)
        o = jnp.sum(jnp.where((lane >> 6) == (rown >> 1), out * coef, 0.0), axis=0, keepdims=True)
        o8 = jnp.broadcast_to(o, (SUBLANES, GROUP_W))
        o_ref[0] = _head_rms(o8, ng_ref[...], 1.0 - lam_init)[0:1]


def attn_sample(q, k_new, v_new, cache_k, cache_v, page_table, layer, lq1, lk1, lq2, lk2, norm_g, lam_init):
    b = q.shape[0]
    depth, n_pool, page = cache_k.shape[:3]
    n_pages = page_table.shape[1]
    assert n_pages % PG_PAGES == 0
    as_pages = lambda c: jnp.transpose(c, (0, 1, 3, 4, 2)).reshape(depth, n_pool, GROUP_W, page)
    ck, cv = as_pages(cache_k), as_pages(cache_v)
    row = lambda z: z.reshape(1, -1)
    const = lambda shape: pl.BlockSpec(shape, lambda i, j, pt: (0,) * len(shape))
    tok = pl.BlockSpec((1, 1, GROUP_W), lambda i, j, pt: (i, 0, 0))
    page_spec = lambda r: pl.BlockSpec((1, 1, GROUP_W, page),
                                       lambda i, j, pt, r=r: (layer, pt[i, j * PG_PAGES + r], 0, 0))
    span = PG_PAGES * page
    grid_spec = pltpu.PrefetchScalarGridSpec(
        num_scalar_prefetch=1,
        grid=(b, n_pages // PG_PAGES),
        in_specs=[tok, tok, tok, const((1, DA_QK)), const((1, DA_QK)), const((1, DA_QK)), const((1, DA_QK)),
                  const((1, GROUP_W))] + [page_spec(r) for r in range(PG_PAGES)] * 2,
        out_specs=pl.BlockSpec((1, 1, GROUP_W), lambda i, j, pt: (i, 0, 0)),
        scratch_shapes=[pltpu.VMEM((N_BR, GROUP_W), F32), pltpu.VMEM((N_BR, span), F32),
                        pltpu.VMEM((N_BR, 1), F32), pltpu.VMEM((N_BR, 1), F32), pltpu.VMEM((N_BR, GROUP_W), F32)])
    r3 = lambda z: z.reshape(b, 1, GROUP_W)
    out = pl.pallas_call(
        functools.partial(_paged_body, lam_init=lam_init, page=page, past=n_pages * page),
        grid_spec=grid_spec,
        out_shape=jax.ShapeDtypeStruct((b, 1, GROUP_W), F32),
        compiler_params=_cparams("arbitrary", "arbitrary"),
        name="attn_sample",
    )(page_table, r3(q), r3(k_new), r3(v_new), row(lq1), row(lk1), row(lq2), row(lk2),
      row(jnp.tile(norm_g, N_HEADS)), *([ck] * PG_PAGES), *([cv] * PG_PAGES))
    return out.reshape(b, GROUP_W)


def _conv_body(p_ref, buf_ref, w_ref, b_ref, lg_ref, lb_ref, y_ref, tail_ref, hist_ref, *, t_valid):
    tb = p_ref.shape[1]
    ti = pl.program_id(1)

    @pl.when(ti == 0)
    def _():
        hist_ref[...] = buf_ref[0]

    x = p_ref[0]
    glu = x[:, :GROUP_W] * _sigmoid(x[:, GROUP_W:])
    ext = jnp.concatenate([hist_ref[...], glu], axis=0)
    hist_new = ext[t_valid:t_valid + CV_HALO, :]
    hist_ref[...] = hist_new
    acc = None
    for j in range(CONV_WIDTH):
        lo = CV_HALO - (CONV_WIDTH - 1) + j
        term = ext[lo:lo + tb, :] * w_ref[j:j + 1, :]
        acc = term if acc is None else acc + term
    h = acc + b_ref[...]
    hc = h - jnp.mean(h, axis=-1, keepdims=True)
    var = jnp.mean(hc * hc, axis=-1, keepdims=True)
    hn = hc * lax.rsqrt(var + LN_EPS) * lg_ref[...] + lb_ref[...]
    y_ref[0] = hn * _sigmoid(hn)

    @pl.when(ti == pl.num_programs(1) - 1)
    def _():
        tail_ref[0] = hist_new


def conv_mix(p, buf, conv_w, conv_b, ln_g, ln_b, *, t_valid):
    b, t, _ = p.shape
    tb = min(512, t)
    assert t == tb or t_valid == t
    keep = CONV_WIDTH - 1
    hist = jnp.pad(buf, ((0, 0), (CV_HALO - keep, 0), (0, 0)))
    row = lambda z: z.reshape(1, -1)
    const = lambda shape: pl.BlockSpec(shape, lambda i, j: (0,) * len(shape))
    y, tail = pl.pallas_call(
        functools.partial(_conv_body, t_valid=min(t_valid, tb)),
        grid=(b, t // tb),
        in_specs=[pl.BlockSpec((1, tb, CV_COLS), lambda i, j: (i, j, 0)),
                  pl.BlockSpec((1, CV_HALO, GROUP_W), lambda i, j: (i, 0, 0)),
                  const((CONV_WIDTH, GROUP_W)), const((1, GROUP_W)), const((1, GROUP_W)), const((1, GROUP_W))],
        out_specs=[pl.BlockSpec((1, tb, GROUP_W), lambda i, j: (i, j, 0)),
                   pl.BlockSpec((1, CV_HALO, GROUP_W), lambda i, j: (i, 0, 0))],
        out_shape=[jax.ShapeDtypeStruct((b, t, GROUP_W), F32),
                   jax.ShapeDtypeStruct((b, CV_HALO, GROUP_W), F32)],
        scratch_shapes=[pltpu.VMEM((CV_HALO, GROUP_W), F32)],
        compiler_params=_cparams("parallel", "arbitrary"),
        name="conv_mix",
    )(p, hist, conv_w, row(conv_b), row(ln_g), row(ln_b))
    return y, tail[:, CV_HALO - keep:]


def trunk_layer(l, x, paged, cv_buf, rw_shift, rw_S, dn_buf, dn_S, P, W):
    b, t, _ = x.shape
    n = b * t
    x2 = x.reshape(n, D_MODEL)
    q, k, v, p_cv, p_rw, p_dn = in_proj(x2, P['g_pre_mix'][l], W['w_in'][l])
    lam_init = 0.8 - 0.6 * math.exp(-0.3 * l)
    lam_args = (P['da_lq1'][l], P['da_lk1'][l], P['da_lq2'][l], P['da_lk2'][l], P['da_norm_g'][l], lam_init)
    if paged is None:
        r3 = lambda z: z.reshape(b, t, GROUP_W)
        o_a = attn_prompt(r3(q), r3(k), r3(v), *lam_args).reshape(n, GROUP_W)
    else:
        assert t == 1
        o_a = attn_sample(q, k, v, *paged, l, *lam_args)
    hs = lambda z: z.reshape(b, t, N_HEADS, HEAD_DIM)
    k, v = hs(k), hs(v)
    t_pad = -(-t // SUBLANES) * SUBLANES
    pad_t = lambda z: z if t_pad == t else jnp.pad(z, ((0, 0), (0, t_pad - t), (0, 0)))
    o_b, cv_new = conv_mix(pad_t(p_cv.reshape(b, t, CV_COLS)), cv_buf, P['cv_w'][l], P['cv_b'][l],
                           P['cv_ln_g'][l], P['cv_ln_b'][l], t_valid=t)
    o_b = o_b[:, :t]
    p_rw = p_rw.reshape(b, t, RW_COLS)
    o_c, rw_S_new = rwkv_mix(
        pad_t(p_rw), rw_shift, rw_S, P['rw_mu'][l], P['rw_w0'][l], P['rw_w_up'][l],
        P['rw_a0'][l], P['rw_a_up'][l], P['rw_g_up'][l], P['rw_k_k'][l], P['rw_k_a'][l],
        P['rw_r_k'][l].reshape(-1), P['rw_gn_g'][l], P['rw_gn_b'][l], t_valid=t)
    o_c = o_c[:, :t]
    rw_shift_new = p_rw[:, -1]
    p_dn = p_dn.reshape(b, t, DN_PAD)
    o_d, dn_S_new = dn_mix(pad_t(p_dn), dn_buf, dn_S, P['dn_conv_w'][l], P['dn_A_log'][l],
                           P['dn_dt_bias'][l], P['dn_norm_g'][l], t_valid=t)
    o_d = o_d[:, :t]
    keep = DN_CONV - 1
    dn_buf_new = (p_dn[:, t - keep:, :DN_QKV] if t >= keep else
                  jnp.concatenate([dn_buf, p_dn[..., :DN_QKV]], axis=1)[:, -keep:])
    x2 = out_proj(x2, o_a, o_b.reshape(n, GROUP_W), o_c.reshape(n, GROUP_W), o_d.reshape(n, GROUP_W),
                  W['w_out'][l], P['g_post_mix'][l])
    x2 = ffn(x2, P['g_pre_ffn'][l], W['ffn_w_gate'][l], W['ffn_w_up'][l], W['ffn_w_down'][l],
             P['g_post_ffn'][l])
    return x2.reshape(b, t, D_MODEL), (k, v, cv_new, rw_shift_new, rw_S_new, dn_buf_new, dn_S_new)


def kernel(x_prompt, x_sample, cache_k, cache_v, page_table, state_conv, state_rw_shift,
           state_rw_wkv, state_dn_conv, state_dn_ssm, g_pre_mix, g_post_mix, g_pre_ffn,
           g_post_ffn, w_in, w_out, da_lq1, da_lk1, da_lq2, da_lk2, da_norm_g, cv_w, cv_b,
           cv_ln_g, cv_ln_b, rw_mu, rw_w0, rw_w_up, rw_a0, rw_a_up, rw_g_up, rw_k_k, rw_k_a,
           rw_r_k, rw_gn_g, rw_gn_b, dn_conv_w, dn_A_log, dn_dt_bias, dn_norm_g,
           ffn_w_gate, ffn_w_up, ffn_w_down):
    P = dict(g_pre_mix=g_pre_mix, g_post_mix=g_post_mix, g_pre_ffn=g_pre_ffn, g_post_ffn=g_post_ffn,
             da_lq1=da_lq1, da_lk1=da_lk1, da_lq2=da_lq2, da_lk2=da_lk2,
             da_norm_g=da_norm_g, cv_w=cv_w, cv_b=cv_b, cv_ln_g=cv_ln_g, cv_ln_b=cv_ln_b,
             rw_mu=rw_mu, rw_w0=rw_w0, rw_w_up=rw_w_up, rw_a0=rw_a0, rw_a_up=rw_a_up,
             rw_g_up=rw_g_up, rw_k_k=rw_k_k, rw_k_a=rw_k_a, rw_r_k=rw_r_k, rw_gn_g=rw_gn_g,
             rw_gn_b=rw_gn_b, dn_conv_w=dn_conv_w, dn_A_log=dn_A_log, dn_dt_bias=dn_dt_bias,
             dn_norm_g=dn_norm_g)
    depth = w_in.shape[0]
    W = dict(w_in=[_prep_w_in(w_in[l]) for l in range(depth)],
             w_out=[w_out[l].astype(BF16) for l in range(depth)],
             ffn_w_gate=[ffn_w_gate[l].astype(BF16) for l in range(depth)],
             ffn_w_up=[ffn_w_up[l].astype(BF16) for l in range(depth)],
             ffn_w_down=[ffn_w_down[l].astype(BF16) for l in range(depth)])
    bp = x_prompt.shape[0]
    dtp = x_prompt.dtype
    yp = x_prompt
    ys = x_sample
    outs_p = []
    outs_s = []
    for l in range(depth):
        yp, st_p = trunk_layer(
            l, yp, None,
            jnp.zeros((bp, CONV_WIDTH - 1, GROUP_W), dtp), jnp.zeros((bp, RW_COLS), dtp),
            jnp.zeros((bp, N_HEADS, HEAD_DIM, HEAD_DIM), jnp.float32),
            jnp.zeros((bp, DN_CONV - 1, DN_QKV), dtp),
            jnp.zeros((bp, N_HEADS, HEAD_DIM, HEAD_DIM), jnp.float32), P, W)
        ys, st_s = trunk_layer(l, ys, (cache_k, cache_v, page_table), state_conv[l], state_rw_shift[l],
                               state_rw_wkv[l], state_dn_conv[l], state_dn_ssm[l], P, W)
        outs_p.append(st_p)
        outs_s.append(st_s)
    stk = lambda outs, i: jnp.stack([o[i] for o in outs])
    return (yp, ys, stk(outs_p, 0), stk(outs_p, 1), stk(outs_s, 0), stk(outs_s, 1),
            stk(outs_p, 2), stk(outs_s, 2), stk(outs_p, 3), stk(outs_s, 3),
            stk(outs_p, 4), stk(outs_s, 4), stk(outs_p, 5), stk(outs_s, 5),
            stk(outs_p, 6), stk(outs_s, 6))
```

```python
import functools
import math

import jax
import jax.numpy as jnp
from jax import lax
from jax.experimental import pallas as pl
from jax.experimental.pallas import tpu as pltpu

D_MODEL = 1024
HEAD_DIM = 64
N_HEADS = 4
GROUP_W = 256
DA_QK = HEAD_DIM // 2
CONV_WIDTH = 31
RW_W_RANK = 64
RW_A_RANK = 64
RW_G_RANK = 128
RW_OFF_W = 3 * GROUP_W
RW_OFF_A = RW_OFF_W + RW_W_RANK
RW_OFF_G = RW_OFF_A + RW_A_RANK
RW_COLS = RW_OFF_G + RW_G_RANK
DN_CONV = 4
DN_CHUNK = 64
DN_QKV = 3 * GROUP_W
DN_COLS = DN_QKV + 2 * N_HEADS + GROUP_W
DA_COLS = 3 * GROUP_W
CV_COLS = 2 * GROUP_W
OFF_CV = DA_COLS
OFF_RW = OFF_CV + CV_COLS
OFF_DN = OFF_RW + RW_COLS
N_IN = OFF_DN + DN_COLS
FFN_HIDDEN = 2816
RMS_EPS = 1e-6
LN_EPS = 1e-5
RW_GN_EPS = 64e-5
NEG_INF = -1e30
LOG2E = math.log2(math.e)

LANES = 128
SUBLANES = 8
DN_PAD = DN_QKV + GROUP_W + LANES
DN_HALO = 8
CV_HALO = 32
N_BR = 2 * N_HEADS
AT_TQ = 512
AT_TK = 256
PG_PAGES = 16
PG_SLOTS = 3
VMEM_LIMIT = 56 * 1024 * 1024

F32 = jnp.float32
BF16 = jnp.bfloat16


def _cparams(*sem):
    return pltpu.CompilerParams(dimension_semantics=sem, vmem_limit_bytes=VMEM_LIMIT)


def _rms(x, g):
    return x * lax.rsqrt(jnp.mean(x * x, axis=-1, keepdims=True) + RMS_EPS) * g


def _row_tile(n, target):
    return target if n % target == 0 else n


def _in_proj_body(x_ref, g_ref, w_ref, q_ref, k_ref, v_ref, cv_ref, rw_ref, dn_ref):
    hn = _rms(x_ref[...], g_ref[...]).astype(BF16)
    off = 0
    for ref in (q_ref, k_ref, v_ref, cv_ref, rw_ref, dn_ref):
        w = ref.shape[-1]
        ref[...] = jnp.dot(hn, w_ref[:, off:off + w], preferred_element_type=F32)
        off += w


def _prep_w_in(w):
    dn = w[:, OFF_DN:]
    dn = jnp.concatenate([dn[:, :DN_QKV], dn[:, DN_QKV + 2 * N_HEADS:], dn[:, DN_QKV:DN_QKV + 2 * N_HEADS],
                          jnp.zeros((w.shape[0], LANES - 2 * N_HEADS), w.dtype)], axis=1)
    return jnp.concatenate([w[:, :OFF_DN], dn], axis=1).astype(BF16)


def in_proj(x, g, w):
    n = x.shape[0]
    tm = _row_tile(n, 512)
    widths = (GROUP_W, GROUP_W, GROUP_W, CV_COLS, RW_COLS, DN_PAD)
    row = lambda wd: pl.BlockSpec((tm, wd), lambda i: (i, 0))
    return pl.pallas_call(
        _in_proj_body,
        grid=(n // tm,),
        in_specs=[row(D_MODEL), pl.BlockSpec((1, D_MODEL), lambda i: (0, 0)),
                  pl.BlockSpec(w.shape, lambda i: (0, 0))],
        out_specs=[row(wd) for wd in widths],
        out_shape=[jax.ShapeDtypeStruct((n, wd), F32) for wd in widths],
        compiler_params=_cparams("parallel"),
        name="in_proj",
    )(x, g.reshape(1, -1), w)


def _out_proj_body(x_ref, a_ref, b_ref, c_ref, d_ref, w_ref, g_ref, o_ref):
    acc = None
    for i, ref in enumerate((a_ref, b_ref, c_ref, d_ref)):
        part = jnp.dot(ref[...].astype(BF16), w_ref[i * GROUP_W:(i + 1) * GROUP_W, :],
                       preferred_element_type=F32)
        acc = part if acc is None else acc + part
    o_ref[...] = x_ref[...] + _rms(acc, g_ref[...])


def out_proj(x, oa, ob, oc, od, w, g):
    n = x.shape[0]
    tm = _row_tile(n, 512)
    row = lambda wd: pl.BlockSpec((tm, wd), lambda i: (i, 0))
    return pl.pallas_call(
        _out_proj_body,
        grid=(n // tm,),
        in_specs=[row(D_MODEL), row(GROUP_W), row(GROUP_W), row(GROUP_W), row(GROUP_W),
                  pl.BlockSpec(w.shape, lambda i: (0, 0)), pl.BlockSpec((1, D_MODEL), lambda i: (0, 0))],
        out_specs=row(D_MODEL),
        out_shape=jax.ShapeDtypeStruct((n, D_MODEL), F32),
        compiler_params=_cparams("parallel"),
        name="out_proj",
    )(x, oa, ob, oc, od, w, g.reshape(1, -1))


def _ffn_body(x_ref, g1_ref, wg_ref, wu_ref, wd_ref, g2_ref, o_ref, hn_ref, acc_ref):
    j = pl.program_id(1)

    @pl.when(j == 0)
    def _():
        hn_ref[...] = _rms(x_ref[...], g1_ref[...]).astype(BF16)
        acc_ref[...] = jnp.zeros_like(acc_ref)

    hn = hn_ref[...]
    gate = jnp.dot(hn, wg_ref[...], preferred_element_type=F32)
    up = jnp.dot(hn, wu_ref[...], preferred_element_type=F32)
    f = (gate * jax.nn.sigmoid(gate) * up).astype(BF16)
    acc_ref[...] += jnp.dot(f, wd_ref[...], preferred_element_type=F32)

    @pl.when(j == pl.num_programs(1) - 1)
    def _():
        o_ref[...] = x_ref[...] + _rms(acc_ref[...], g2_ref[...])


def ffn(x, g1, wg, wu, wd, g2):
    n = x.shape[0]
    tm = _row_tile(n, 512)
    th = FFN_HIDDEN // 2
    return pl.pallas_call(
        _ffn_body,
        grid=(n // tm, FFN_HIDDEN // th),
        in_specs=[pl.BlockSpec((tm, D_MODEL), lambda i, j: (i, 0)),
                  pl.BlockSpec((1, D_MODEL), lambda i, j: (0, 0)),
                  pl.BlockSpec((D_MODEL, th), lambda i, j: (0, j)),
                  pl.BlockSpec((D_MODEL, th), lambda i, j: (0, j)),
                  pl.BlockSpec((th, D_MODEL), lambda i, j: (j, 0)),
                  pl.BlockSpec((1, D_MODEL), lambda i, j: (0, 0))],
        out_specs=pl.BlockSpec((tm, D_MODEL), lambda i, j: (i, 0)),
        out_shape=jax.ShapeDtypeStruct((n, D_MODEL), F32),
        scratch_shapes=[pltpu.VMEM((tm, D_MODEL), BF16), pltpu.VMEM((tm, D_MODEL), F32)],
        compiler_params=_cparams("parallel", "arbitrary"),
        name="ffn",
    )(x, g1.reshape(1, -1), wg, wu, wd, g2.reshape(1, -1))


def _mm(a, b):
    return jnp.dot(a.astype(BF16), b.astype(BF16), preferred_element_type=F32)


def _mm_nt(a, b):
    return lax.dot_general(a.astype(BF16), b.astype(BF16), (((1,), (1,)), ((), ())),
                           preferred_element_type=F32)


def _mm_tn(a, b):
    return lax.dot_general(a.astype(BF16), b.astype(BF16), (((0,), (0,)), ((), ())),
                           preferred_element_type=F32)


def _split3(x):
    hi = x.astype(BF16)
    r1 = x - hi.astype(F32)
    mid = r1.astype(BF16)
    lo = (r1 - mid.astype(F32)).astype(BF16)
    return hi, mid, lo


def _mm_exact_rhs(a01, x):
    a = a01.astype(BF16)
    hi, mid, lo = _split3(x)
    d = lambda y: jnp.dot(a, y, preferred_element_type=F32)
    return d(hi) + (d(mid) + d(lo))


def _mm_exact_lhs(x, a01):
    a = a01.astype(BF16)
    hi, mid, lo = _split3(x)
    d = lambda y: jnp.dot(y, a, preferred_element_type=F32)
    return d(hi) + (d(mid) + d(lo))


def _iota2(n, m):
    return lax.broadcasted_iota(jnp.int32, (n, m), 0), lax.broadcasted_iota(jnp.int32, (n, m), 1)


def _same_block(r, c, blk):
    s = int(math.log2(blk))
    return (r >> s) == (c >> s)


def _unit_lower_inverse(mats, r, c, chunk):
    eye = (r == c).astype(F32)
    base = min(8, chunk)
    blk = _same_block(r, c, base)
    pw = [jnp.where(blk, a, 0.0) for a in mats]
    d = [eye + p for p in pw]
    for _ in range(max(int(math.log2(base)) - 1, 0)):
        pw = [_mm(p, p) for p in pw]
        d = [x + _mm(x, p) for x, p in zip(d, pw)]
    size = base * 2
    while size <= chunk:
        ring = _same_block(r, c, size) & ~_same_block(r, c, size // 2)
        t = [_mm(jnp.where(ring, a, 0.0), x) for a, x in zip(mats, d)]
        d = [x + _mm(x, y) for x, y in zip(d, t)]
        size *= 2
    return d


def _softplus(z):
    return jnp.maximum(z, 0.0) + jnp.log(1.0 + jnp.exp(-jnp.abs(z)))


def _sigmoid(z):
    return 1.0 / (1.0 + jnp.exp(-z))


def _head_stackers(chunk):
    hw = N_HEADS * chunk
    lr, lc = _iota2(hw, GROUP_W)
    own = (lr >> int(math.log2(chunk))) == (lc >> int(math.log2(HEAD_DIM)))

    def stack_bd(z):
        return jnp.where(own, jnp.concatenate([z] * N_HEADS, axis=0), 0.0)

    def stack_heads(z):
        return jnp.concatenate([z[:, h * HEAD_DIM:(h + 1) * HEAD_DIM] for h in range(N_HEADS)], axis=0)

    return stack_bd, stack_heads


def _row_to_col(row_vals, r, c):
    n = r.shape[0]
    return jnp.sum(jnp.where(r == c, jnp.broadcast_to(row_vals, (n, n)), 0.0), axis=1, keepdims=True)


def _seqs_per_step(b, t):
    want = 2 if t >= HEAD_DIM else 8
    return want if b % want == 0 else 1


def _rwkv_body(p_ref, shift_ref, s0_ref, mu_ref, w0_ref, wup_ref, a0_ref, aup_ref, gup_ref, kk_ref, ka_ref,
               rk_ref, gng_ref, gnb_ref, y_ref, s_out_ref, st_ref, prev_ref, *, chunk, t_valid):
    nb, tb = p_ref.shape[0], p_ref.shape[1]
    ti = pl.program_id(1)
    hw = N_HEADS * chunk

    @pl.when(ti == 0)
    def _():
        st_ref[...] = s0_ref[...]
        prev_ref[...] = shift_ref[...]

    row = lax.broadcasted_iota(jnp.int32, (tb, 1), 0)
    gr, gcl = _iota2(GROUP_W, GROUP_W)
    head_ones = _same_block(gr, gcl, HEAD_DIM).astype(F32)
    tr, tc = _iota2(tb, tb)
    cum = (_same_block(tr, tc, chunk) & (tr >= tc)).astype(F32)
    sr, sc = _iota2(hw, hw)
    head_blk = _same_block(sr, sc, chunk)
    low_incl = head_blk & (sr >= sc)
    low_strict = head_blk & (sr > sc)
    stack_bd, stack_heads = _head_stackers(chunk)
    slices = [slice(c0, c0 + chunk) for c0 in range(0, tb, chunk)]

    def prepare(s):
        x = p_ref[s]
        prev = jnp.where(row == 0, prev_ref[s], pltpu.roll(x, 1, axis=0))
        prev_ref[s] = x[tb - 1:tb, :]
        xm = x + (prev - x) * mu_ref[...]
        r = xm[:, :GROUP_W]
        k = xm[:, GROUP_W:2 * GROUP_W]
        v = xm[:, 2 * GROUP_W:RW_OFF_W]
        wd = xm[:, RW_OFF_W:RW_OFF_A]
        ad = xm[:, RW_OFF_A:RW_OFF_G]
        gd = xm[:, RW_OFF_G:RW_COLS]
        w_log = -_softplus(-(w0_ref[...] + _mm(jnp.tanh(wd), wup_ref[...]))) - 0.5
        logw = -jnp.exp(w_log)
        a = _sigmoid(a0_ref[...] + _mm(ad, aup_ref[...]))
        g = _mm(_sigmoid(gd), gup_ref[...])
        kk = k * kk_ref[...]
        kk = kk * lax.rsqrt(_mm_exact_lhs(kk * kk, head_ones) + 1e-6)
        k2 = k * (1.0 + (a - 1.0) * ka_ref[...])
        bonus = _mm_exact_lhs(r * k2 * rk_ref[...], head_ones) * v
        if t_valid < tb:
            ok = row < t_valid
            logw = jnp.where(ok, logw, 0.0)
            kk = jnp.where(ok, kk, 0.0)
            k2 = jnp.where(ok, k2, 0.0)
            v = jnp.where(ok, v, 0.0)
        alpha = -kk
        beta = kk * a
        gcum = _mm_exact_rhs(cum, logw)
        pre = []
        for sl in slices:
            gc = gcum[sl]
            g_last = gc[chunk - 1:chunk, :]
            e_neg = jnp.exp(-gc)
            e_rem = jnp.exp(g_last - gc)
            r_t = stack_bd(r[sl] * jnp.exp(gc))
            a_t = stack_bd(alpha[sl] * jnp.exp(gc - logw[sl]))
            k_t = stack_bd(k2[sl] * e_neg)
            b_t = stack_bd(beta[sl] * e_neg)
            vs = stack_heads(v[sl])
            pre.append(dict(
                r_t=r_t, a_t=a_t, vs=vs,
                a_ab=jnp.where(low_strict, _mm_nt(a_t, b_t), 0.0),
                a_ak_v=_mm(jnp.where(low_strict, _mm_nt(a_t, k_t), 0.0), vs),
                a_rb=jnp.where(low_incl, _mm_nt(r_t, b_t), 0.0),
                a_rk_v=_mm(jnp.where(low_incl, _mm_nt(r_t, k_t), 0.0), vs),
                kv=_mm_tn(stack_bd(k2[sl] * e_rem), vs), b_hat=stack_bd(beta[sl] * e_rem),
                p_col=_row_to_col(jnp.exp(g_last), gr, gcl)))
        return pre, bonus, g

    seqs = [prepare(s) for s in range(nb)]
    flat = [c for pre, _, _ in seqs for c in pre]
    tinvs = _unit_lower_inverse([c["a_ab"] for c in flat], sr, sc, chunk)
    for c, tinv in zip(flat, tinvs):
        c["w_bd"] = _mm(tinv, c["a_t"])
        c["u_loc"] = _mm(tinv, c["a_ak_v"])
    sts = [st_ref[s] for s in range(nb)]
    for ci, sl in enumerate(slices):
        for s in range(nb):
            c, st = seqs[s][0][ci], sts[s]
            u = c["u_loc"] + _mm(c["w_bd"], st)
            y = _mm(c["r_t"], st) + c["a_rk_v"] + _mm(c["a_rb"], u)
            sts[s] = st * c["p_col"] + c["kv"] + _mm_tn(c["b_hat"], u)
            yc = y - jnp.mean(y, axis=-1, keepdims=True)
            yn = yc * lax.rsqrt(jnp.mean(yc * yc, axis=-1, keepdims=True) + RW_GN_EPS)
            for h in range(N_HEADS):
                y_ref[s, sl, h * HEAD_DIM:(h + 1) * HEAD_DIM] = yn[h * chunk:(h + 1) * chunk, :]
    for s in range(nb):
        st_ref[s] = sts[s]
        y_ref[s] = (y_ref[s] * gng_ref[...] + gnb_ref[...] + seqs[s][1]) * seqs[s][2]

    @pl.when(ti == pl.num_programs(1) - 1)
    def _():
        s_out_ref[...] = st_ref[...]


def rwkv_mix(p, shift_prev, s0, mu, w0, w_up, a0, a_up, g_up, k_k, k_a, r_k, gn_g, gn_b, *, t_valid):
    b, t, _ = p.shape
    chunk = min(HEAD_DIM, t)
    tb = min(256, t)
    nb = _seqs_per_step(b, t)
    row = lambda z: z.reshape(1, -1)
    st0 = jnp.swapaxes(s0, 2, 3).reshape(b, GROUP_W, HEAD_DIM)
    const = lambda shape: pl.BlockSpec(shape, lambda i, j: (0,) * len(shape))
    y, st = pl.pallas_call(
        functools.partial(_rwkv_body, chunk=chunk, t_valid=t_valid),
        grid=(b // nb, t // tb),
        in_specs=[pl.BlockSpec((nb, tb, RW_COLS), lambda i, j: (i, j, 0)),
                  pl.BlockSpec((nb, 1, RW_COLS), lambda i, j: (i, 0, 0)),
                  pl.BlockSpec((nb, GROUP_W, HEAD_DIM), lambda i, j: (i, 0, 0)),
                  const((1, RW_COLS)), const((1, GROUP_W)), const((RW_W_RANK, GROUP_W)),
                  const((1, GROUP_W)), const((RW_A_RANK, GROUP_W)), const((RW_G_RANK, GROUP_W)),
                  const((1, GROUP_W)), const((1, GROUP_W)), const((1, GROUP_W)),
                  const((1, GROUP_W)), const((1, GROUP_W))],
        out_specs=[pl.BlockSpec((nb, tb, GROUP_W), lambda i, j: (i, j, 0)),
                   pl.BlockSpec((nb, GROUP_W, HEAD_DIM), lambda i, j: (i, 0, 0))],
        out_shape=[jax.ShapeDtypeStruct((b, t, GROUP_W), F32),
                   jax.ShapeDtypeStruct((b, GROUP_W, HEAD_DIM), F32)],
        scratch_shapes=[pltpu.VMEM((nb, GROUP_W, HEAD_DIM), F32), pltpu.VMEM((nb, 1, RW_COLS), F32)],
        compiler_params=_cparams("parallel", "arbitrary"),
        name="rwkv_mix",
    )(p, shift_prev.reshape(b, 1, RW_COLS), st0, row(mu), row(w0), w_up, row(a0), a_up, g_up,
      row(k_k), row(k_a), row(r_k), row(gn_g), row(gn_b))
    return y, jnp.swapaxes(st.reshape(b, N_HEADS, HEAD_DIM, HEAD_DIM), 2, 3)


def _dn_body(p_ref, cprev_ref, s0_ref, cw_ref, alog_ref, dtb_ref, ng_ref, y_ref, s_out_ref, st_ref, hist_ref,
             *, chunk, t_valid):
    nb, tb = p_ref.shape[0], p_ref.shape[1]
    ti = pl.program_id(1)
    hw = N_HEADS * chunk

    @pl.when(ti == 0)
    def _():
        st_ref[...] = s0_ref[...]
        hist_ref[...] = cprev_ref[...]

    gr, gcl = _iota2(GROUP_W, GROUP_W)
    head_ones = _same_block(gr, gcl, HEAD_DIM).astype(F32)
    er, ec = _iota2(LANES, GROUP_W)
    lanes_a = (er == (ec >> 6)).astype(F32)
    lanes_b = (er == (ec >> 6) + N_HEADS).astype(F32)
    tr, tc = _iota2(tb, tb)
    cum = (_same_block(tr, tc, chunk) & (tr >= tc)).astype(F32)
    sr, sc = _iota2(hw, hw)
    head_blk = _same_block(sr, sc, chunk)
    low_incl = head_blk & (sr >= sc)
    low_strict = head_blk & (sr > sc)
    eye_hw = sr == sc
    stack_bd, stack_heads = _head_stackers(chunk)
    slices = [slice(c0, c0 + chunk) for c0 in range(0, tb, chunk)]

    def prepare(s):
        x = p_ref[s]
        ext = jnp.concatenate([hist_ref[s], x[:, :DN_QKV]], axis=0)
        hist_ref[s] = ext[tb:tb + DN_HALO, :]
        conv = None
        for j in range(DN_CONV):
            lo = DN_HALO - (DN_CONV - 1) + j
            term = ext[lo:lo + tb, :] * cw_ref[j:j + 1, :]
            conv = term if conv is None else conv + term
        qkv = conv * _sigmoid(conv)
        q = qkv[:, :GROUP_W]
        k = qkv[:, GROUP_W:2 * GROUP_W]
        v = qkv[:, 2 * GROUP_W:]
        q = q * lax.rsqrt(_mm_exact_lhs(q * q, head_ones) + 1e-6) * (HEAD_DIM ** -0.5)
        k = k * lax.rsqrt(_mm_exact_lhs(k * k, head_ones) + 1e-6)
        z = x[:, DN_QKV:DN_QKV + GROUP_W]
        ab = x[:, DN_QKV + GROUP_W:]
        g_w = _mm_exact_lhs(-jnp.exp(alog_ref[...]) * _softplus(ab + dtb_ref[...]), lanes_a)
        beta = _mm_exact_lhs(_sigmoid(ab), lanes_b)
        if t_valid < tb:
            ok = lax.broadcasted_iota(jnp.int32, (tb, 1), 0) < t_valid
            g_w = jnp.where(ok, g_w, 0.0)
            beta = jnp.where(ok, beta, 0.0)
            k = jnp.where(ok, k, 0.0)
            v = jnp.where(ok, v, 0.0)
        gcum = _mm_exact_rhs(cum, g_w)
        kb = k * beta
        vb = v * beta
        pre = []
        for sl in slices:
            gc = gcum[sl]
            g_last = gc[chunk - 1:chunk, :]
            gi = jnp.broadcast_to(stack_heads(gc)[:, :1], (hw, hw))
            gj = jnp.sum(jnp.where(eye_hw, gi, 0.0), axis=0, keepdims=True)
            decay = jnp.exp(jnp.where(low_incl, gi - gj, NEG_INF))
            k_bd = stack_bd(k[sl])
            pre.append(dict(
                m=jnp.where(low_strict, _mm_nt(stack_bd(kb[sl]), k_bd) * decay, 0.0),
                qk=jnp.where(low_incl, _mm_nt(stack_bd(q[sl]), k_bd) * decay, 0.0),
                kb_g=stack_bd(kb[sl] * jnp.exp(gc)), q_g=stack_bd(q[sl] * jnp.exp(gc)), vb=stack_heads(vb[sl]),
                k_rem=stack_bd(k[sl] * jnp.exp(g_last - gc)), p_col=_row_to_col(jnp.exp(g_last), gr, gcl)))
        return pre, z

    seqs = [prepare(s) for s in range(nb)]
    flat = [c for pre, _ in seqs for c in pre]
    tinvs = _unit_lower_inverse([-c["m"] for c in flat], sr, sc, chunk)
    for c, tinv in zip(flat, tinvs):
        c["w_bd"] = _mm(tinv, c["kb_g"])
        c["u_loc"] = _mm(tinv, c["vb"])
    sts = [st_ref[s] for s in range(nb)]
    for ci, sl in enumerate(slices):
        for s in range(nb):
            c, st = seqs[s][0][ci], sts[s]
            v_new = c["u_loc"] - _mm(c["w_bd"], st)
            o = _mm(c["q_g"], st) + _mm(c["qk"], v_new)
            sts[s] = st * c["p_col"] + _mm_tn(c["k_rem"], v_new)
            on = o * lax.rsqrt(jnp.mean(o * o, axis=-1, keepdims=True) + RMS_EPS) * ng_ref[...]
            for h in range(N_HEADS):
                y_ref[s, sl, h * HEAD_DIM:(h + 1) * HEAD_DIM] = on[h * chunk:(h + 1) * chunk, :]
    for s in range(nb):
        st_ref[s] = sts[s]
        z = seqs[s][1]
        y_ref[s] = y_ref[s] * (z * _sigmoid(z))

    @pl.when(ti == pl.num_programs(1) - 1)
    def _():
        s_out_ref[...] = st_ref[...]


def dn_mix(p, conv_prev, s0, conv_w, a_log, dt_bias, norm_g, *, t_valid):
    b, t, _ = p.shape
    chunk = min(DN_CHUNK, t)
    tb = min(256, t)
    nb = _seqs_per_step(b, t)
    hist = jnp.pad(conv_prev, ((0, 0), (DN_HALO - (DN_CONV - 1), 0), (0, 0)))
    lane_pad = lambda z: jnp.pad(z, (0, LANES - z.shape[0])).reshape(1, LANES)
    const = lambda shape: pl.BlockSpec(shape, lambda i, j: (0,) * len(shape))
    y, st = pl.pallas_call(
        functools.partial(_dn_body, chunk=chunk, t_valid=t_valid),
        grid=(b // nb, t // tb),
        in_specs=[pl.BlockSpec((nb, tb, DN_PAD), lambda i, j: (i, j, 0)),
                  pl.BlockSpec((nb, DN_HALO, DN_QKV), lambda i, j: (i, 0, 0)),
                  pl.BlockSpec((nb, GROUP_W, HEAD_DIM), lambda i, j: (i, 0, 0)),
                  const((DN_CONV, DN_QKV)), const((1, LANES)), const((1, LANES)), const((1, HEAD_DIM))],
        out_specs=[pl.BlockSpec((nb, tb, GROUP_W), lambda i, j: (i, j, 0)),
                   pl.BlockSpec((nb, GROUP_W, HEAD_DIM), lambda i, j: (i, 0, 0))],
        out_shape=[jax.ShapeDtypeStruct((b, t, GROUP_W), F32),
                   jax.ShapeDtypeStruct((b, GROUP_W, HEAD_DIM), F32)],
        scratch_shapes=[pltpu.VMEM((nb, GROUP_W, HEAD_DIM), F32), pltpu.VMEM((nb, DN_HALO, DN_QKV), F32)],
        compiler_params=_cparams("parallel", "arbitrary"),
        name="dn_mix",
    )(p, hist, s0.reshape(b, GROUP_W, HEAD_DIM), conv_w, lane_pad(a_log), lane_pad(dt_bias),
      norm_g.reshape(1, HEAD_DIM))
    return y, st.reshape(b, N_HEADS, HEAD_DIM, HEAD_DIM)


def _diff_lambda(lq1_ref, lk1_ref, lq2_ref, lk2_ref, lam_init):
    return (jnp.exp(jnp.sum(lq1_ref[...] * lk1_ref[...], axis=-1, keepdims=True))
            - jnp.exp(jnp.sum(lq2_ref[...] * lk2_ref[...], axis=-1, keepdims=True)) + lam_init)


def _head_rms(o, g, scale):
    gr, gcl = _iota2(GROUP_W, GROUP_W)
    head_ones = _same_block(gr, gcl, HEAD_DIM).astype(F32)
    ms = _mm_exact_lhs(o * o, head_ones) * (1.0 / HEAD_DIM)
    return o * lax.rsqrt(ms + RMS_EPS) * g * scale


def _attn_body(q_ref, k_ref, v_ref, lq1_ref, lk1_ref, lq2_ref, lk2_ref, ng_ref, o_ref,
               kb_ref, vt_ref, bias_ref, biasd_ref, qs_ref, *stats, lam_init):
    m_refs, acc_refs = stats[:N_BR], stats[N_BR:]
    b = pl.program_id(0)
    i = pl.program_id(1)
    tq, tk = AT_TQ, AT_TK
    t = k_ref.shape[1]
    n_sub = tq // tk
    slope = [LOG2E * 2.0 ** (-2 * (h + 1)) for h in range(N_HEADS)]

    @pl.when((b == 0) & (i == 0))
    def _():
        jj, ii = _iota2(tk, tq)
        rel = (ii - jj).astype(F32)
        for h in range(N_HEADS):
            bias_ref[:, h * tq:(h + 1) * tq] = -slope[h] * rel
            for d in range(n_sub):
                rd = rel - float(d * tk)
                biasd_ref[d, :, h * tq:(h + 1) * tq] = jnp.where(rd >= 0, -slope[h] * rel, NEG_INF)

    @pl.when(i == 0)
    def _():
        kb_ref[...] = k_ref[0].astype(BF16)
        for j in range(t // tk):
            vt_ref[:, j * tk:(j + 1) * tk] = v_ref[0, j * tk:(j + 1) * tk, :].T.astype(BF16)

    q = q_ref[0] * (DA_QK ** -0.5 * LOG2E)
    lane = lax.broadcasted_iota(jnp.int32, (1, GROUP_W), 1)
    for c in range(N_BR):
        own = (lane >> 5) == c
        qs_ref[c * tq:(c + 1) * tq, :] = jnp.where(own, q, 0.0).astype(BF16)
    for c in range(N_BR):
        m_refs[c][...] = jnp.full_like(m_refs[c], NEG_INF)
        acc_refs[c][...] = jnp.zeros_like(acc_refs[c])

    def tile(j, b_ref):
        start = pl.multiple_of(j * tk, tk)
        kt = kb_ref[pl.ds(start, tk), :]
        dist = (tq * i - tk * j).astype(F32)
        ss = [_mm_nt(kt, qs_ref[c * tq:(c + 1) * tq, :]) for c in range(N_BR)]
        ps, alphas = [], []
        for c in range(N_BR):
            h = c >> 1
            s = ss[c] + b_ref[:, h * tq:(h + 1) * tq]
            ct = -slope[h] * dist
            m_old = m_refs[c][...]
            m_new = jnp.maximum(m_old, jnp.max(s, axis=0, keepdims=True) + ct)
            alpha = jnp.exp2(m_old - m_new)
            p = jnp.exp2(s - (m_new - ct))
            m_refs[c][...] = m_new
            ps.append(p.astype(BF16))
            alphas.append(alpha)
        ones = jnp.ones((2 * SUBLANES, tk), BF16)
        for c in range(N_BR):
            h = c >> 1
            lhs = jnp.concatenate([vt_ref[h * HEAD_DIM:(h + 1) * HEAD_DIM, pl.ds(start, tk)], ones], axis=0)
            pv = jnp.dot(lhs, ps[c], preferred_element_type=F32)
            acc_refs[c][...] = acc_refs[c][...] * alphas[c] + pv

    def body(j, carry):
        tile(j, bias_ref)
        return carry

    lax.fori_loop(0, i * n_sub, body, 0)
    for d in range(n_sub):
        tile(i * n_sub + d, biasd_ref.at[d])

    lam = _diff_lambda(lq1_ref, lk1_ref, lq2_ref, lk2_ref, lam_init)
    norm = [acc_refs[c][:HEAD_DIM, :] * (1.0 / acc_refs[c][HEAD_DIM:HEAD_DIM + 1, :]) for c in range(N_BR)]
    o = jnp.concatenate([norm[2 * h] - lam * norm[2 * h + 1] for h in range(N_HEADS)], axis=0).T
    o_ref[0] = _head_rms(o, ng_ref[...], 1.0 - lam_init)


def attn_prompt(q, k, v, lq1, lk1, lq2, lk2, norm_g, lam_init):
    b, t, _ = q.shape
    tq, tk = AT_TQ, AT_TK
    assert t % tq == 0 and tq % tk == 0
    row = lambda z: z.reshape(1, -1)
    const = lambda shape: pl.BlockSpec(shape, lambda i, j: (0,) * len(shape))
    return pl.pallas_call(
        functools.partial(_attn_body, lam_init=lam_init),
        grid=(b, t // tq),
        in_specs=[pl.BlockSpec((1, tq, GROUP_W), lambda i, j: (i, j, 0)),
                  pl.BlockSpec((1, t, GROUP_W), lambda i, j: (i, 0, 0)),
                  pl.BlockSpec((1, t, GROUP_W), lambda i, j: (i, 0, 0)),
                  const((1, DA_QK)), const((1, DA_QK)), const((1, DA_QK)), const((1, DA_QK)),
                  const((1, GROUP_W))],
        out_specs=pl.BlockSpec((1, tq, GROUP_W), lambda i, j: (i, j, 0)),
        out_shape=jax.ShapeDtypeStruct((b, t, GROUP_W), F32),
        scratch_shapes=[pltpu.VMEM((t, GROUP_W), BF16), pltpu.VMEM((GROUP_W, t), BF16),
                        pltpu.VMEM((tk, N_HEADS * tq), F32), pltpu.VMEM((tq // tk, tk, N_HEADS * tq), F32),
                        pltpu.VMEM((N_BR * tq, GROUP_W), BF16),
                        *([pltpu.VMEM((1, tq), F32)] * N_BR),
                        *([pltpu.VMEM((HEAD_DIM + 2 * SUBLANES, tq), F32)] * N_BR)],
        compiler_params=_cparams("arbitrary", "arbitrary"),
        name="attn_prompt",
    )(q, k, v, row(lq1), row(lk1), row(lq2), row(lk2), row(jnp.tile(norm_g, N_HEADS)))


def _paged_body(pt_ref, q_ref, kn_ref, vn_ref, lq1_ref, lk1_ref, lq2_ref, lk2_ref, ng_ref, ck_hbm, cv_hbm, o_ref,
                kbuf, vbuf, sem, qs_ref, base_ref, m_ref, l_ref, acc_ref, *, layer, lam_init, page, n_pages):
    n_seq = q_ref.shape[0]
    n_groups = n_pages // PG_PAGES
    n_steps = n_seq * n_groups
    span = PG_PAGES * page
    past = n_pages * page
    rowc = lax.broadcasted_iota(jnp.int32, (N_BR, 1), 0)
    slope = jnp.exp2(-2.0 * ((rowc >> 1) + 1).astype(F32))
    lane = lax.broadcasted_iota(jnp.int32, (N_BR, GROUP_W), 1)
    rown = lax.broadcasted_iota(jnp.int32, (N_BR, GROUP_W), 0)
    base_ref[...] = slope * lax.broadcasted_iota(jnp.int32, (N_BR, span), 1).astype(F32)

    def page_copies(step, slot):
        seq, grp = step // n_groups, step % n_groups
        for r in range(PG_PAGES):
            pid = pt_ref[seq, grp * PG_PAGES + r]
            yield pltpu.make_async_copy(ck_hbm.at[layer, pid], kbuf.at[slot, r], sem.at[slot])
            yield pltpu.make_async_copy(cv_hbm.at[layer, pid], vbuf.at[slot, r], sem.at[slot])

    def start(step, slot):
        for cp in page_copies(step, slot):
            cp.start()

    def wait(step, slot):
        for cp in page_copies(step, slot):
            cp.wait()

    for ahead in range(PG_SLOTS - 1):
        start(ahead, ahead)

    def body(step, carry):
        slot = lax.rem(step, PG_SLOTS)
        nxt = step + (PG_SLOTS - 1)

        @pl.when(nxt < n_steps)
        def _():
            start(nxt, lax.rem(nxt, PG_SLOTS))

        wait(step, slot)
        seq, grp = step // n_groups, step % n_groups

        @pl.when(grp == 0)
        def _():
            q = q_ref[pl.ds(seq, 1), :] * (DA_QK ** -0.5)
            qs_ref[...] = jnp.where((lane >> 5) == rown, jnp.broadcast_to(q, (N_BR, GROUP_W)), 0.0)
            m_ref[...] = jnp.full_like(m_ref, NEG_INF)
            l_ref[...] = jnp.zeros_like(l_ref)
            acc_ref[...] = jnp.zeros_like(acc_ref)

        qs = qs_ref[...]
        qb = qs.astype(BF16)
        s = jnp.concatenate(
            [jnp.dot(qb, kbuf[slot, r].astype(BF16), preferred_element_type=F32) for r in range(PG_PAGES)],
            axis=1)
        off = -slope * jnp.asarray(past - grp * span, F32)
        s = s + base_ref[...]
        m_old = m_ref[...]
        m_new = jnp.maximum(m_old, jnp.max(s, axis=1, keepdims=True) + off)
        alpha = jnp.exp(m_old - m_new)
        p = jnp.exp(s - (m_new - off))
        l_new = alpha * l_ref[...] + jnp.sum(p, axis=1, keepdims=True)
        pb = p.astype(BF16)
        pv = None
        for r in range(PG_PAGES):
            d = _mm_nt(pb[:, r * page:(r + 1) * page], vbuf[slot, r])
            pv = d if pv is None else pv + d
        acc = acc_ref[...] * alpha + pv
        m_ref[...] = m_new
        l_ref[...] = l_new
        acc_ref[...] = acc

        @pl.when(grp == n_groups - 1)
        def _():
            s_self = jnp.sum(qs * kn_ref[pl.ds(seq, 1), :], axis=1, keepdims=True)
            m_fin = jnp.maximum(m_new, s_self)
            a_fin = jnp.exp(m_new - m_fin)
            p_self = jnp.exp(s_self - m_fin)
            l_fin = a_fin * l_new + p_self
            out = (acc * a_fin + p_self * vn_ref[pl.ds(seq, 1), :]) / l_fin
            lam = _diff_lambda(lq1_ref, lk1_ref, lq2_ref, lk2_ref, lam_init)
            coef = jnp.where((rown & 1) == 0, 1.0, -lam)
            o = jnp.sum(jnp.where((lane >> 6) == (rown >> 1), out * coef, 0.0), axis=0, keepdims=True)
            o8 = jnp.broadcast_to(o, (SUBLANES, GROUP_W))
            o_ref[pl.ds(seq, 1), :] = _head_rms(o8, ng_ref[...], 1.0 - lam_init)[0:1]

        return carry

    lax.fori_loop(0, n_steps, body, 0)


def attn_sample(q, k_new, v_new, cache_k, cache_v, page_table, layer, lq1, lk1, lq2, lk2, norm_g, lam_init):
    b = q.shape[0]
    depth, n_pool, page = cache_k.shape[:3]
    n_pages = page_table.shape[1]
    assert n_pages % PG_PAGES == 0 and b * (n_pages // PG_PAGES) >= PG_SLOTS
    as_pages = lambda c: jnp.transpose(c, (0, 1, 3, 4, 2)).reshape(depth, n_pool, GROUP_W, page)
    row = lambda z: z.reshape(1, -1)
    full = lambda shape: pl.BlockSpec(shape, lambda i, pt: (0,) * len(shape))
    hbm = pl.BlockSpec(memory_space=pl.ANY)
    span = PG_PAGES * page
    grid_spec = pltpu.PrefetchScalarGridSpec(
        num_scalar_prefetch=1,
        grid=(1,),
        in_specs=[full((b, GROUP_W)), full((b, GROUP_W)), full((b, GROUP_W)),
                  full((1, DA_QK)), full((1, DA_QK)), full((1, DA_QK)), full((1, DA_QK)), full((1, GROUP_W)),
                  hbm, hbm],
        out_specs=full((b, GROUP_W)),
        scratch_shapes=[pltpu.VMEM((PG_SLOTS, PG_PAGES, GROUP_W, page), F32),
                        pltpu.VMEM((PG_SLOTS, PG_PAGES, GROUP_W, page), F32),
                        pltpu.SemaphoreType.DMA((PG_SLOTS,)),
                        pltpu.VMEM((N_BR, GROUP_W), F32), pltpu.VMEM((N_BR, span), F32),
                        pltpu.VMEM((N_BR, 1), F32), pltpu.VMEM((N_BR, 1), F32), pltpu.VMEM((N_BR, GROUP_W), F32)])
    return pl.pallas_call(
        functools.partial(_paged_body, layer=layer, lam_init=lam_init, page=page, n_pages=n_pages),
        grid_spec=grid_spec,
        out_shape=jax.ShapeDtypeStruct((b, GROUP_W), F32),
        compiler_params=_cparams("arbitrary"),
        name="attn_sample",
    )(page_table, q, k_new, v_new, row(lq1), row(lk1), row(lq2), row(lk2),
      row(jnp.tile(norm_g, N_HEADS)), as_pages(cache_k), as_pages(cache_v))


def _conv_body(p_ref, buf_ref, w_ref, b_ref, lg_ref, lb_ref, y_ref, tail_ref, ext_ref, sh_ref, *, t_valid):
    tb = p_ref.shape[1]
    ti = pl.program_id(1)

    @pl.when(ti == 0)
    def _():
        ext_ref[0:CV_HALO, :] = buf_ref[0]

    x = p_ref[0]
    ext_ref[CV_HALO:CV_HALO + tb, :] = x[:, :GROUP_W] * _sigmoid(x[:, GROUP_W:])
    span = tb + CV_HALO - SUBLANES
    for ph in range(1, SUBLANES):
        sh_ref[ph - 1] = ext_ref[ph:ph + span, :]
    acc = None
    for j in range(CONV_WIDTH):
        lo = CV_HALO - (CONV_WIDTH - 1) + j
        ph, base = lo % SUBLANES, lo - lo % SUBLANES
        rows = ext_ref[base:base + tb, :] if ph == 0 else sh_ref[ph - 1, base:base + tb, :]
        term = rows * w_ref[j:j + 1, :]
        acc = term if acc is None else acc + term
    hist_new = ext_ref[t_valid:t_valid + CV_HALO, :]
    ext_ref[0:CV_HALO, :] = hist_new
    h = acc + b_ref[...]
    hc = h - jnp.mean(h, axis=-1, keepdims=True)
    var = jnp.mean(hc * hc, axis=-1, keepdims=True)
    hn = hc * lax.rsqrt(var + LN_EPS) * lg_ref[...] + lb_ref[...]
    y_ref[0] = hn * _sigmoid(hn)

    @pl.when(ti == pl.num_programs(1) - 1)
    def _():
        tail_ref[0] = hist_new


def conv_mix(p, buf, conv_w, conv_b, ln_g, ln_b, *, t_valid):
    b, t, _ = p.shape
    tb = min(512, t)
    assert t == tb or t_valid == t
    keep = CONV_WIDTH - 1
    hist = jnp.pad(buf, ((0, 0), (CV_HALO - keep, 0), (0, 0)))
    row = lambda z: z.reshape(1, -1)
    const = lambda shape: pl.BlockSpec(shape, lambda i, j: (0,) * len(shape))
    y, tail = pl.pallas_call(
        functools.partial(_conv_body, t_valid=min(t_valid, tb)),
        grid=(b, t // tb),
        in_specs=[pl.BlockSpec((1, tb, CV_COLS), lambda i, j: (i, j, 0)),
                  pl.BlockSpec((1, CV_HALO, GROUP_W), lambda i, j: (i, 0, 0)),
                  const((CONV_WIDTH, GROUP_W)), const((1, GROUP_W)), const((1, GROUP_W)), const((1, GROUP_W))],
        out_specs=[pl.BlockSpec((1, tb, GROUP_W), lambda i, j: (i, j, 0)),
                   pl.BlockSpec((1, CV_HALO, GROUP_W), lambda i, j: (i, 0, 0))],
        out_shape=[jax.ShapeDtypeStruct((b, t, GROUP_W), F32),
                   jax.ShapeDtypeStruct((b, CV_HALO, GROUP_W), F32)],
        scratch_shapes=[pltpu.VMEM((CV_HALO + tb, GROUP_W), F32),
                        pltpu.VMEM((SUBLANES - 1, CV_HALO + tb - SUBLANES, GROUP_W), F32)],
        compiler_params=_cparams("parallel", "arbitrary"),
        name="conv_mix",
    )(p, hist, conv_w, row(conv_b), row(ln_g), row(ln_b))
    return y, tail[:, CV_HALO - keep:]


def trunk_layer(l, x, paged, cv_buf, rw_shift, rw_S, dn_buf, dn_S, P, W):
    b, t, _ = x.shape
    n = b * t
    x2 = x.reshape(n, D_MODEL)
    q, k, v, p_cv, p_rw, p_dn = in_proj(x2, P['g_pre_mix'][l], W['w_in'][l])
    lam_init = 0.8 - 0.6 * math.exp(-0.3 * l)
    lam_args = (P['da_lq1'][l], P['da_lk1'][l], P['da_lq2'][l], P['da_lk2'][l], P['da_norm_g'][l], lam_init)
    if paged is None:
        r3 = lambda z: z.reshape(b, t, GROUP_W)
        o_a = attn_prompt(r3(q), r3(k), r3(v), *lam_args).reshape(n, GROUP_W)
    else:
        assert t == 1
        o_a = attn_sample(q, k, v, *paged, l, *lam_args)
    hs = lambda z: z.reshape(b, t, N_HEADS, HEAD_DIM)
    k, v = hs(k), hs(v)
    t_pad = -(-t // SUBLANES) * SUBLANES
    pad_t = lambda z: z if t_pad == t else jnp.pad(z, ((0, 0), (0, t_pad - t), (0, 0)))
    o_b, cv_new = conv_mix(pad_t(p_cv.reshape(b, t, CV_COLS)), cv_buf, P['cv_w'][l], P['cv_b'][l],
                           P['cv_ln_g'][l], P['cv_ln_b'][l], t_valid=t)
    o_b = o_b[:, :t]
    p_rw = p_rw.reshape(b, t, RW_COLS)
    o_c, rw_S_new = rwkv_mix(
        pad_t(p_rw), rw_shift, rw_S, P['rw_mu'][l], P['rw_w0'][l], P['rw_w_up'][l],
        P['rw_a0'][l], P['rw_a_up'][l], P['rw_g_up'][l], P['rw_k_k'][l], P['rw_k_a'][l],
        P['rw_r_k'][l].reshape(-1), P['rw_gn_g'][l], P['rw_gn_b'][l], t_valid=t)
    o_c = o_c[:, :t]
    rw_shift_new = p_rw[:, -1]
    p_dn = p_dn.reshape(b, t, DN_PAD)
    o_d, dn_S_new = dn_mix(pad_t(p_dn), dn_buf, dn_S, P['dn_conv_w'][l], P['dn_A_log'][l],
                           P['dn_dt_bias'][l], P['dn_norm_g'][l], t_valid=t)
    o_d = o_d[:, :t]
    keep = DN_CONV - 1
    dn_buf_new = (p_dn[:, t - keep:, :DN_QKV] if t >= keep else
                  jnp.concatenate([dn_buf, p_dn[..., :DN_QKV]], axis=1)[:, -keep:])
    x2 = out_proj(x2, o_a, o_b.reshape(n, GROUP_W), o_c.reshape(n, GROUP_W), o_d.reshape(n, GROUP_W),
                  W['w_out'][l], P['g_post_mix'][l])
    x2 = ffn(x2, P['g_pre_ffn'][l], W['ffn_w_gate'][l], W['ffn_w_up'][l], W['ffn_w_down'][l],
             P['g_post_ffn'][l])
    return x2.reshape(b, t, D_MODEL), (k, v, cv_new, rw_shift_new, rw_S_new, dn_buf_new, dn_S_new)


def kernel(x_prompt, x_sample, cache_k, cache_v, page_table, state_conv, state_rw_shift,
           state_rw_wkv, state_dn_conv, state_dn_ssm, g_pre_mix, g_post_mix, g_pre_ffn,
           g_post_ffn, w_in, w_out, da_lq1, da_lk1, da_lq2, da_lk2, da_norm_g, cv_w, cv_b,
           cv_ln_g, cv_ln_b, rw_mu, rw_w0, rw_w_up, rw_a0, rw_a_up, rw_g_up, rw_k_k, rw_k_a,
           rw_r_k, rw_gn_g, rw_gn_b, dn_conv_w, dn_A_log, dn_dt_bias, dn_norm_g,
           ffn_w_gate, ffn_w_up, ffn_w_down):
    P = dict(g_pre_mix=g_pre_mix, g_post_mix=g_post_mix, g_pre_ffn=g_pre_ffn, g_post_ffn=g_post_ffn,
             da_lq1=da_lq1, da_lk1=da_lk1, da_lq2=da_lq2, da_lk2=da_lk2,
             da_norm_g=da_norm_g, cv_w=cv_w, cv_b=cv_b, cv_ln_g=cv_ln_g, cv_ln_b=cv_ln_b,
             rw_mu=rw_mu, rw_w0=rw_w0, rw_w_up=rw_w_up, rw_a0=rw_a0, rw_a_up=rw_a_up,
             rw_g_up=rw_g_up, rw_k_k=rw_k_k, rw_k_a=rw_k_a, rw_r_k=rw_r_k, rw_gn_g=rw_gn_g,
             rw_gn_b=rw_gn_b, dn_conv_w=dn_conv_w, dn_A_log=dn_A_log, dn_dt_bias=dn_dt_bias,
             dn_norm_g=dn_norm_g)
    depth = w_in.shape[0]
    W = dict(w_in=[_prep_w_in(w_in[l]) for l in range(depth)],
             w_out=[w_out[l].astype(BF16) for l in range(depth)],
             ffn_w_gate=[ffn_w_gate[l].astype(BF16) for l in range(depth)],
             ffn_w_up=[ffn_w_up[l].astype(BF16) for l in range(depth)],
             ffn_w_down=[ffn_w_down[l].astype(BF16) for l in range(depth)])
    bp = x_prompt.shape[0]
    dtp = x_prompt.dtype
    yp = x_prompt
    ys = x_sample
    outs_p = []
    outs_s = []
    for l in range(depth):
        yp, st_p = trunk_layer(
            l, yp, None,
            jnp.zeros((bp, CONV_WIDTH - 1, GROUP_W), dtp), jnp.zeros((bp, RW_COLS), dtp),
            jnp.zeros((bp, N_HEADS, HEAD_DIM, HEAD_DIM), jnp.float32),
            jnp.zeros((bp, DN_CONV - 1, DN_QKV), dtp),
            jnp.zeros((bp, N_HEADS, HEAD_DIM, HEAD_DIM), jnp.float32), P, W)
        ys, st_s = trunk_layer(l, ys, (cache_k, cache_v, page_table), state_conv[l], state_rw_shift[l],
                               state_rw_wkv[l], state_dn_conv[l], state_dn_ssm[l], P, W)
        outs_p.append(st_p)
        outs_s.append(st_s)
    stk = lambda outs, i: jnp.stack([o[i] for o in outs])
    return (yp, ys, stk(outs_p, 0), stk(outs_p, 1), stk(outs_s, 0), stk(outs_s, 1),
            stk(outs_p, 2), stk(outs_s, 2), stk(outs_p, 3), stk(outs_s, 3),
            stk(outs_p, 4), stk(outs_s, 4), stk(outs_p, 5), stk(outs_s, 5),
            stk(outs_p, 6), stk(outs_s, 6))
```

```python
import functools
import math

import jax
import jax.numpy as jnp
from jax import lax
from jax.experimental import pallas as pl
from jax.experimental.pallas import tpu as pltpu

D_MODEL = 1024
HEAD_DIM = 64
N_HEADS = 4
GROUP_W = 256
DA_QK = HEAD_DIM // 2
CONV_WIDTH = 31
RW_W_RANK = 64
RW_A_RANK = 64
RW_G_RANK = 128
RW_OFF_W = 3 * GROUP_W
RW_OFF_A = RW_OFF_W + RW_W_RANK
RW_OFF_G = RW_OFF_A + RW_A_RANK
RW_COLS = RW_OFF_G + RW_G_RANK
DN_CONV = 4
DN_CHUNK = 64
DN_QKV = 3 * GROUP_W
DN_COLS = DN_QKV + 2 * N_HEADS + GROUP_W
DA_COLS = 3 * GROUP_W
CV_COLS = 2 * GROUP_W
OFF_CV = DA_COLS
OFF_RW = OFF_CV + CV_COLS
OFF_DN = OFF_RW + RW_COLS
N_IN = OFF_DN + DN_COLS
FFN_HIDDEN = 2816
RMS_EPS = 1e-6
LN_EPS = 1e-5
RW_GN_EPS = 64e-5
NEG_INF = -1e30
LOG2E = math.log2(math.e)

LANES = 128
SUBLANES = 8
DN_PAD = DN_QKV + GROUP_W + LANES
DN_HALO = 8
CV_HALO = 32
N_BR = 2 * N_HEADS
AT_TQ = 512
AT_TK = 256
PG_PAGES = 16
PG_SLOTS = 3
VMEM_LIMIT = 56 * 1024 * 1024

F32 = jnp.float32
BF16 = jnp.bfloat16


def _cparams(*sem):
    return pltpu.CompilerParams(dimension_semantics=sem, vmem_limit_bytes=VMEM_LIMIT)


def _rms(x, g):
    return x * lax.rsqrt(jnp.mean(x * x, axis=-1, keepdims=True) + RMS_EPS) * g


def _row_tile(n, target):
    return target if n % target == 0 else n


def _in_proj_body(x_ref, g_ref, w_ref, q_ref, k_ref, v_ref, cv_ref, rw_ref, dn_ref):
    hn = _rms(x_ref[...], g_ref[...]).astype(BF16)
    off = 0
    for ref in (q_ref, k_ref, v_ref, cv_ref, rw_ref, dn_ref):
        w = ref.shape[-1]
        ref[...] = jnp.dot(hn, w_ref[:, off:off + w], preferred_element_type=F32)
        off += w


def _prep_w_in(w):
    dn = w[:, OFF_DN:]
    dn = jnp.concatenate([dn[:, :DN_QKV], dn[:, DN_QKV + 2 * N_HEADS:], dn[:, DN_QKV:DN_QKV + 2 * N_HEADS],
                          jnp.zeros((w.shape[0], LANES - 2 * N_HEADS), w.dtype)], axis=1)
    return jnp.concatenate([w[:, :OFF_DN], dn], axis=1).astype(BF16)


def in_proj(x, g, w):
    n = x.shape[0]
    tm = _row_tile(n, 512)
    widths = (GROUP_W, GROUP_W, GROUP_W, CV_COLS, RW_COLS, DN_PAD)
    row = lambda wd: pl.BlockSpec((tm, wd), lambda i: (i, 0))
    return pl.pallas_call(
        _in_proj_body,
        grid=(n // tm,),
        in_specs=[row(D_MODEL), pl.BlockSpec((1, D_MODEL), lambda i: (0, 0)),
                  pl.BlockSpec(w.shape, lambda i: (0, 0))],
        out_specs=[row(wd) for wd in widths],
        out_shape=[jax.ShapeDtypeStruct((n, wd), F32) for wd in widths],
        compiler_params=_cparams("parallel"),
        name="in_proj",
    )(x, g.reshape(1, -1), w)


def _out_proj_body(x_ref, a_ref, b_ref, c_ref, d_ref, w_ref, g_ref, o_ref):
    acc = None
    for i, ref in enumerate((a_ref, b_ref, c_ref, d_ref)):
        part = jnp.dot(ref[...].astype(BF16), w_ref[i * GROUP_W:(i + 1) * GROUP_W, :],
                       preferred_element_type=F32)
        acc = part if acc is None else acc + part
    o_ref[...] = x_ref[...] + _rms(acc, g_ref[...])


def out_proj(x, oa, ob, oc, od, w, g):
    n = x.shape[0]
    tm = _row_tile(n, 512)
    row = lambda wd: pl.BlockSpec((tm, wd), lambda i: (i, 0))
    return pl.pallas_call(
        _out_proj_body,
        grid=(n // tm,),
        in_specs=[row(D_MODEL), row(GROUP_W), row(GROUP_W), row(GROUP_W), row(GROUP_W),
                  pl.BlockSpec(w.shape, lambda i: (0, 0)), pl.BlockSpec((1, D_MODEL), lambda i: (0, 0))],
        out_specs=row(D_MODEL),
        out_shape=jax.ShapeDtypeStruct((n, D_MODEL), F32),
        compiler_params=_cparams("parallel"),
        name="out_proj",
    )(x, oa, ob, oc, od, w, g.reshape(1, -1))


def _ffn_body(x_ref, g1_ref, wg_ref, wu_ref, wd_ref, g2_ref, o_ref, hn_ref, acc_ref):
    j = pl.program_id(1)

    @pl.when(j == 0)
    def _():
        hn_ref[...] = _rms(x_ref[...], g1_ref[...]).astype(BF16)
        acc_ref[...] = jnp.zeros_like(acc_ref)

    hn = hn_ref[...]
    gate = jnp.dot(hn, wg_ref[...], preferred_element_type=F32)
    up = jnp.dot(hn, wu_ref[...], preferred_element_type=F32)
    f = (gate * jax.nn.sigmoid(gate) * up).astype(BF16)
    acc_ref[...] += jnp.dot(f, wd_ref[...], preferred_element_type=F32)

    @pl.when(j == pl.num_programs(1) - 1)
    def _():
        o_ref[...] = x_ref[...] + _rms(acc_ref[...], g2_ref[...])


def ffn(x, g1, wg, wu, wd, g2):
    n = x.shape[0]
    tm = _row_tile(n, 512)
    th = FFN_HIDDEN // 2
    return pl.pallas_call(
        _ffn_body,
        grid=(n // tm, FFN_HIDDEN // th),
        in_specs=[pl.BlockSpec((tm, D_MODEL), lambda i, j: (i, 0)),
                  pl.BlockSpec((1, D_MODEL), lambda i, j: (0, 0)),
                  pl.BlockSpec((D_MODEL, th), lambda i, j: (0, j)),
                  pl.BlockSpec((D_MODEL, th), lambda i, j: (0, j)),
                  pl.BlockSpec((th, D_MODEL), lambda i, j: (j, 0)),
                  pl.BlockSpec((1, D_MODEL), lambda i, j: (0, 0))],
        out_specs=pl.BlockSpec((tm, D_MODEL), lambda i, j: (i, 0)),
        out_shape=jax.ShapeDtypeStruct((n, D_MODEL), F32),
        scratch_shapes=[pltpu.VMEM((tm, D_MODEL), BF16), pltpu.VMEM((tm, D_MODEL), F32)],
        compiler_params=_cparams("parallel", "arbitrary"),
        name="ffn",
    )(x, g1.reshape(1, -1), wg, wu, wd, g2.reshape(1, -1))


def _mm(a, b):
    return jnp.dot(a.astype(BF16), b.astype(BF16), preferred_element_type=F32)


def _mm_nt(a, b):
    return lax.dot_general(a.astype(BF16), b.astype(BF16), (((1,), (1,)), ((), ())),
                           preferred_element_type=F32)


def _mm_tn(a, b):
    return lax.dot_general(a.astype(BF16), b.astype(BF16), (((0,), (0,)), ((), ())),
                           preferred_element_type=F32)


def _split3(x):
    hi = x.astype(BF16)
    r1 = x - hi.astype(F32)
    mid = r1.astype(BF16)
    lo = (r1 - mid.astype(F32)).astype(BF16)
    return hi, mid, lo


def _mm_exact_rhs(a01, x):
    a = a01.astype(BF16)
    hi, mid, lo = _split3(x)
    d = lambda y: jnp.dot(a, y, preferred_element_type=F32)
    return d(hi) + (d(mid) + d(lo))


def _mm_exact_lhs(x, a01):
    a = a01.astype(BF16)
    hi, mid, lo = _split3(x)
    d = lambda y: jnp.dot(y, a, preferred_element_type=F32)
    return d(hi) + (d(mid) + d(lo))


def _iota2(n, m):
    return lax.broadcasted_iota(jnp.int32, (n, m), 0), lax.broadcasted_iota(jnp.int32, (n, m), 1)


def _same_block(r, c, blk):
    s = int(math.log2(blk))
    return (r >> s) == (c >> s)


def _unit_lower_inverse(mats, to_bd, chunk):
    r, c = _iota2(chunk, N_HEADS * chunk)
    c = c & (chunk - 1)
    eye = (r == c).astype(F32)
    base = min(8, chunk)
    blk = _same_block(r, c, base)
    pw = [jnp.where(blk, a, 0.0) for a in mats]
    d = [eye + p for p in pw]
    pw_bd = [to_bd(p) for p in pw]
    for _ in range(max(int(math.log2(base)) - 1, 0)):
        pw = [_mm(p, pb) for p, pb in zip(pw, pw_bd)]
        pw_bd = [to_bd(p) for p in pw]
        d = [x + _mm(x, pb) for x, pb in zip(d, pw_bd)]
    size = base * 2
    while size <= chunk:
        ring = _same_block(r, c, size) & ~_same_block(r, c, size // 2)
        t = [_mm(jnp.where(ring, a, 0.0), to_bd(x)) for a, x in zip(mats, d)]
        d = [x + _mm(x, to_bd(y)) for x, y in zip(d, t)]
        size *= 2
    return d


def _softplus(z):
    return jnp.maximum(z, 0.0) + jnp.log(1.0 + jnp.exp(-jnp.abs(z)))


def _sigmoid(z):
    return 1.0 / (1.0 + jnp.exp(-z))


def _block_diag_former(chunk, width):
    hw = N_HEADS * chunk
    lr, lc = _iota2(hw, width)
    own = (lr >> int(math.log2(chunk))) == (lc >> int(math.log2(width // N_HEADS)))

    def to_bd(z):
        return jnp.where(own, jnp.concatenate([z] * N_HEADS, axis=0), 0.0).astype(BF16)

    return to_bd


def _head_sum(x, head_ones):
    return _mm_exact_lhs(x, head_ones)


def _row_to_col(row_vals, r, c):
    n = r.shape[0]
    return jnp.sum(jnp.where(r == c, jnp.broadcast_to(row_vals, (n, n)), 0.0), axis=1, keepdims=True)


def _seqs_per_step(b, t):
    want = 4 if t >= HEAD_DIM else 8
    return want if b % want == 0 else 1


def _rwkv_body(p_ref, shift_ref, s0_ref, mu_ref, w0_ref, wup_ref, a0_ref, aup_ref, gup_ref, kk_ref, ka_ref,
               rk_ref, gng_ref, gnb_ref, y_ref, s_out_ref, st_ref, prev_ref, *, chunk, t_valid):
    nb, tb = p_ref.shape[0], p_ref.shape[1]
    ti = pl.program_id(1)
    gr, gcl = _iota2(GROUP_W, GROUP_W)
    head_mask = _same_block(gr, gcl, HEAD_DIM)
    head_ones = head_mask.astype(F32)
    to_bd = _block_diag_former(chunk, GROUP_W)
    to_bd_t = _block_diag_former(chunk, N_HEADS * chunk)

    @pl.when(ti == 0)
    def _():
        for s in range(nb):
            st_ref[s] = jnp.where(head_mask, jnp.concatenate([s0_ref[s]] * N_HEADS, axis=1), 0.0)
        prev_ref[...] = shift_ref[...]

    row = lax.broadcasted_iota(jnp.int32, (nb * tb, 1), 0)
    tr, tc = _iota2(tb, tb)
    cum = (_same_block(tr, tc, chunk) & (tr >= tc)).astype(F32)
    cr, cc = _iota2(chunk, N_HEADS * chunk)
    cc = cc & (chunk - 1)
    low_incl = cr >= cc
    low_strict = cr > cc
    slices = [slice(c0, c0 + chunk) for c0 in range(0, tb, chunk)]

    x = p_ref[...].reshape(nb * tb, RW_COLS)
    prev = pltpu.roll(x, 1, axis=0)
    for s in range(nb):
        prev = jnp.where(row == s * tb, prev_ref[s], prev)
        prev_ref[s] = x[(s + 1) * tb - 1:(s + 1) * tb, :]
    xm = x + (prev - x) * mu_ref[...]
    r = xm[:, :GROUP_W]
    k = xm[:, GROUP_W:2 * GROUP_W]
    v = xm[:, 2 * GROUP_W:RW_OFF_W]
    wd = xm[:, RW_OFF_W:RW_OFF_A]
    ad = xm[:, RW_OFF_A:RW_OFF_G]
    gd = xm[:, RW_OFF_G:RW_COLS]
    w_log = -_softplus(-(w0_ref[...] + _mm(jnp.tanh(wd), wup_ref[...]))) - 0.5
    logw = -jnp.exp(w_log)
    a = _sigmoid(a0_ref[...] + _mm(ad, aup_ref[...]))
    g = _mm(_sigmoid(gd), gup_ref[...])
    kk = k * kk_ref[...]
    kk = kk * lax.rsqrt(_head_sum(kk * kk, head_ones) + 1e-6)
    k2 = k * (1.0 + (a - 1.0) * ka_ref[...])
    bonus = _head_sum(r * k2 * rk_ref[...], head_ones) * v
    if t_valid < tb:
        ok = (row & (tb - 1)) < t_valid
        logw = jnp.where(ok, logw, 0.0)
        kk = jnp.where(ok, kk, 0.0)
        k2 = jnp.where(ok, k2, 0.0)
        v = jnp.where(ok, v, 0.0)
    alpha = -kk
    beta = kk * a
    seqs = []
    for s in range(nb):
        gcum = _mm_exact_rhs(cum, logw[s * tb:(s + 1) * tb])
        pre = []
        for c0 in range(0, tb, chunk):
            gc = gcum[c0:c0 + chunk]
            sl = slice(s * tb + c0, s * tb + c0 + chunk)
            g_last = gc[chunk - 1:chunk, :]
            e_neg = jnp.exp(-gc)
            e_rem = jnp.exp(g_last - gc)
            r_t = r[sl] * jnp.exp(gc)
            a_t = alpha[sl] * jnp.exp(gc - logw[sl])
            k_bd = to_bd(k2[sl] * e_neg)
            b_bd = to_bd(beta[sl] * e_neg)
            v_bd = to_bd(v[sl])
            ar = jnp.concatenate([a_t, r_t], axis=0)
            sc_b = _mm_nt(ar, b_bd)
            sc_k = _mm_nt(ar, k_bd)
            av = _mm(jnp.concatenate([jnp.where(low_strict, sc_k[:chunk], 0.0),
                                      jnp.where(low_incl, sc_k[chunk:], 0.0)], axis=0), v_bd)
            pre.append(dict(
                r_t=r_t, a_bd=to_bd(a_t),
                a_ab=jnp.where(low_strict, sc_b[:chunk], 0.0), a_ak_v=av[:chunk],
                a_rb=jnp.where(low_incl, sc_b[chunk:], 0.0), a_rk_v=av[chunk:],
                kv=_mm_tn(k2[sl] * e_rem, v[sl]), b_hat=beta[sl] * e_rem,
                p_col=_row_to_col(jnp.exp(g_last), gr, gcl)))
        seqs.append((pre, bonus[s * tb:(s + 1) * tb], g[s * tb:(s + 1) * tb]))
    flat = [c for pre, _, _ in seqs for c in pre]
    tinvs = _unit_lower_inverse([c["a_ab"] for c in flat], to_bd_t, chunk)
    for c, tinv in zip(flat, tinvs):
        c["w"] = _mm(tinv, c["a_bd"])
        c["u_loc"] = _mm(tinv, to_bd(c["a_ak_v"]))
    sts = [st_ref[s] for s in range(nb)]
    for ci, sl in enumerate(slices):
        for s in range(nb):
            c, st = seqs[s][0][ci], sts[s]
            ws = _mm(jnp.concatenate([c["w"], c["r_t"]], axis=0), st)
            u = c["u_loc"] + ws[:chunk]
            y = ws[chunk:] + c["a_rk_v"] + _mm(c["a_rb"], to_bd(u))
            sts[s] = st * c["p_col"] + jnp.where(head_mask, c["kv"] + _mm_tn(c["b_hat"], u), 0.0)
            yc = y - _head_sum(y, head_ones) * (1.0 / HEAD_DIM)
            yn = yc * lax.rsqrt(_head_sum(yc * yc, head_ones) * (1.0 / HEAD_DIM) + RW_GN_EPS)
            y_ref[s, sl, :] = yn
    for s in range(nb):
        st_ref[s] = sts[s]
        y_ref[s] = (y_ref[s] * gng_ref[...] + gnb_ref[...] + seqs[s][1]) * seqs[s][2]

    @pl.when(ti == pl.num_programs(1) - 1)
    def _():
        for s in range(nb):
            s_out_ref[s] = jnp.concatenate(
                [sts[s][h * HEAD_DIM:(h + 1) * HEAD_DIM, h * HEAD_DIM:(h + 1) * HEAD_DIM] for h in range(N_HEADS)],
                axis=0)


def rwkv_mix(p, shift_prev, s0, mu, w0, w_up, a0, a_up, g_up, k_k, k_a, r_k, gn_g, gn_b, *, t_valid):
    b, t, _ = p.shape
    chunk = min(HEAD_DIM, t)
    tb = min(256, t)
    nb = _seqs_per_step(b, t)
    row = lambda z: z.reshape(1, -1)
    st0 = jnp.swapaxes(s0, 2, 3).reshape(b, GROUP_W, HEAD_DIM)
    const = lambda shape: pl.BlockSpec(shape, lambda i, j: (0,) * len(shape))
    y, st = pl.pallas_call(
        functools.partial(_rwkv_body, chunk=chunk, t_valid=t_valid),
        grid=(b // nb, t // tb),
        in_specs=[pl.BlockSpec((nb, tb, RW_COLS), lambda i, j: (i, j, 0)),
                  pl.BlockSpec((nb, 1, RW_COLS), lambda i, j: (i, 0, 0)),
                  pl.BlockSpec((nb, GROUP_W, HEAD_DIM), lambda i, j: (i, 0, 0)),
                  const((1, RW_COLS)), const((1, GROUP_W)), const((RW_W_RANK, GROUP_W)),
                  const((1, GROUP_W)), const((RW_A_RANK, GROUP_W)), const((RW_G_RANK, GROUP_W)),
                  const((1, GROUP_W)), const((1, GROUP_W)), const((1, GROUP_W)),
                  const((1, GROUP_W)), const((1, GROUP_W))],
        out_specs=[pl.BlockSpec((nb, tb, GROUP_W), lambda i, j: (i, j, 0)),
                   pl.BlockSpec((nb, GROUP_W, HEAD_DIM), lambda i, j: (i, 0, 0))],
        out_shape=[jax.ShapeDtypeStruct((b, t, GROUP_W), F32),
                   jax.ShapeDtypeStruct((b, GROUP_W, HEAD_DIM), F32)],
        scratch_shapes=[pltpu.VMEM((nb, GROUP_W, GROUP_W), F32), pltpu.VMEM((nb, 1, RW_COLS), F32)],
        compiler_params=_cparams("parallel", "arbitrary"),
        name="rwkv_mix",
    )(p, shift_prev.reshape(b, 1, RW_COLS), st0, row(mu), row(w0), w_up, row(a0), a_up, g_up,
      row(k_k), row(k_a), row(r_k), row(gn_g), row(gn_b))
    return y, jnp.swapaxes(st.reshape(b, N_HEADS, HEAD_DIM, HEAD_DIM), 2, 3)


def _dn_body(p_ref, cprev_ref, s0_ref, cw_ref, alog_ref, dtb_ref, ng_ref, y_ref, s_out_ref, st_ref, hist_ref,
             *, chunk, t_valid):
    nb, tb = p_ref.shape[0], p_ref.shape[1]
    ti = pl.program_id(1)
    gr, gcl = _iota2(GROUP_W, GROUP_W)
    head_mask = _same_block(gr, gcl, HEAD_DIM)
    head_ones = head_mask.astype(F32)
    to_bd = _block_diag_former(chunk, GROUP_W)
    to_bd_t = _block_diag_former(chunk, N_HEADS * chunk)

    @pl.when(ti == 0)
    def _():
        for s in range(nb):
            st_ref[s] = jnp.where(head_mask, jnp.concatenate([s0_ref[s]] * N_HEADS, axis=1), 0.0)
        hist_ref[...] = cprev_ref[...]

    er, ec = _iota2(LANES, GROUP_W)
    lanes_a = (er == (ec >> 6)).astype(F32)
    lanes_b = (er == (ec >> 6) + N_HEADS).astype(F32)
    tr, tc = _iota2(tb, tb)
    cum = (_same_block(tr, tc, chunk) & (tr >= tc)).astype(F32)
    cr, cc = _iota2(chunk, N_HEADS * chunk)
    cc = cc & (chunk - 1)
    low_incl = cr >= cc
    low_strict = cr > cc
    eye_c = cr == cc
    slices = [slice(c0, c0 + chunk) for c0 in range(0, tb, chunk)]

    convs = []
    for s in range(nb):
        ext = jnp.concatenate([hist_ref[s], p_ref[s, :, :DN_QKV]], axis=0)
        hist_ref[s] = ext[tb:tb + DN_HALO, :]
        conv = None
        for j in range(DN_CONV):
            lo = DN_HALO - (DN_CONV - 1) + j
            term = ext[lo:lo + tb, :] * cw_ref[j:j + 1, :]
            conv = term if conv is None else conv + term
        convs.append(conv)
    conv = jnp.concatenate(convs, axis=0)
    x = p_ref[...].reshape(nb * tb, DN_PAD)
    qkv = conv * _sigmoid(conv)
    q = qkv[:, :GROUP_W]
    k = qkv[:, GROUP_W:2 * GROUP_W]
    v = qkv[:, 2 * GROUP_W:]
    q = q * lax.rsqrt(_head_sum(q * q, head_ones) + 1e-6) * (HEAD_DIM ** -0.5)
    k = k * lax.rsqrt(_head_sum(k * k, head_ones) + 1e-6)
    z = x[:, DN_QKV:DN_QKV + GROUP_W]
    ab = x[:, DN_QKV + GROUP_W:]
    g_w = _mm_exact_lhs(-jnp.exp(alog_ref[...]) * _softplus(ab + dtb_ref[...]), lanes_a)
    beta = _mm_exact_lhs(_sigmoid(ab), lanes_b)
    if t_valid < tb:
        ok = (lax.broadcasted_iota(jnp.int32, (nb * tb, 1), 0) & (tb - 1)) < t_valid
        g_w = jnp.where(ok, g_w, 0.0)
        beta = jnp.where(ok, beta, 0.0)
        k = jnp.where(ok, k, 0.0)
        v = jnp.where(ok, v, 0.0)
    kb = k * beta
    vb = v * beta
    seqs = []
    for s in range(nb):
        gcum = _mm_exact_rhs(cum, g_w[s * tb:(s + 1) * tb])
        pre = []
        for c0 in range(0, tb, chunk):
            gc = gcum[c0:c0 + chunk]
            sl = slice(s * tb + c0, s * tb + c0 + chunk)
            g_last = gc[chunk - 1:chunk, :]
            gi = gc if chunk == HEAD_DIM else jnp.concatenate(
                [gc[:, h * HEAD_DIM:h * HEAD_DIM + chunk] for h in range(N_HEADS)], axis=1)
            gj = jnp.sum(jnp.where(eye_c, gi, 0.0), axis=0, keepdims=True)
            decay = jnp.exp(jnp.where(low_incl, gi - gj, NEG_INF))
            k_bd = to_bd(k[sl])
            sc = _mm_nt(jnp.concatenate([kb[sl], q[sl]], axis=0), k_bd)
            pre.append(dict(
                m=jnp.where(low_strict, sc[:chunk] * decay, 0.0),
                qk=jnp.where(low_incl, sc[chunk:] * decay, 0.0),
                kbg_bd=to_bd(kb[sl] * jnp.exp(gc)), q_g=q[sl] * jnp.exp(gc), vb_bd=to_bd(vb[sl]),
                k_rem=k[sl] * jnp.exp(g_last - gc), p_col=_row_to_col(jnp.exp(g_last), gr, gcl)))
        seqs.append((pre, z[s * tb:(s + 1) * tb]))
    flat = [c for pre, _ in seqs for c in pre]
    tinvs = _unit_lower_inverse([-c["m"] for c in flat], to_bd_t, chunk)
    for c, tinv in zip(flat, tinvs):
        c["w"] = _mm(tinv, c["kbg_bd"])
        c["u_loc"] = _mm(tinv, c["vb_bd"])
    sts = [st_ref[s] for s in range(nb)]
    for ci, sl in enumerate(slices):
        for s in range(nb):
            c, st = seqs[s][0][ci], sts[s]
            ws = _mm(jnp.concatenate([c["w"], c["q_g"]], axis=0), st)
            v_new = c["u_loc"] - ws[:chunk]
            o = ws[chunk:] + _mm(c["qk"], to_bd(v_new))
            sts[s] = st * c["p_col"] + jnp.where(head_mask, _mm_tn(c["k_rem"], v_new), 0.0)
            y_ref[s, sl, :] = o * lax.rsqrt(_head_sum(o * o, head_ones) * (1.0 / HEAD_DIM) + RMS_EPS) * ng_ref[...]
    for s in range(nb):
        st_ref[s] = sts[s]
        z = seqs[s][1]
        y_ref[s] = y_ref[s] * (z * _sigmoid(z))

    @pl.when(ti == pl.num_programs(1) - 1)
    def _():
        for s in range(nb):
            s_out_ref[s] = jnp.concatenate(
                [sts[s][h * HEAD_DIM:(h + 1) * HEAD_DIM, h * HEAD_DIM:(h + 1) * HEAD_DIM] for h in range(N_HEADS)],
                axis=0)


def dn_mix(p, conv_prev, s0, conv_w, a_log, dt_bias, norm_g, *, t_valid):
    b, t, _ = p.shape
    chunk = min(DN_CHUNK, t)
    tb = min(256, t)
    nb = _seqs_per_step(b, t)
    hist = jnp.pad(conv_prev, ((0, 0), (DN_HALO - (DN_CONV - 1), 0), (0, 0)))
    lane_pad = lambda z: jnp.pad(z, (0, LANES - z.shape[0])).reshape(1, LANES)
    const = lambda shape: pl.BlockSpec(shape, lambda i, j: (0,) * len(shape))
    y, st = pl.pallas_call(
        functools.partial(_dn_body, chunk=chunk, t_valid=t_valid),
        grid=(b // nb, t // tb),
        in_specs=[pl.BlockSpec((nb, tb, DN_PAD), lambda i, j: (i, j, 0)),
                  pl.BlockSpec((nb, DN_HALO, DN_QKV), lambda i, j: (i, 0, 0)),
                  pl.BlockSpec((nb, GROUP_W, HEAD_DIM), lambda i, j: (i, 0, 0)),
                  const((DN_CONV, DN_QKV)), const((1, LANES)), const((1, LANES)), const((1, GROUP_W))],
        out_specs=[pl.BlockSpec((nb, tb, GROUP_W), lambda i, j: (i, j, 0)),
                   pl.BlockSpec((nb, GROUP_W, HEAD_DIM), lambda i, j: (i, 0, 0))],
        out_shape=[jax.ShapeDtypeStruct((b, t, GROUP_W), F32),
                   jax.ShapeDtypeStruct((b, GROUP_W, HEAD_DIM), F32)],
        scratch_shapes=[pltpu.VMEM((nb, GROUP_W, GROUP_W), F32), pltpu.VMEM((nb, DN_HALO, DN_QKV), F32)],
        compiler_params=_cparams("parallel", "arbitrary"),
        name="dn_mix",
    )(p, hist, s0.reshape(b, GROUP_W, HEAD_DIM), conv_w, lane_pad(a_log), lane_pad(dt_bias),
      jnp.tile(norm_g, N_HEADS).reshape(1, GROUP_W))
    return y, st.reshape(b, N_HEADS, HEAD_DIM, HEAD_DIM)


def _diff_lambda(lq1_ref, lk1_ref, lq2_ref, lk2_ref, lam_init):
    return (jnp.exp(jnp.sum(lq1_ref[...] * lk1_ref[...], axis=-1, keepdims=True))
            - jnp.exp(jnp.sum(lq2_ref[...] * lk2_ref[...], axis=-1, keepdims=True)) + lam_init)


def _head_rms(o, g, scale):
    gr, gcl = _iota2(GROUP_W, GROUP_W)
    head_ones = _same_block(gr, gcl, HEAD_DIM).astype(F32)
    ms = _mm_exact_lhs(o * o, head_ones) * (1.0 / HEAD_DIM)
    return o * lax.rsqrt(ms + RMS_EPS) * g * scale


def _attn_body(q_ref, k_ref, v_ref, lq1_ref, lk1_ref, lq2_ref, lk2_ref, ng_ref, o_ref,
               kb_ref, vt_ref, bias_ref, biasd_ref, qs_ref, *stats, lam_init):
    m_refs, acc_refs = stats[:N_BR], stats[N_BR:]
    b = pl.program_id(0)
    i = pl.program_id(1)
    tq, tk = AT_TQ, AT_TK
    t = k_ref.shape[1]
    n_sub = tq // tk
    slope = [LOG2E * 2.0 ** (-2 * (h + 1)) for h in range(N_HEADS)]

    @pl.when((b == 0) & (i == 0))
    def _():
        jj, ii = _iota2(tk, tq)
        rel = (ii - jj).astype(F32)
        for h in range(N_HEADS):
            bias_ref[:, h * tq:(h + 1) * tq] = -slope[h] * rel
            for d in range(n_sub):
                rd = rel - float(d * tk)
                biasd_ref[d, :, h * tq:(h + 1) * tq] = jnp.where(rd >= 0, -slope[h] * rel, NEG_INF)

    @pl.when(i == 0)
    def _():
        kb_ref[...] = k_ref[0].astype(BF16)
        for j in range(t // tk):
            vt_ref[:, j * tk:(j + 1) * tk] = v_ref[0, j * tk:(j + 1) * tk, :].T.astype(BF16)

    q = q_ref[0] * (DA_QK ** -0.5 * LOG2E)
    lane = lax.broadcasted_iota(jnp.int32, (1, GROUP_W), 1)
    for c in range(N_BR):
        own = (lane >> 5) == c
        qs_ref[c * tq:(c + 1) * tq, :] = jnp.where(own, q, 0.0).astype(BF16)
    for c in range(N_BR):
        m_refs[c][...] = jnp.full_like(m_refs[c], NEG_INF)
        acc_refs[c][...] = jnp.zeros_like(acc_refs[c])

    def tile(j, b_ref):
        start = pl.multiple_of(j * tk, tk)
        kt = kb_ref[pl.ds(start, tk), :]
        dist = (tq * i - tk * j).astype(F32)
        ss = [_mm_nt(kt, qs_ref[c * tq:(c + 1) * tq, :]) for c in range(N_BR)]
        ps, alphas = [], []
        for c in range(N_BR):
            h = c >> 1
            s = ss[c] + b_ref[:, h * tq:(h + 1) * tq]
            ct = -slope[h] * dist
            m_old = m_refs[c][...]
            m_new = jnp.maximum(m_old, jnp.max(s, axis=0, keepdims=True) + ct)
            alpha = jnp.exp2(m_old - m_new)
            p = jnp.exp2(s - (m_new - ct))
            m_refs[c][...] = m_new
            ps.append(p.astype(BF16))
            alphas.append(alpha)
        ones = jnp.ones((2 * SUBLANES, tk), BF16)
        for c in range(N_BR):
            h = c >> 1
            lhs = jnp.concatenate([vt_ref[h * HEAD_DIM:(h + 1) * HEAD_DIM, pl.ds(start, tk)], ones], axis=0)
            pv = jnp.dot(lhs, ps[c], preferred_element_type=F32)
            acc_refs[c][...] = acc_refs[c][...] * alphas[c] + pv

    def body(j, carry):
        tile(j, bias_ref)
        return carry

    lax.fori_loop(0, i * n_sub, body, 0)
    for d in range(n_sub):
        tile(i * n_sub + d, biasd_ref.at[d])

    lam = _diff_lambda(lq1_ref, lk1_ref, lq2_ref, lk2_ref, lam_init)
    norm = [acc_refs[c][:HEAD_DIM, :] * (1.0 / acc_refs[c][HEAD_DIM:HEAD_DIM + 1, :]) for c in range(N_BR)]
    o = jnp.concatenate([norm[2 * h] - lam * norm[2 * h + 1] for h in range(N_HEADS)], axis=0).T
    o_ref[0] = _head_rms(o, ng_ref[...], 1.0 - lam_init)


def attn_prompt(q, k, v, lq1, lk1, lq2, lk2, norm_g, lam_init):
    b, t, _ = q.shape
    tq, tk = AT_TQ, AT_TK
    assert t % tq == 0 and tq % tk == 0
    row = lambda z: z.reshape(1, -1)
    const = lambda shape: pl.BlockSpec(shape, lambda i, j: (0,) * len(shape))
    return pl.pallas_call(
        functools.partial(_attn_body, lam_init=lam_init),
        grid=(b, t // tq),
        in_specs=[pl.BlockSpec((1, tq, GROUP_W), lambda i, j: (i, j, 0)),
                  pl.BlockSpec((1, t, GROUP_W), lambda i, j: (i, 0, 0)),
                  pl.BlockSpec((1, t, GROUP_W), lambda i, j: (i, 0, 0)),
                  const((1, DA_QK)), const((1, DA_QK)), const((1, DA_QK)), const((1, DA_QK)),
                  const((1, GROUP_W))],
        out_specs=pl.BlockSpec((1, tq, GROUP_W), lambda i, j: (i, j, 0)),
        out_shape=jax.ShapeDtypeStruct((b, t, GROUP_W), F32),
        scratch_shapes=[pltpu.VMEM((t, GROUP_W), BF16), pltpu.VMEM((GROUP_W, t), BF16),
                        pltpu.VMEM((tk, N_HEADS * tq), F32), pltpu.VMEM((tq // tk, tk, N_HEADS * tq), F32),
                        pltpu.VMEM((N_BR * tq, GROUP_W), BF16),
                        *([pltpu.VMEM((1, tq), F32)] * N_BR),
                        *([pltpu.VMEM((HEAD_DIM + 2 * SUBLANES, tq), F32)] * N_BR)],
        compiler_params=_cparams("arbitrary", "arbitrary"),
        name="attn_prompt",
    )(q, k, v, row(lq1), row(lk1), row(lq2), row(lk2), row(jnp.tile(norm_g, N_HEADS)))


def _paged_body(pt_ref, q_ref, kn_ref, vn_ref, lq1_ref, lk1_ref, lq2_ref, lk2_ref, ng_ref, ck_hbm, cv_hbm, o_ref,
                kbuf, vbuf, sem, qs_ref, base_ref, m_ref, l_ref, acc_ref, *, layer, lam_init, page, n_pages):
    n_seq = q_ref.shape[0]
    n_groups = n_pages // PG_PAGES
    n_steps = n_seq * n_groups
    span = PG_PAGES * page
    past = n_pages * page
    rowc = lax.broadcasted_iota(jnp.int32, (N_BR, 1), 0)
    slope = jnp.exp2(-2.0 * ((rowc >> 1) + 1).astype(F32))
    lane = lax.broadcasted_iota(jnp.int32, (N_BR, GROUP_W), 1)
    rown = lax.broadcasted_iota(jnp.int32, (N_BR, GROUP_W), 0)
    base_ref[...] = slope * lax.broadcasted_iota(jnp.int32, (N_BR, span), 1).astype(F32)

    def page_copies(step, slot):
        seq, grp = step // n_groups, step % n_groups
        for r in range(PG_PAGES):
            pid = pt_ref[seq, grp * PG_PAGES + r]
            yield pltpu.make_async_copy(ck_hbm.at[layer, pid], kbuf.at[slot, r], sem.at[slot])
            yield pltpu.make_async_copy(cv_hbm.at[layer, pid], vbuf.at[slot, r], sem.at[slot])

    def start(step, slot):
        for cp in page_copies(step, slot):
            cp.start()

    def wait(step, slot):
        for cp in page_copies(step, slot):
            cp.wait()

    for ahead in range(PG_SLOTS - 1):
        start(ahead, ahead)

    def body(step, carry):
        slot = lax.rem(step, PG_SLOTS)
        nxt = step + (PG_SLOTS - 1)

        @pl.when(nxt < n_steps)
        def _():
            start(nxt, lax.rem(nxt, PG_SLOTS))

        wait(step, slot)
        seq, grp = step // n_groups, step % n_groups

        @pl.when(grp == 0)
        def _():
            q = q_ref[pl.ds(seq, 1), :] * (DA_QK ** -0.5)
            qs_ref[...] = jnp.where((lane >> 5) == rown, jnp.broadcast_to(q, (N_BR, GROUP_W)), 0.0)
            m_ref[...] = jnp.full_like(m_ref, NEG_INF)
            l_ref[...] = jnp.zeros_like(l_ref)
            acc_ref[...] = jnp.zeros_like(acc_ref)

        qs = qs_ref[...]
        qb = qs.astype(BF16)
        s = jnp.concatenate(
            [jnp.dot(qb, kbuf[slot, r].astype(BF16), preferred_element_type=F32) for r in range(PG_PAGES)],
            axis=1)
        off = -slope * jnp.asarray(past - grp * span, F32)
        s = s + base_ref[...]
        m_old = m_ref[...]
        m_new = jnp.maximum(m_old, jnp.max(s, axis=1, keepdims=True) + off)
        alpha = jnp.exp(m_old - m_new)
        p = jnp.exp(s - (m_new - off))
        l_new = alpha * l_ref[...] + jnp.sum(p, axis=1, keepdims=True)
        pb = p.astype(BF16)
        pv = None
        for r in range(PG_PAGES):
            d = _mm_nt(pb[:, r * page:(r + 1) * page], vbuf[slot, r])
            pv = d if pv is None else pv + d
        acc = acc_ref[...] * alpha + pv
        m_ref[...] = m_new
        l_ref[...] = l_new
        acc_ref[...] = acc

        @pl.when(grp == n_groups - 1)
        def _():
            s_self = jnp.sum(qs * kn_ref[pl.ds(seq, 1), :], axis=1, keepdims=True)
            m_fin = jnp.maximum(m_new, s_self)
            a_fin = jnp.exp(m_new - m_fin)
            p_self = jnp.exp(s_self - m_fin)
            l_fin = a_fin * l_new + p_self
            out = (acc * a_fin + p_self * vn_ref[pl.ds(seq, 1), :]) / l_fin
            lam = _diff_lambda(lq1_ref, lk1_ref, lq2_ref, lk2_ref, lam_init)
            coef = jnp.where((rown & 1) == 0, 1.0, -lam)
            o = jnp.sum(jnp.where((lane >> 6) == (rown >> 1), out * coef, 0.0), axis=0, keepdims=True)
            o8 = jnp.broadcast_to(o, (SUBLANES, GROUP_W))
            o_ref[pl.ds(seq, 1), :] = _head_rms(o8, ng_ref[...], 1.0 - lam_init)[0:1]

        return carry

    lax.fori_loop(0, n_steps, body, 0)


def attn_sample(q, k_new, v_new, cache_k, cache_v, page_table, layer, lq1, lk1, lq2, lk2, norm_g, lam_init):
    b = q.shape[0]
    depth, n_pool, page = cache_k.shape[:3]
    n_pages = page_table.shape[1]
    assert n_pages % PG_PAGES == 0 and b * (n_pages // PG_PAGES) >= PG_SLOTS
    as_pages = lambda c: jnp.transpose(c, (0, 1, 3, 4, 2)).reshape(depth, n_pool, GROUP_W, page)
    row = lambda z: z.reshape(1, -1)
    full = lambda shape: pl.BlockSpec(shape, lambda i, pt: (0,) * len(shape))
    hbm = pl.BlockSpec(memory_space=pl.ANY)
    span = PG_PAGES * page
    grid_spec = pltpu.PrefetchScalarGridSpec(
        num_scalar_prefetch=1,
        grid=(1,),
        in_specs=[full((b, GROUP_W)), full((b, GROUP_W)), full((b, GROUP_W)),
                  full((1, DA_QK)), full((1, DA_QK)), full((1, DA_QK)), full((1, DA_QK)), full((1, GROUP_W)),
                  hbm, hbm],
        out_specs=full((b, GROUP_W)),
        scratch_shapes=[pltpu.VMEM((PG_SLOTS, PG_PAGES, GROUP_W, page), F32),
                        pltpu.VMEM((PG_SLOTS, PG_PAGES, GROUP_W, page), F32),
                        pltpu.SemaphoreType.DMA((PG_SLOTS,)),
                        pltpu.VMEM((N_BR, GROUP_W), F32), pltpu.VMEM((N_BR, span), F32),
                        pltpu.VMEM((N_BR, 1), F32), pltpu.VMEM((N_BR, 1), F32), pltpu.VMEM((N_BR, GROUP_W), F32)])
    return pl.pallas_call(
        functools.partial(_paged_body, layer=layer, lam_init=lam_init, page=page, n_pages=n_pages),
        grid_spec=grid_spec,
        out_shape=jax.ShapeDtypeStruct((b, GROUP_W), F32),
        compiler_params=_cparams("arbitrary"),
        name="attn_sample",
    )(page_table, q, k_new, v_new, row(lq1), row(lk1), row(lq2), row(lk2),
      row(jnp.tile(norm_g, N_HEADS)), as_pages(cache_k), as_pages(cache_v))


def _conv_body(p_ref, buf_ref, w_ref, b_ref, lg_ref, lb_ref, y_ref, tail_ref, ext_ref, sh_ref, *, t_valid):
    tb = p_ref.shape[1]
    ti = pl.program_id(1)

    @pl.when(ti == 0)
    def _():
        ext_ref[0:CV_HALO, :] = buf_ref[0]

    x = p_ref[0]
    ext_ref[CV_HALO:CV_HALO + tb, :] = x[:, :GROUP_W] * _sigmoid(x[:, GROUP_W:])
    span = tb + CV_HALO - SUBLANES
    for ph in range(1, SUBLANES):
        sh_ref[ph - 1] = ext_ref[ph:ph + span, :]
    acc = None
    for j in range(CONV_WIDTH):
        lo = CV_HALO - (CONV_WIDTH - 1) + j
        ph, base = lo % SUBLANES, lo - lo % SUBLANES
        rows = ext_ref[base:base + tb, :] if ph == 0 else sh_ref[ph - 1, base:base + tb, :]
        term = rows * w_ref[j:j + 1, :]
        acc = term if acc is None else acc + term
    hist_new = ext_ref[t_valid:t_valid + CV_HALO, :]
    ext_ref[0:CV_HALO, :] = hist_new
    h = acc + b_ref[...]
    hc = h - jnp.mean(h, axis=-1, keepdims=True)
    var = jnp.mean(hc * hc, axis=-1, keepdims=True)
    hn = hc * lax.rsqrt(var + LN_EPS) * lg_ref[...] + lb_ref[...]
    y_ref[0] = hn * _sigmoid(hn)

    @pl.when(ti == pl.num_programs(1) - 1)
    def _():
        tail_ref[0] = hist_new


def conv_mix(p, buf, conv_w, conv_b, ln_g, ln_b, *, t_valid):
    b, t, _ = p.shape
    tb = min(512, t)
    assert t == tb or t_valid == t
    keep = CONV_WIDTH - 1
    hist = jnp.pad(buf, ((0, 0), (CV_HALO - keep, 0), (0, 0)))
    row = lambda z: z.reshape(1, -1)
    const = lambda shape: pl.BlockSpec(shape, lambda i, j: (0,) * len(shape))
    y, tail = pl.pallas_call(
        functools.partial(_conv_body, t_valid=min(t_valid, tb)),
        grid=(b, t // tb),
        in_specs=[pl.BlockSpec((1, tb, CV_COLS), lambda i, j: (i, j, 0)),
                  pl.BlockSpec((1, CV_HALO, GROUP_W), lambda i, j: (i, 0, 0)),
                  const((CONV_WIDTH, GROUP_W)), const((1, GROUP_W)), const((1, GROUP_W)), const((1, GROUP_W))],
        out_specs=[pl.BlockSpec((1, tb, GROUP_W), lambda i, j: (i, j, 0)),
                   pl.BlockSpec((1, CV_HALO, GROUP_W), lambda i, j: (i, 0, 0))],
        out_shape=[jax.ShapeDtypeStruct((b, t, GROUP_W), F32),
                   jax.ShapeDtypeStruct((b, CV_HALO, GROUP_W), F32)],
        scratch_shapes=[pltpu.VMEM((CV_HALO + tb, GROUP_W), F32),
                        pltpu.VMEM((SUBLANES - 1, CV_HALO + tb - SUBLANES, GROUP_W), F32)],
        compiler_params=_cparams("parallel", "arbitrary"),
        name="conv_mix",
    )(p, hist, conv_w, row(conv_b), row(ln_g), row(ln_b))
    return y, tail[:, CV_HALO - keep:]


def trunk_layer(l, x, paged, cv_buf, rw_shift, rw_S, dn_buf, dn_S, P, W):
    b, t, _ = x.shape
    n = b * t
    x2 = x.reshape(n, D_MODEL)
    q, k, v, p_cv, p_rw, p_dn = in_proj(x2, P['g_pre_mix'][l], W['w_in'][l])
    lam_init = 0.8 - 0.6 * math.exp(-0.3 * l)
    lam_args = (P['da_lq1'][l], P['da_lk1'][l], P['da_lq2'][l], P['da_lk2'][l], P['da_norm_g'][l], lam_init)
    if paged is None:
        r3 = lambda z: z.reshape(b, t, GROUP_W)
        o_a = attn_prompt(r3(q), r3(k), r3(v), *lam_args).reshape(n, GROUP_W)
    else:
        assert t == 1
        o_a = attn_sample(q, k, v, *paged, l, *lam_args)
    hs = lambda z: z.reshape(b, t, N_HEADS, HEAD_DIM)
    k, v = hs(k), hs(v)
    t_pad = -(-t // SUBLANES) * SUBLANES
    pad_t = lambda z: z if t_pad == t else jnp.pad(z, ((0, 0), (0, t_pad - t), (0, 0)))
    o_b, cv_new = conv_mix(pad_t(p_cv.reshape(b, t, CV_COLS)), cv_buf, P['cv_w'][l], P['cv_b'][l],
                           P['cv_ln_g'][l], P['cv_ln_b'][l], t_valid=t)
    o_b = o_b[:, :t]
    p_rw = p_rw.reshape(b, t, RW_COLS)
    o_c, rw_S_new = rwkv_mix(
        pad_t(p_rw), rw_shift, rw_S, P['rw_mu'][l], P['rw_w0'][l], P['rw_w_up'][l],
        P['rw_a0'][l], P['rw_a_up'][l], P['rw_g_up'][l], P['rw_k_k'][l], P['rw_k_a'][l],
        P['rw_r_k'][l].reshape(-1), P['rw_gn_g'][l], P['rw_gn_b'][l], t_valid=t)
    o_c = o_c[:, :t]
    rw_shift_new = p_rw[:, -1]
    p_dn = p_dn.reshape(b, t, DN_PAD)
    o_d, dn_S_new = dn_mix(pad_t(p_dn), dn_buf, dn_S, P['dn_conv_w'][l], P['dn_A_log'][l],
                           P['dn_dt_bias'][l], P['dn_norm_g'][l], t_valid=t)
    o_d = o_d[:, :t]
    keep = DN_CONV - 1
    dn_buf_new = (p_dn[:, t - keep:, :DN_QKV] if t >= keep else
                  jnp.concatenate([dn_buf, p_dn[..., :DN_QKV]], axis=1)[:, -keep:])
    x2 = out_proj(x2, o_a, o_b.reshape(n, GROUP_W), o_c.reshape(n, GROUP_W), o_d.reshape(n, GROUP_W),
                  W['w_out'][l], P['g_post_mix'][l])
    x2 = ffn(x2, P['g_pre_ffn'][l], W['ffn_w_gate'][l], W['ffn_w_up'][l], W['ffn_w_down'][l],
             P['g_post_ffn'][l])
    return x2.reshape(b, t, D_MODEL), (k, v, cv_new, rw_shift_new, rw_S_new, dn_buf_new, dn_S_new)


def kernel(x_prompt, x_sample, cache_k, cache_v, page_table, state_conv, state_rw_shift,
           state_rw_wkv, state_dn_conv, state_dn_ssm, g_pre_mix, g_post_mix, g_pre_ffn,
           g_post_ffn, w_in, w_out, da_lq1, da_lk1, da_lq2, da_lk2, da_norm_g, cv_w, cv_b,
           cv_ln_g, cv_ln_b, rw_mu, rw_w0, rw_w_up, rw_a0, rw_a_up, rw_g_up, rw_k_k, rw_k_a,
           rw_r_k, rw_gn_g, rw_gn_b, dn_conv_w, dn_A_log, dn_dt_bias, dn_norm_g,
           ffn_w_gate, ffn_w_up, ffn_w_down):
    P = dict(g_pre_mix=g_pre_mix, g_post_mix=g_post_mix, g_pre_ffn=g_pre_ffn, g_post_ffn=g_post_ffn,
             da_lq1=da_lq1, da_lk1=da_lk1, da_lq2=da_lq2, da_lk2=da_lk2,
             da_norm_g=da_norm_g, cv_w=cv_w, cv_b=cv_b, cv_ln_g=cv_ln_g, cv_ln_b=cv_ln_b,
             rw_mu=rw_mu, rw_w0=rw_w0, rw_w_up=rw_w_up, rw_a0=rw_a0, rw_a_up=rw_a_up,
             rw_g_up=rw_g_up, rw_k_k=rw_k_k, rw_k_a=rw_k_a, rw_r_k=rw_r_k, rw_gn_g=rw_gn_g,
             rw_gn_b=rw_gn_b, dn_conv_w=dn_conv_w, dn_A_log=dn_A_log, dn_dt_bias=dn_dt_bias,
             dn_norm_g=dn_norm_g)
    depth = w_in.shape[0]
    W = dict(w_in=[_prep_w_in(w_in[l]) for l in range(depth)],
             w_out=[w_out[l].astype(BF16) for l in range(depth)],
             ffn_w_gate=[ffn_w_gate[l].astype(BF16) for l in range(depth)],
             ffn_w_up=[ffn_w_up[l].astype(BF16) for l in range(depth)],
             ffn_w_down=[ffn_w_down[l].astype(BF16) for l in range(depth)])
    bp = x_prompt.shape[0]
    dtp = x_prompt.dtype
    yp = x_prompt
    ys = x_sample
    outs_p = []
    outs_s = []
    for l in range(depth):
        yp, st_p = trunk_layer(
            l, yp, None,
            jnp.zeros((bp, CONV_WIDTH - 1, GROUP_W), dtp), jnp.zeros((bp, RW_COLS), dtp),
            jnp.zeros((bp, N_HEADS, HEAD_DIM, HEAD_DIM), jnp.float32),
            jnp.zeros((bp, DN_CONV - 1, DN_QKV), dtp),
            jnp.zeros((bp, N_HEADS, HEAD_DIM, HEAD_DIM), jnp.float32), P, W)
        ys, st_s = trunk_layer(l, ys, (cache_k, cache_v, page_table), state_conv[l], state_rw_shift[l],
                               state_rw_wkv[l], state_dn_conv[l], state_dn_ssm[l], P, W)
        outs_p.append(st_p)
        outs_s.append(st_s)
    stk = lambda outs, i: jnp.stack([o[i] for o in outs])
    return (yp, ys, stk(outs_p, 0), stk(outs_p, 1), stk(outs_s, 0), stk(outs_s, 1),
            stk(outs_p, 2), stk(outs_s, 2), stk(outs_p, 3), stk(outs_s, 3),
            stk(outs_p, 4), stk(outs_s, 4), stk(outs_p, 5), stk(outs_s, 5),
            stk(outs_p, 6), stk(outs_s, 6))
```

```python
import functools
import math

import jax
import jax.numpy as jnp
from jax import lax
from jax.experimental import pallas as pl
from jax.experimental.pallas import tpu as pltpu

D_MODEL = 1024
HEAD_DIM = 64
N_HEADS = 4
GROUP_W = 256
DA_QK = HEAD_DIM // 2
CONV_WIDTH = 31
RW_W_RANK = 64
RW_A_RANK = 64
RW_G_RANK = 128
RW_OFF_W = 3 * GROUP_W
RW_OFF_A = RW_OFF_W + RW_W_RANK
RW_OFF_G = RW_OFF_A + RW_A_RANK
RW_COLS = RW_OFF_G + RW_G_RANK
DN_CONV = 4
DN_CHUNK = 64
DN_QKV = 3 * GROUP_W
DN_COLS = DN_QKV + 2 * N_HEADS + GROUP_W
DA_COLS = 3 * GROUP_W
CV_COLS = 2 * GROUP_W
OFF_CV = DA_COLS
OFF_RW = OFF_CV + CV_COLS
OFF_DN = OFF_RW + RW_COLS
N_IN = OFF_DN + DN_COLS
FFN_HIDDEN = 2816
RMS_EPS = 1e-6
LN_EPS = 1e-5
RW_GN_EPS = 64e-5
NEG_INF = -1e30
LOG2E = math.log2(math.e)

LANES = 128
SUBLANES = 8
DN_PAD = DN_QKV + GROUP_W + LANES
DN_HALO = 8
CV_HALO = 32
N_BR = 2 * N_HEADS
AT_TQ = 512
AT_TK = 512
PG_PAGES = 16
PG_SLOTS = 3
VMEM_LIMIT = 56 * 1024 * 1024

F32 = jnp.float32
BF16 = jnp.bfloat16


def _cparams(*sem):
    return pltpu.CompilerParams(dimension_semantics=sem, vmem_limit_bytes=VMEM_LIMIT)


def _rms(x, g):
    return x * lax.rsqrt(jnp.mean(x * x, axis=-1, keepdims=True) + RMS_EPS) * g


def _row_tile(n, target):
    return target if n % target == 0 else n


def _in_proj_body(x_ref, g_ref, w_ref, q_ref, k_ref, v_ref, cv_ref, rw_ref, dn_ref):
    hn = _rms(x_ref[...], g_ref[...]).astype(BF16)
    off = 0
    for ref in (q_ref, k_ref, v_ref, cv_ref, rw_ref, dn_ref):
        w = ref.shape[-1]
        ref[...] = jnp.dot(hn, w_ref[:, off:off + w], preferred_element_type=F32)
        off += w


def _prep_w_in(w):
    dn = w[:, OFF_DN:]
    dn = jnp.concatenate([dn[:, :DN_QKV], dn[:, DN_QKV + 2 * N_HEADS:], dn[:, DN_QKV:DN_QKV + 2 * N_HEADS],
                          jnp.zeros((w.shape[0], LANES - 2 * N_HEADS), w.dtype)], axis=1)
    return jnp.concatenate([w[:, :OFF_DN], dn], axis=1).astype(BF16)


def in_proj(x, g, w):
    n = x.shape[0]
    tm = _row_tile(n, 512)
    widths = (GROUP_W, GROUP_W, GROUP_W, CV_COLS, RW_COLS, DN_PAD)
    row = lambda wd: pl.BlockSpec((tm, wd), lambda i: (i, 0))
    return pl.pallas_call(
        _in_proj_body,
        grid=(n // tm,),
        in_specs=[row(D_MODEL), pl.BlockSpec((1, D_MODEL), lambda i: (0, 0)),
                  pl.BlockSpec(w.shape, lambda i: (0, 0))],
        out_specs=[row(wd) for wd in widths],
        out_shape=[jax.ShapeDtypeStruct((n, wd), F32) for wd in widths],
        compiler_params=_cparams("parallel"),
        name="in_proj",
    )(x, g.reshape(1, -1), w)


def _out_proj_body(x_ref, a_ref, b_ref, c_ref, d_ref, w_ref, g_ref, o_ref):
    acc = None
    for i, ref in enumerate((a_ref, b_ref, c_ref, d_ref)):
        part = jnp.dot(ref[...].astype(BF16), w_ref[i * GROUP_W:(i + 1) * GROUP_W, :],
                       preferred_element_type=F32)
        acc = part if acc is None else acc + part
    o_ref[...] = x_ref[...] + _rms(acc, g_ref[...])


def out_proj(x, oa, ob, oc, od, w, g):
    n = x.shape[0]
    tm = _row_tile(n, 512)
    row = lambda wd: pl.BlockSpec((tm, wd), lambda i: (i, 0))
    return pl.pallas_call(
        _out_proj_body,
        grid=(n // tm,),
        in_specs=[row(D_MODEL), row(GROUP_W), row(GROUP_W), row(GROUP_W), row(GROUP_W),
                  pl.BlockSpec(w.shape, lambda i: (0, 0)), pl.BlockSpec((1, D_MODEL), lambda i: (0, 0))],
        out_specs=row(D_MODEL),
        out_shape=jax.ShapeDtypeStruct((n, D_MODEL), F32),
        compiler_params=_cparams("parallel"),
        name="out_proj",
    )(x, oa, ob, oc, od, w, g.reshape(1, -1))


def _ffn_body(x_ref, g1_ref, wg_ref, wu_ref, wd_ref, g2_ref, o_ref, hn_ref, acc_ref):
    j = pl.program_id(1)

    @pl.when(j == 0)
    def _():
        hn_ref[...] = _rms(x_ref[...], g1_ref[...]).astype(BF16)
        acc_ref[...] = jnp.zeros_like(acc_ref)

    hn = hn_ref[...]
    gate = jnp.dot(hn, wg_ref[...], preferred_element_type=F32)
    up = jnp.dot(hn, wu_ref[...], preferred_element_type=F32)
    f = (gate * jax.nn.sigmoid(gate) * up).astype(BF16)
    acc_ref[...] += jnp.dot(f, wd_ref[...], preferred_element_type=F32)

    @pl.when(j == pl.num_programs(1) - 1)
    def _():
        o_ref[...] = x_ref[...] + _rms(acc_ref[...], g2_ref[...])


def ffn(x, g1, wg, wu, wd, g2):
    n = x.shape[0]
    tm = _row_tile(n, 512)
    th = FFN_HIDDEN // 2
    return pl.pallas_call(
        _ffn_body,
        grid=(n // tm, FFN_HIDDEN // th),
        in_specs=[pl.BlockSpec((tm, D_MODEL), lambda i, j: (i, 0)),
                  pl.BlockSpec((1, D_MODEL), lambda i, j: (0, 0)),
                  pl.BlockSpec((D_MODEL, th), lambda i, j: (0, j)),
                  pl.BlockSpec((D_MODEL, th), lambda i, j: (0, j)),
                  pl.BlockSpec((th, D_MODEL), lambda i, j: (j, 0)),
                  pl.BlockSpec((1, D_MODEL), lambda i, j: (0, 0))],
        out_specs=pl.BlockSpec((tm, D_MODEL), lambda i, j: (i, 0)),
        out_shape=jax.ShapeDtypeStruct((n, D_MODEL), F32),
        scratch_shapes=[pltpu.VMEM((tm, D_MODEL), BF16), pltpu.VMEM((tm, D_MODEL), F32)],
        compiler_params=_cparams("parallel", "arbitrary"),
        name="ffn",
    )(x, g1.reshape(1, -1), wg, wu, wd, g2.reshape(1, -1))


def _mm(a, b):
    return jnp.dot(a.astype(BF16), b.astype(BF16), preferred_element_type=F32)


def _mm_nt(a, b):
    return lax.dot_general(a.astype(BF16), b.astype(BF16), (((1,), (1,)), ((), ())),
                           preferred_element_type=F32)


def _mm_tn(a, b):
    return lax.dot_general(a.astype(BF16), b.astype(BF16), (((0,), (0,)), ((), ())),
                           preferred_element_type=F32)


def _split2(x):
    hi = x.astype(BF16)
    return hi, (x - hi.astype(F32)).astype(BF16)


def _mm_exact_rhs(a01, x):
    a = a01.astype(BF16)
    hi, lo = _split2(x)
    return jnp.dot(a, hi, preferred_element_type=F32) + jnp.dot(a, lo, preferred_element_type=F32)


def _mm_exact_lhs(x, a01):
    a = a01.astype(BF16)
    hi, lo = _split2(x)
    return jnp.dot(hi, a, preferred_element_type=F32) + jnp.dot(lo, a, preferred_element_type=F32)


def _iota2(n, m):
    return lax.broadcasted_iota(jnp.int32, (n, m), 0), lax.broadcasted_iota(jnp.int32, (n, m), 1)


def _same_block(r, c, blk):
    s = int(math.log2(blk))
    return (r >> s) == (c >> s)


def _unit_lower_inverse(mats, to_bd, chunk):
    r, c = _iota2(chunk, N_HEADS * chunk)
    c = c & (chunk - 1)
    eye = (r == c).astype(F32)
    base = min(8, chunk)
    blk = _same_block(r, c, base)
    pw = [jnp.where(blk, a, 0.0) for a in mats]
    d = [eye + p for p in pw]
    pw_bd = [to_bd(p) for p in pw]
    for _ in range(max(int(math.log2(base)) - 1, 0)):
        pw = [_mm(p, pb) for p, pb in zip(pw, pw_bd)]
        pw_bd = [to_bd(p) for p in pw]
        d = [x + _mm(x, pb) for x, pb in zip(d, pw_bd)]
    size = base * 2
    while size <= chunk:
        ring = _same_block(r, c, size) & ~_same_block(r, c, size // 2)
        t = [_mm(jnp.where(ring, a, 0.0), to_bd(x)) for a, x in zip(mats, d)]
        d = [x + _mm(x, to_bd(y)) for x, y in zip(d, t)]
        size *= 2
    return d


def _softplus(z):
    return jnp.maximum(z, 0.0) + jnp.log(1.0 + jnp.exp(-jnp.abs(z)))


def _sigmoid(z):
    return 1.0 / (1.0 + jnp.exp(-z))


def _block_diag_former(chunk, width):
    hw = N_HEADS * chunk
    lr, lc = _iota2(hw, width)
    own = (lr >> int(math.log2(chunk))) == (lc >> int(math.log2(width // N_HEADS)))

    def to_bd(z):
        return jnp.where(own, jnp.concatenate([z] * N_HEADS, axis=0), 0.0).astype(BF16)

    return to_bd


def _head_sum(x, head_ones):
    return _mm_exact_lhs(x, head_ones)


def _row_to_col(row_vals, r, c):
    n = r.shape[0]
    return jnp.sum(jnp.where(r == c, jnp.broadcast_to(row_vals, (n, n)), 0.0), axis=1, keepdims=True)


def _seqs_per_step(b, t):
    want = 4 if t >= HEAD_DIM else 8
    return want if b % want == 0 else 1


def _rwkv_body(p_ref, shift_ref, s0_ref, mu_ref, w0_ref, wup_ref, a0_ref, aup_ref, gup_ref, kk_ref, ka_ref,
               rk_ref, gng_ref, gnb_ref, y_ref, s_out_ref, st_ref, prev_ref, *, chunk, t_valid):
    nb, tb = p_ref.shape[0], p_ref.shape[1]
    ti = pl.program_id(1)
    gr, gcl = _iota2(GROUP_W, GROUP_W)
    head_mask = _same_block(gr, gcl, HEAD_DIM)
    head_ones = head_mask.astype(F32)
    to_bd = _block_diag_former(chunk, GROUP_W)
    to_bd_t = _block_diag_former(chunk, N_HEADS * chunk)

    @pl.when(ti == 0)
    def _():
        for s in range(nb):
            st_ref[s] = jnp.where(head_mask, jnp.concatenate([s0_ref[s]] * N_HEADS, axis=1), 0.0)
        prev_ref[...] = shift_ref[...]

    row = lax.broadcasted_iota(jnp.int32, (nb * tb, 1), 0)
    tr, tc = _iota2(tb, tb)
    cum = (_same_block(tr, tc, chunk) & (tr >= tc)).astype(F32)
    cr, cc = _iota2(chunk, N_HEADS * chunk)
    cc = cc & (chunk - 1)
    low_incl = cr >= cc
    low_strict = cr > cc
    slices = [slice(c0, c0 + chunk) for c0 in range(0, tb, chunk)]

    x = p_ref[...].reshape(nb * tb, RW_COLS)
    prev = pltpu.roll(x, 1, axis=0)
    for s in range(nb):
        prev = jnp.where(row == s * tb, prev_ref[s], prev)
        prev_ref[s] = x[(s + 1) * tb - 1:(s + 1) * tb, :]
    xm = x + (prev - x) * mu_ref[...]
    r = xm[:, :GROUP_W]
    k = xm[:, GROUP_W:2 * GROUP_W]
    v = xm[:, 2 * GROUP_W:RW_OFF_W]
    wd = xm[:, RW_OFF_W:RW_OFF_A]
    ad = xm[:, RW_OFF_A:RW_OFF_G]
    gd = xm[:, RW_OFF_G:RW_COLS]
    w_log = -_softplus(-(w0_ref[...] + _mm(jnp.tanh(wd), wup_ref[...]))) - 0.5
    logw = -jnp.exp(w_log)
    a = _sigmoid(a0_ref[...] + _mm(ad, aup_ref[...]))
    g = _mm(_sigmoid(gd), gup_ref[...])
    kk = k * kk_ref[...]
    kk = kk * lax.rsqrt(_head_sum(kk * kk, head_ones) + 1e-6)
    k2 = k * (1.0 + (a - 1.0) * ka_ref[...])
    bonus = _head_sum(r * k2 * rk_ref[...], head_ones) * v
    if t_valid < tb:
        ok = (row & (tb - 1)) < t_valid
        logw = jnp.where(ok, logw, 0.0)
        kk = jnp.where(ok, kk, 0.0)
        k2 = jnp.where(ok, k2, 0.0)
        v = jnp.where(ok, v, 0.0)
    alpha = -kk
    beta = kk * a
    seqs = []
    for s in range(nb):
        gcum = _mm_exact_rhs(cum, logw[s * tb:(s + 1) * tb])
        pre = []
        for c0 in range(0, tb, chunk):
            gc = gcum[c0:c0 + chunk]
            sl = slice(s * tb + c0, s * tb + c0 + chunk)
            g_last = gc[chunk - 1:chunk, :]
            e_neg = jnp.exp(-gc)
            e_rem = jnp.exp(g_last - gc)
            r_t = r[sl] * jnp.exp(gc)
            a_t = alpha[sl] * jnp.exp(gc - logw[sl])
            k_bd = to_bd(k2[sl] * e_neg)
            b_bd = to_bd(beta[sl] * e_neg)
            v_bd = to_bd(v[sl])
            ar = jnp.concatenate([a_t, r_t], axis=0)
            sc_b = _mm_nt(ar, b_bd)
            sc_k = _mm_nt(ar, k_bd)
            av = _mm(jnp.concatenate([jnp.where(low_strict, sc_k[:chunk], 0.0),
                                      jnp.where(low_incl, sc_k[chunk:], 0.0)], axis=0), v_bd)
            pre.append(dict(
                r_t=r_t, a_bd=to_bd(a_t),
                a_ab=jnp.where(low_strict, sc_b[:chunk], 0.0), a_ak_v=av[:chunk],
                a_rb=jnp.where(low_incl, sc_b[chunk:], 0.0), a_rk_v=av[chunk:],
                kv=_mm_tn(k2[sl] * e_rem, v[sl]), b_hat=beta[sl] * e_rem,
                p_col=_row_to_col(jnp.exp(g_last), gr, gcl)))
        seqs.append((pre, bonus[s * tb:(s + 1) * tb], g[s * tb:(s + 1) * tb]))
    flat = [c for pre, _, _ in seqs for c in pre]
    tinvs = _unit_lower_inverse([c["a_ab"] for c in flat], to_bd_t, chunk)
    for c, tinv in zip(flat, tinvs):
        c["w"] = _mm(tinv, c["a_bd"])
        c["u_loc"] = _mm(tinv, to_bd(c["a_ak_v"]))
    sts = [st_ref[s] for s in range(nb)]
    for ci, sl in enumerate(slices):
        for s in range(nb):
            c, st = seqs[s][0][ci], sts[s]
            ws = _mm(jnp.concatenate([c["w"], c["r_t"]], axis=0), st)
            u = c["u_loc"] + ws[:chunk]
            y = ws[chunk:] + c["a_rk_v"] + _mm(c["a_rb"], to_bd(u))
            sts[s] = st * c["p_col"] + jnp.where(head_mask, c["kv"] + _mm_tn(c["b_hat"], u), 0.0)
            yc = y - _head_sum(y, head_ones) * (1.0 / HEAD_DIM)
            yn = yc * lax.rsqrt(_head_sum(yc * yc, head_ones) * (1.0 / HEAD_DIM) + RW_GN_EPS)
            y_ref[s, sl, :] = yn
    for s in range(nb):
        st_ref[s] = sts[s]
        y_ref[s] = (y_ref[s] * gng_ref[...] + gnb_ref[...] + seqs[s][1]) * seqs[s][2]

    @pl.when(ti == pl.num_programs(1) - 1)
    def _():
        for s in range(nb):
            s_out_ref[s] = jnp.concatenate(
                [sts[s][h * HEAD_DIM:(h + 1) * HEAD_DIM, h * HEAD_DIM:(h + 1) * HEAD_DIM] for h in range(N_HEADS)],
                axis=0)


def rwkv_mix(p, shift_prev, s0, mu, w0, w_up, a0, a_up, g_up, k_k, k_a, r_k, gn_g, gn_b, *, t_valid):
    b, t, _ = p.shape
    chunk = min(HEAD_DIM, t)
    tb = min(256, t)
    nb = _seqs_per_step(b, t)
    row = lambda z: z.reshape(1, -1)
    st0 = jnp.swapaxes(s0, 2, 3).reshape(b, GROUP_W, HEAD_DIM)
    const = lambda shape: pl.BlockSpec(shape, lambda i, j: (0,) * len(shape))
    y, st = pl.pallas_call(
        functools.partial(_rwkv_body, chunk=chunk, t_valid=t_valid),
        grid=(b // nb, t // tb),
        in_specs=[pl.BlockSpec((nb, tb, RW_COLS), lambda i, j: (i, j, 0)),
                  pl.BlockSpec((nb, 1, RW_COLS), lambda i, j: (i, 0, 0)),
                  pl.BlockSpec((nb, GROUP_W, HEAD_DIM), lambda i, j: (i, 0, 0)),
                  const((1, RW_COLS)), const((1, GROUP_W)), const((RW_W_RANK, GROUP_W)),
                  const((1, GROUP_W)), const((RW_A_RANK, GROUP_W)), const((RW_G_RANK, GROUP_W)),
                  const((1, GROUP_W)), const((1, GROUP_W)), const((1, GROUP_W)),
                  const((1, GROUP_W)), const((1, GROUP_W))],
        out_specs=[pl.BlockSpec((nb, tb, GROUP_W), lambda i, j: (i, j, 0)),
                   pl.BlockSpec((nb, GROUP_W, HEAD_DIM), lambda i, j: (i, 0, 0))],
        out_shape=[jax.ShapeDtypeStruct((b, t, GROUP_W), F32),
                   jax.ShapeDtypeStruct((b, GROUP_W, HEAD_DIM), F32)],
        scratch_shapes=[pltpu.VMEM((nb, GROUP_W, GROUP_W), F32), pltpu.VMEM((nb, 1, RW_COLS), F32)],
        compiler_params=_cparams("parallel", "arbitrary"),
        name="rwkv_mix",
    )(p, shift_prev.reshape(b, 1, RW_COLS), st0, row(mu), row(w0), w_up, row(a0), a_up, g_up,
      row(k_k), row(k_a), row(r_k), row(gn_g), row(gn_b))
    return y, jnp.swapaxes(st.reshape(b, N_HEADS, HEAD_DIM, HEAD_DIM), 2, 3)


def _dn_body(p_ref, cprev_ref, s0_ref, cw_ref, alog_ref, dtb_ref, ng_ref, y_ref, s_out_ref, st_ref, hist_ref,
             *, chunk, t_valid):
    nb, tb = p_ref.shape[0], p_ref.shape[1]
    ti = pl.program_id(1)
    gr, gcl = _iota2(GROUP_W, GROUP_W)
    head_mask = _same_block(gr, gcl, HEAD_DIM)
    head_ones = head_mask.astype(F32)
    to_bd = _block_diag_former(chunk, GROUP_W)
    to_bd_t = _block_diag_former(chunk, N_HEADS * chunk)

    @pl.when(ti == 0)
    def _():
        for s in range(nb):
            st_ref[s] = jnp.where(head_mask, jnp.concatenate([s0_ref[s]] * N_HEADS, axis=1), 0.0)
        hist_ref[...] = cprev_ref[...]

    er, ec = _iota2(LANES, GROUP_W)
    lanes_a = (er == (ec >> 6)).astype(F32)
    lanes_b = (er == (ec >> 6) + N_HEADS).astype(F32)
    tr, tc = _iota2(tb, tb)
    cum = (_same_block(tr, tc, chunk) & (tr >= tc)).astype(F32)
    cr, cc = _iota2(chunk, N_HEADS * chunk)
    cc = cc & (chunk - 1)
    low_incl = cr >= cc
    low_strict = cr > cc
    eye_c = cr == cc
    slices = [slice(c0, c0 + chunk) for c0 in range(0, tb, chunk)]

    convs = []
    for s in range(nb):
        ext = jnp.concatenate([hist_ref[s], p_ref[s, :, :DN_QKV]], axis=0)
        hist_ref[s] = ext[tb:tb + DN_HALO, :]
        conv = None
        for j in range(DN_CONV):
            lo = DN_HALO - (DN_CONV - 1) + j
            term = ext[lo:lo + tb, :] * cw_ref[j:j + 1, :]
            conv = term if conv is None else conv + term
        convs.append(conv)
    conv = jnp.concatenate(convs, axis=0)
    x = p_ref[...].reshape(nb * tb, DN_PAD)
    qkv = conv * _sigmoid(conv)
    q = qkv[:, :GROUP_W]
    k = qkv[:, GROUP_W:2 * GROUP_W]
    v = qkv[:, 2 * GROUP_W:]
    q = q * lax.rsqrt(_head_sum(q * q, head_ones) + 1e-6) * (HEAD_DIM ** -0.5)
    k = k * lax.rsqrt(_head_sum(k * k, head_ones) + 1e-6)
    z = x[:, DN_QKV:DN_QKV + GROUP_W]
    ab = x[:, DN_QKV + GROUP_W:]
    g_w = _mm_exact_lhs(-jnp.exp(alog_ref[...]) * _softplus(ab + dtb_ref[...]), lanes_a)
    beta = _mm_exact_lhs(_sigmoid(ab), lanes_b)
    if t_valid < tb:
        ok = (lax.broadcasted_iota(jnp.int32, (nb * tb, 1), 0) & (tb - 1)) < t_valid
        g_w = jnp.where(ok, g_w, 0.0)
        beta = jnp.where(ok, beta, 0.0)
        k = jnp.where(ok, k, 0.0)
        v = jnp.where(ok, v, 0.0)
    kb = k * beta
    vb = v * beta
    seqs = []
    for s in range(nb):
        gcum = _mm_exact_rhs(cum, g_w[s * tb:(s + 1) * tb])
        pre = []
        for c0 in range(0, tb, chunk):
            gc = gcum[c0:c0 + chunk]
            sl = slice(s * tb + c0, s * tb + c0 + chunk)
            g_last = gc[chunk - 1:chunk, :]
            gi = gc if chunk == HEAD_DIM else jnp.concatenate(
                [gc[:, h * HEAD_DIM:h * HEAD_DIM + chunk] for h in range(N_HEADS)], axis=1)
            gj = jnp.sum(jnp.where(eye_c, gi, 0.0), axis=0, keepdims=True)
            decay = jnp.exp(jnp.where(low_incl, gi - gj, NEG_INF))
            k_bd = to_bd(k[sl])
            sc = _mm_nt(jnp.concatenate([kb[sl], q[sl]], axis=0), k_bd)
            pre.append(dict(
                m=jnp.where(low_strict, sc[:chunk] * decay, 0.0),
                qk=jnp.where(low_incl, sc[chunk:] * decay, 0.0),
                kbg_bd=to_bd(kb[sl] * jnp.exp(gc)), q_g=q[sl] * jnp.exp(gc), vb_bd=to_bd(vb[sl]),
                k_rem=k[sl] * jnp.exp(g_last - gc), p_col=_row_to_col(jnp.exp(g_last), gr, gcl)))
        seqs.append((pre, z[s * tb:(s + 1) * tb]))
    flat = [c for pre, _ in seqs for c in pre]
    tinvs = _unit_lower_inverse([-c["m"] for c in flat], to_bd_t, chunk)
    for c, tinv in zip(flat, tinvs):
        c["w"] = _mm(tinv, c["kbg_bd"])
        c["u_loc"] = _mm(tinv, c["vb_bd"])
    sts = [st_ref[s] for s in range(nb)]
    for ci, sl in enumerate(slices):
        for s in range(nb):
            c, st = seqs[s][0][ci], sts[s]
            ws = _mm(jnp.concatenate([c["w"], c["q_g"]], axis=0), st)
            v_new = c["u_loc"] - ws[:chunk]
            o = ws[chunk:] + _mm(c["qk"], to_bd(v_new))
            sts[s] = st * c["p_col"] + jnp.where(head_mask, _mm_tn(c["k_rem"], v_new), 0.0)
            y_ref[s, sl, :] = o * lax.rsqrt(_head_sum(o * o, head_ones) * (1.0 / HEAD_DIM) + RMS_EPS) * ng_ref[...]
    for s in range(nb):
        st_ref[s] = sts[s]
        z = seqs[s][1]
        y_ref[s] = y_ref[s] * (z * _sigmoid(z))

    @pl.when(ti == pl.num_programs(1) - 1)
    def _():
        for s in range(nb):
            s_out_ref[s] = jnp.concatenate(
                [sts[s][h * HEAD_DIM:(h + 1) * HEAD_DIM, h * HEAD_DIM:(h + 1) * HEAD_DIM] for h in range(N_HEADS)],
                axis=0)


def dn_mix(p, conv_prev, s0, conv_w, a_log, dt_bias, norm_g, *, t_valid):
    b, t, _ = p.shape
    chunk = min(DN_CHUNK, t)
    tb = min(256, t)
    nb = _seqs_per_step(b, t)
    hist = jnp.pad(conv_prev, ((0, 0), (DN_HALO - (DN_CONV - 1), 0), (0, 0)))
    lane_pad = lambda z: jnp.pad(z, (0, LANES - z.shape[0])).reshape(1, LANES)
    const = lambda shape: pl.BlockSpec(shape, lambda i, j: (0,) * len(shape))
    y, st = pl.pallas_call(
        functools.partial(_dn_body, chunk=chunk, t_valid=t_valid),
        grid=(b // nb, t // tb),
        in_specs=[pl.BlockSpec((nb, tb, DN_PAD), lambda i, j: (i, j, 0)),
                  pl.BlockSpec((nb, DN_HALO, DN_QKV), lambda i, j: (i, 0, 0)),
                  pl.BlockSpec((nb, GROUP_W, HEAD_DIM), lambda i, j: (i, 0, 0)),
                  const((DN_CONV, DN_QKV)), const((1, LANES)), const((1, LANES)), const((1, GROUP_W))],
        out_specs=[pl.BlockSpec((nb, tb, GROUP_W), lambda i, j: (i, j, 0)),
                   pl.BlockSpec((nb, GROUP_W, HEAD_DIM), lambda i, j: (i, 0, 0))],
        out_shape=[jax.ShapeDtypeStruct((b, t, GROUP_W), F32),
                   jax.ShapeDtypeStruct((b, GROUP_W, HEAD_DIM), F32)],
        scratch_shapes=[pltpu.VMEM((nb, GROUP_W, GROUP_W), F32), pltpu.VMEM((nb, DN_HALO, DN_QKV), F32)],
        compiler_params=_cparams("parallel", "arbitrary"),
        name="dn_mix",
    )(p, hist, s0.reshape(b, GROUP_W, HEAD_DIM), conv_w, lane_pad(a_log), lane_pad(dt_bias),
      jnp.tile(norm_g, N_HEADS).reshape(1, GROUP_W))
    return y, st.reshape(b, N_HEADS, HEAD_DIM, HEAD_DIM)


def _diff_lambda(lq1_ref, lk1_ref, lq2_ref, lk2_ref, lam_init):
    return (jnp.exp(jnp.sum(lq1_ref[...] * lk1_ref[...], axis=-1, keepdims=True))
            - jnp.exp(jnp.sum(lq2_ref[...] * lk2_ref[...], axis=-1, keepdims=True)) + lam_init)


def _head_rms(o, g, scale):
    gr, gcl = _iota2(GROUP_W, GROUP_W)
    head_ones = _same_block(gr, gcl, HEAD_DIM).astype(F32)
    ms = _mm_exact_lhs(o * o, head_ones) * (1.0 / HEAD_DIM)
    return o * lax.rsqrt(ms + RMS_EPS) * g * scale


def _attn_body(q_ref, k_ref, v_ref, lq1_ref, lk1_ref, lq2_ref, lk2_ref, ng_ref, o_ref,
               kb_ref, vt_ref, bias_ref, biasd_ref, qs_ref, *stats, lam_init):
    m_refs, acc_refs = stats[:N_BR], stats[N_BR:]
    b = pl.program_id(0)
    i = pl.program_id(1)
    tq, tk = AT_TQ, AT_TK
    t = k_ref.shape[1]
    n_sub = tq // tk
    slope = [LOG2E * 2.0 ** (-2 * (h + 1)) for h in range(N_HEADS)]

    @pl.when((b == 0) & (i == 0))
    def _():
        jj, ii = _iota2(tk, tq)
        rel = (ii - jj).astype(F32)
        for h in range(N_HEADS):
            bias_ref[:, h * tq:(h + 1) * tq] = -slope[h] * rel
            for d in range(n_sub):
                rd = rel - float(d * tk)
                biasd_ref[d, :, h * tq:(h + 1) * tq] = jnp.where(rd >= 0, -slope[h] * rel, NEG_INF)

    @pl.when(i == 0)
    def _():
        kb_ref[...] = k_ref[0].astype(BF16)
        for j in range(t // tk):
            vt_ref[:, j * tk:(j + 1) * tk] = v_ref[0, j * tk:(j + 1) * tk, :].T.astype(BF16)

    q = q_ref[0] * (DA_QK ** -0.5 * LOG2E)
    lane = lax.broadcasted_iota(jnp.int32, (1, GROUP_W), 1)
    for c in range(N_BR):
        own = (lane >> 5) == c
        qs_ref[c * tq:(c + 1) * tq, :] = jnp.where(own, q, 0.0).astype(BF16)
    for c in range(N_BR):
        m_refs[c][...] = jnp.full_like(m_refs[c], NEG_INF)
        acc_refs[c][...] = jnp.zeros_like(acc_refs[c])

    def tile(j, b_ref):
        start = pl.multiple_of(j * tk, tk)
        kt = kb_ref[pl.ds(start, tk), :]
        dist = (tq * i - tk * j).astype(F32)
        ss = [_mm_nt(kt, qs_ref[c * tq:(c + 1) * tq, :]) for c in range(N_BR)]
        ps, alphas = [], []
        for c in range(N_BR):
            h = c >> 1
            s = ss[c] + b_ref[:, h * tq:(h + 1) * tq]
            ct = -slope[h] * dist
            m_old = m_refs[c][...]
            m_new = jnp.maximum(m_old, jnp.max(s, axis=0, keepdims=True) + ct)
            alpha = jnp.exp2(m_old - m_new)
            p = jnp.exp2(s - (m_new - ct))
            m_refs[c][...] = m_new
            ps.append(p.astype(BF16))
            alphas.append(alpha)
        ones = jnp.ones((2 * SUBLANES, tk), BF16)
        for c in range(N_BR):
            h = c >> 1
            lhs = jnp.concatenate([vt_ref[h * HEAD_DIM:(h + 1) * HEAD_DIM, pl.ds(start, tk)], ones], axis=0)
            pv = jnp.dot(lhs, ps[c], preferred_element_type=F32)
            acc_refs[c][...] = acc_refs[c][...] * alphas[c] + pv

    def body(j, carry):
        tile(j, bias_ref)
        return carry

    lax.fori_loop(0, i * n_sub, body, 0)
    for d in range(n_sub):
        tile(i * n_sub + d, biasd_ref.at[d])

    lam = _diff_lambda(lq1_ref, lk1_ref, lq2_ref, lk2_ref, lam_init)
    norm = [acc_refs[c][:HEAD_DIM, :] * (1.0 / acc_refs[c][HEAD_DIM:HEAD_DIM + 1, :]) for c in range(N_BR)]
    o = jnp.concatenate([norm[2 * h] - lam * norm[2 * h + 1] for h in range(N_HEADS)], axis=0).T
    o_ref[0] = _head_rms(o, ng_ref[...], 1.0 - lam_init)


def attn_prompt(q, k, v, lq1, lk1, lq2, lk2, norm_g, lam_init):
    b, t, _ = q.shape
    tq, tk = AT_TQ, AT_TK
    assert t % tq == 0 and tq % tk == 0
    row = lambda z: z.reshape(1, -1)
    const = lambda shape: pl.BlockSpec(shape, lambda i, j: (0,) * len(shape))
    return pl.pallas_call(
        functools.partial(_attn_body, lam_init=lam_init),
        grid=(b, t // tq),
        in_specs=[pl.BlockSpec((1, tq, GROUP_W), lambda i, j: (i, j, 0)),
                  pl.BlockSpec((1, t, GROUP_W), lambda i, j: (i, 0, 0)),
                  pl.BlockSpec((1, t, GROUP_W), lambda i, j: (i, 0, 0)),
                  const((1, DA_QK)), const((1, DA_QK)), const((1, DA_QK)), const((1, DA_QK)),
                  const((1, GROUP_W))],
        out_specs=pl.BlockSpec((1, tq, GROUP_W), lambda i, j: (i, j, 0)),
        out_shape=jax.ShapeDtypeStruct((b, t, GROUP_W), F32),
        scratch_shapes=[pltpu.VMEM((t, GROUP_W), BF16), pltpu.VMEM((GROUP_W, t), BF16),
                        pltpu.VMEM((tk, N_HEADS * tq), F32), pltpu.VMEM((tq // tk, tk, N_HEADS * tq), F32),
                        pltpu.VMEM((N_BR * tq, GROUP_W), BF16),
                        *([pltpu.VMEM((1, tq), F32)] * N_BR),
                        *([pltpu.VMEM((HEAD_DIM + 2 * SUBLANES, tq), F32)] * N_BR)],
        compiler_params=_cparams("arbitrary", "arbitrary"),
        name="attn_prompt",
    )(q, k, v, row(lq1), row(lk1), row(lq2), row(lk2), row(jnp.tile(norm_g, N_HEADS)))


def _paged_body(pt_ref, q_ref, kn_ref, vn_ref, lq1_ref, lk1_ref, lq2_ref, lk2_ref, ng_ref, ck_hbm, cv_hbm, o_ref,
                kbuf, vbuf, sem, qs_ref, base_ref, m_ref, l_ref, acc_ref, *, layer, lam_init, page, n_pages):
    n_seq = q_ref.shape[0]
    n_groups = n_pages // PG_PAGES
    n_steps = n_seq * n_groups
    span = PG_PAGES * page
    past = n_pages * page
    rowc = lax.broadcasted_iota(jnp.int32, (N_BR, 1), 0)
    slope = jnp.exp2(-2.0 * ((rowc >> 1) + 1).astype(F32))
    lane = lax.broadcasted_iota(jnp.int32, (N_BR, GROUP_W), 1)
    rown = lax.broadcasted_iota(jnp.int32, (N_BR, GROUP_W), 0)
    base_ref[...] = slope * lax.broadcasted_iota(jnp.int32, (N_BR, span), 1).astype(F32)

    def page_copies(step, slot):
        seq, grp = step // n_groups, step % n_groups
        for r in range(PG_PAGES):
            pid = pt_ref[seq, grp * PG_PAGES + r]
            yield pltpu.make_async_copy(ck_hbm.at[layer, pid], kbuf.at[slot, r], sem.at[slot])
            yield pltpu.make_async_copy(cv_hbm.at[layer, pid], vbuf.at[slot, r], sem.at[slot])

    def start(step, slot):
        for cp in page_copies(step, slot):
            cp.start()

    def wait(step, slot):
        for cp in page_copies(step, slot):
            cp.wait()

    for ahead in range(PG_SLOTS - 1):
        start(ahead, ahead)

    def body(step, carry):
        slot = lax.rem(step, PG_SLOTS)
        nxt = step + (PG_SLOTS - 1)

        @pl.when(nxt < n_steps)
        def _():
            start(nxt, lax.rem(nxt, PG_SLOTS))

        wait(step, slot)
        seq, grp = step // n_groups, step % n_groups

        @pl.when(grp == 0)
        def _():
            q = q_ref[pl.ds(seq, 1), :] * (DA_QK ** -0.5)
            qs_ref[...] = jnp.where((lane >> 5) == rown, jnp.broadcast_to(q, (N_BR, GROUP_W)), 0.0)
            m_ref[...] = jnp.full_like(m_ref, NEG_INF)
            l_ref[...] = jnp.zeros_like(l_ref)
            acc_ref[...] = jnp.zeros_like(acc_ref)

        qs = qs_ref[...]
        qb = qs.astype(BF16)
        s = jnp.concatenate(
            [jnp.dot(qb, kbuf[slot, r].astype(BF16), preferred_element_type=F32) for r in range(PG_PAGES)],
            axis=1)
        off = -slope * jnp.asarray(past - grp * span, F32)
        s = s + base_ref[...]
        m_old = m_ref[...]
        m_new = jnp.maximum(m_old, jnp.max(s, axis=1, keepdims=True) + off)
        alpha = jnp.exp(m_old - m_new)
        p = jnp.exp(s - (m_new - off))
        l_new = alpha * l_ref[...] + jnp.sum(p, axis=1, keepdims=True)
        pb = p.astype(BF16)
        pv = None
        for r in range(PG_PAGES):
            d = _mm_nt(pb[:, r * page:(r + 1) * page], vbuf[slot, r])
            pv = d if pv is None else pv + d
        acc = acc_ref[...] * alpha + pv
        m_ref[...] = m_new
        l_ref[...] = l_new
        acc_ref[...] = acc

        @pl.when(grp == n_groups - 1)
        def _():
            s_self = jnp.sum(qs * kn_ref[pl.ds(seq, 1), :], axis=1, keepdims=True)
            m_fin = jnp.maximum(m_new, s_self)
            a_fin = jnp.exp(m_new - m_fin)
            p_self = jnp.exp(s_self - m_fin)
            l_fin = a_fin * l_new + p_self
            out = (acc * a_fin + p_self * vn_ref[pl.ds(seq, 1), :]) / l_fin
            lam = _diff_lambda(lq1_ref, lk1_ref, lq2_ref, lk2_ref, lam_init)
            coef = jnp.where((rown & 1) == 0, 1.0, -lam)
            o = jnp.sum(jnp.where((lane >> 6) == (rown >> 1), out * coef, 0.0), axis=0, keepdims=True)
            o8 = jnp.broadcast_to(o, (SUBLANES, GROUP_W))
            o_ref[pl.ds(seq, 1), :] = _head_rms(o8, ng_ref[...], 1.0 - lam_init)[0:1]

        return carry

    lax.fori_loop(0, n_steps, body, 0)


def attn_sample(q, k_new, v_new, cache_k, cache_v, page_table, layer, lq1, lk1, lq2, lk2, norm_g, lam_init):
    b = q.shape[0]
    depth, n_pool, page = cache_k.shape[:3]
    n_pages = page_table.shape[1]
    assert n_pages % PG_PAGES == 0 and b * (n_pages // PG_PAGES) >= PG_SLOTS
    as_pages = lambda c: jnp.transpose(c, (0, 1, 3, 4, 2)).reshape(depth, n_pool, GROUP_W, page)
    row = lambda z: z.reshape(1, -1)
    full = lambda shape: pl.BlockSpec(shape, lambda i, pt: (0,) * len(shape))
    hbm = pl.BlockSpec(memory_space=pl.ANY)
    span = PG_PAGES * page
    grid_spec = pltpu.PrefetchScalarGridSpec(
        num_scalar_prefetch=1,
        grid=(1,),
        in_specs=[full((b, GROUP_W)), full((b, GROUP_W)), full((b, GROUP_W)),
                  full((1, DA_QK)), full((1, DA_QK)), full((1, DA_QK)), full((1, DA_QK)), full((1, GROUP_W)),
                  hbm, hbm],
        out_specs=full((b, GROUP_W)),
        scratch_shapes=[pltpu.VMEM((PG_SLOTS, PG_PAGES, GROUP_W, page), F32),
                        pltpu.VMEM((PG_SLOTS, PG_PAGES, GROUP_W, page), F32),
                        pltpu.SemaphoreType.DMA((PG_SLOTS,)),
                        pltpu.VMEM((N_BR, GROUP_W), F32), pltpu.VMEM((N_BR, span), F32),
                        pltpu.VMEM((N_BR, 1), F32), pltpu.VMEM((N_BR, 1), F32), pltpu.VMEM((N_BR, GROUP_W), F32)])
    return pl.pallas_call(
        functools.partial(_paged_body, layer=layer, lam_init=lam_init, page=page, n_pages=n_pages),
        grid_spec=grid_spec,
        out_shape=jax.ShapeDtypeStruct((b, GROUP_W), F32),
        compiler_params=_cparams("arbitrary"),
        name="attn_sample",
    )(page_table, q, k_new, v_new, row(lq1), row(lk1), row(lq2), row(lk2),
      row(jnp.tile(norm_g, N_HEADS)), as_pages(cache_k), as_pages(cache_v))


def _conv_body(p_ref, buf_ref, w_ref, b_ref, lg_ref, lb_ref, y_ref, tail_ref, ext_ref, sh_ref, *, t_valid):
    tb = p_ref.shape[1]
    ti = pl.program_id(1)

    @pl.when(ti == 0)
    def _():
        ext_ref[0:CV_HALO, :] = buf_ref[0]

    x = p_ref[0]
    ext_ref[CV_HALO:CV_HALO + tb, :] = x[:, :GROUP_W] * _sigmoid(x[:, GROUP_W:])
    span = tb + CV_HALO - SUBLANES
    for ph in range(1, SUBLANES):
        sh_ref[ph - 1] = ext_ref[ph:ph + span, :]
    acc = None
    for j in range(CONV_WIDTH):
        lo = CV_HALO - (CONV_WIDTH - 1) + j
        ph, base = lo % SUBLANES, lo - lo % SUBLANES
        rows = ext_ref[base:base + tb, :] if ph == 0 else sh_ref[ph - 1, base:base + tb, :]
        term = rows * w_ref[j:j + 1, :]
        acc = term if acc is None else acc + term
    hist_new = ext_ref[t_valid:t_valid + CV_HALO, :]
    ext_ref[0:CV_HALO, :] = hist_new
    h = acc + b_ref[...]
    hc = h - jnp.mean(h, axis=-1, keepdims=True)
    var = jnp.mean(hc * hc, axis=-1, keepdims=True)
    hn = hc * lax.rsqrt(var + LN_EPS) * lg_ref[...] + lb_ref[...]
    y_ref[0] = hn * _sigmoid(hn)

    @pl.when(ti == pl.num_programs(1) - 1)
    def _():
        tail_ref[0] = hist_new


def conv_mix(p, buf, conv_w, conv_b, ln_g, ln_b, *, t_valid):
    b, t, _ = p.shape
    tb = min(512, t)
    assert t == tb or t_valid == t
    keep = CONV_WIDTH - 1
    hist = jnp.pad(buf, ((0, 0), (CV_HALO - keep, 0), (0, 0)))
    row = lambda z: z.reshape(1, -1)
    const = lambda shape: pl.BlockSpec(shape, lambda i, j: (0,) * len(shape))
    y, tail = pl.pallas_call(
        functools.partial(_conv_body, t_valid=min(t_valid, tb)),
        grid=(b, t // tb),
        in_specs=[pl.BlockSpec((1, tb, CV_COLS), lambda i, j: (i, j, 0)),
                  pl.BlockSpec((1, CV_HALO, GROUP_W), lambda i, j: (i, 0, 0)),
                  const((CONV_WIDTH, GROUP_W)), const((1, GROUP_W)), const((1, GROUP_W)), const((1, GROUP_W))],
        out_specs=[pl.BlockSpec((1, tb, GROUP_W), lambda i, j: (i, j, 0)),
                   pl.BlockSpec((1, CV_HALO, GROUP_W), lambda i, j: (i, 0, 0))],
        out_shape=[jax.ShapeDtypeStruct((b, t, GROUP_W), F32),
                   jax.ShapeDtypeStruct((b, CV_HALO, GROUP_W), F32)],
        scratch_shapes=[pltpu.VMEM((CV_HALO + tb, GROUP_W), F32),
                        pltpu.VMEM((SUBLANES - 1, CV_HALO + tb - SUBLANES, GROUP_W), F32)],
        compiler_params=_cparams("parallel", "arbitrary"),
        name="conv_mix",
    )(p, hist, conv_w, row(conv_b), row(ln_g), row(ln_b))
    return y, tail[:, CV_HALO - keep:]


def trunk_layer(l, x, paged, cv_buf, rw_shift, rw_S, dn_buf, dn_S, P, W):
    b, t, _ = x.shape
    n = b * t
    x2 = x.reshape(n, D_MODEL)
    q, k, v, p_cv, p_rw, p_dn = in_proj(x2, P['g_pre_mix'][l], W['w_in'][l])
    lam_init = 0.8 - 0.6 * math.exp(-0.3 * l)
    lam_args = (P['da_lq1'][l], P['da_lk1'][l], P['da_lq2'][l], P['da_lk2'][l], P['da_norm_g'][l], lam_init)
    if paged is None:
        r3 = lambda z: z.reshape(b, t, GROUP_W)
        o_a = attn_prompt(r3(q), r3(k), r3(v), *lam_args).reshape(n, GROUP_W)
    else:
        assert t == 1
        o_a = attn_sample(q, k, v, *paged, l, *lam_args)
    hs = lambda z: z.reshape(b, t, N_HEADS, HEAD_DIM)
    k, v = hs(k), hs(v)
    t_pad = -(-t // SUBLANES) * SUBLANES
    pad_t = lambda z: z if t_pad == t else jnp.pad(z, ((0, 0), (0, t_pad - t), (0, 0)))
    o_b, cv_new = conv_mix(pad_t(p_cv.reshape(b, t, CV_COLS)), cv_buf, P['cv_w'][l], P['cv_b'][l],
                           P['cv_ln_g'][l], P['cv_ln_b'][l], t_valid=t)
    o_b = o_b[:, :t]
    p_rw = p_rw.reshape(b, t, RW_COLS)
    o_c, rw_S_new = rwkv_mix(
        pad_t(p_rw), rw_shift, rw_S, P['rw_mu'][l], P['rw_w0'][l], P['rw_w_up'][l],
        P['rw_a0'][l], P['rw_a_up'][l], P['rw_g_up'][l], P['rw_k_k'][l], P['rw_k_a'][l],
        P['rw_r_k'][l].reshape(-1), P['rw_gn_g'][l], P['rw_gn_b'][l], t_valid=t)
    o_c = o_c[:, :t]
    rw_shift_new = p_rw[:, -1]
    p_dn = p_dn.reshape(b, t, DN_PAD)
    o_d, dn_S_new = dn_mix(pad_t(p_dn), dn_buf, dn_S, P['dn_conv_w'][l], P['dn_A_log'][l],
                           P['dn_dt_bias'][l], P['dn_norm_g'][l], t_valid=t)
    o_d = o_d[:, :t]
    keep = DN_CONV - 1
    dn_buf_new = (p_dn[:, t - keep:, :DN_QKV] if t >= keep else
                  jnp.concatenate([dn_buf, p_dn[..., :DN_QKV]], axis=1)[:, -keep:])
    x2 = out_proj(x2, o_a, o_b.reshape(n, GROUP_W), o_c.reshape(n, GROUP_W), o_d.reshape(n, GROUP_W),
                  W['w_out'][l], P['g_post_mix'][l])
    x2 = ffn(x2, P['g_pre_ffn'][l], W['ffn_w_gate'][l], W['ffn_w_up'][l], W['ffn_w_down'][l],
             P['g_post_ffn'][l])
    return x2.reshape(b, t, D_MODEL), (k, v, cv_new, rw_shift_new, rw_S_new, dn_buf_new, dn_S_new)


def kernel(x_prompt, x_sample, cache_k, cache_v, page_table, state_conv, state_rw_shift,
           state_rw_wkv, state_dn_conv, state_dn_ssm, g_pre_mix, g_post_mix, g_pre_ffn,
           g_post_ffn, w_in, w_out, da_lq1, da_lk1, da_lq2, da_lk2, da_norm_g, cv_w, cv_b,
           cv_ln_g, cv_ln_b, rw_mu, rw_w0, rw_w_up, rw_a0, rw_a_up, rw_g_up, rw_k_k, rw_k_a,
           rw_r_k, rw_gn_g, rw_gn_b, dn_conv_w, dn_A_log, dn_dt_bias, dn_norm_g,
           ffn_w_gate, ffn_w_up, ffn_w_down):
    P = dict(g_pre_mix=g_pre_mix, g_post_mix=g_post_mix, g_pre_ffn=g_pre_ffn, g_post_ffn=g_post_ffn,
             da_lq1=da_lq1, da_lk1=da_lk1, da_lq2=da_lq2, da_lk2=da_lk2,
             da_norm_g=da_norm_g, cv_w=cv_w, cv_b=cv_b, cv_ln_g=cv_ln_g, cv_ln_b=cv_ln_b,
             rw_mu=rw_mu, rw_w0=rw_w0, rw_w_up=rw_w_up, rw_a0=rw_a0, rw_a_up=rw_a_up,
             rw_g_up=rw_g_up, rw_k_k=rw_k_k, rw_k_a=rw_k_a, rw_r_k=rw_r_k, rw_gn_g=rw_gn_g,
             rw_gn_b=rw_gn_b, dn_conv_w=dn_conv_w, dn_A_log=dn_A_log, dn_dt_bias=dn_dt_bias,
             dn_norm_g=dn_norm_g)
    depth = w_in.shape[0]
    W = dict(w_in=[_prep_w_in(w_in[l]) for l in range(depth)],
             w_out=[w_out[l].astype(BF16) for l in range(depth)],
             ffn_w_gate=[ffn_w_gate[l].astype(BF16) for l in range(depth)],
             ffn_w_up=[ffn_w_up[l].astype(BF16) for l in range(depth)],
             ffn_w_down=[ffn_w_down[l].astype(BF16) for l in range(depth)])
    bp = x_prompt.shape[0]
    dtp = x_prompt.dtype
    yp = x_prompt
    ys = x_sample
    outs_p = []
    outs_s = []
    for l in range(depth):
        yp, st_p = trunk_layer(
            l, yp, None,
            jnp.zeros((bp, CONV_WIDTH - 1, GROUP_W), dtp), jnp.zeros((bp, RW_COLS), dtp),
            jnp.zeros((bp, N_HEADS, HEAD_DIM, HEAD_DIM), jnp.float32),
            jnp.zeros((bp, DN_CONV - 1, DN_QKV), dtp),
            jnp.zeros((bp, N_HEADS, HEAD_DIM, HEAD_DIM), jnp.float32), P, W)
        ys, st_s = trunk_layer(l, ys, (cache_k, cache_v, page_table), state_conv[l], state_rw_shift[l],
                               state_rw_wkv[l], state_dn_conv[l], state_dn_ssm[l], P, W)
        outs_p.append(st_p)
        outs_s.append(st_s)
    stk = lambda outs, i: jnp.stack([o[i] for o in outs])
    return (yp, ys, stk(outs_p, 0), stk(outs_p, 1), stk(outs_s, 0), stk(outs_s, 1),
            stk(outs_p, 2), stk(outs_s, 2), stk(outs_p, 3), stk(outs_s, 3),
            stk(outs_p, 4), stk(outs_s, 4), stk(outs_p, 5), stk(outs_s, 5),
            stk(outs_p, 6), stk(outs_s, 6))
```

```python
import functools
import math

import jax
import jax.numpy as jnp
from jax import lax
from jax.experimental import pallas as pl
from jax.experimental.pallas import tpu as pltpu

D_MODEL = 1024
HEAD_DIM = 64
N_HEADS = 4
GROUP_W = 256
DA_QK = HEAD_DIM // 2
CONV_WIDTH = 31
RW_W_RANK = 64
RW_A_RANK = 64
RW_G_RANK = 128
RW_OFF_W = 3 * GROUP_W
RW_OFF_A = RW_OFF_W + RW_W_RANK
RW_OFF_G = RW_OFF_A + RW_A_RANK
RW_COLS = RW_OFF_G + RW_G_RANK
DN_CONV = 4
DN_CHUNK = 64
DN_QKV = 3 * GROUP_W
DN_COLS = DN_QKV + 2 * N_HEADS + GROUP_W
DA_COLS = 3 * GROUP_W
CV_COLS = 2 * GROUP_W
OFF_CV = DA_COLS
OFF_RW = OFF_CV + CV_COLS
OFF_DN = OFF_RW + RW_COLS
N_IN = OFF_DN + DN_COLS
FFN_HIDDEN = 2816
RMS_EPS = 1e-6
LN_EPS = 1e-5
RW_GN_EPS = 64e-5
NEG_INF = -1e30
LOG2E = math.log2(math.e)

LANES = 128
SUBLANES = 8
DN_PAD = DN_QKV + GROUP_W + LANES
DN_HALO = 8
CV_HALO = 32
N_BR = 2 * N_HEADS
AT_TQ = 512
AT_TK = 512
PG_PAGES = 16
PG_SLOTS = 3
VMEM_LIMIT = 56 * 1024 * 1024

F32 = jnp.float32
BF16 = jnp.bfloat16


def _cparams(*sem):
    return pltpu.CompilerParams(dimension_semantics=sem, vmem_limit_bytes=VMEM_LIMIT)


def _rms(x, g):
    return x * lax.rsqrt(jnp.mean(x * x, axis=-1, keepdims=True) + RMS_EPS) * g


def _row_tile(n, target):
    return target if n % target == 0 else n


def _in_proj_body(x_ref, g_ref, w_ref, q_ref, k_ref, v_ref, cv_ref, rw_ref, dn_ref):
    hn = _rms(x_ref[...], g_ref[...]).astype(BF16)
    off = 0
    for ref in (q_ref, k_ref, v_ref, cv_ref, rw_ref, dn_ref):
        w = ref.shape[-1]
        ref[...] = jnp.dot(hn, w_ref[:, off:off + w], preferred_element_type=F32)
        off += w


def _prep_w_in(w):
    dn = w[:, OFF_DN:]
    dn = jnp.concatenate([dn[:, :DN_QKV], dn[:, DN_QKV + 2 * N_HEADS:], dn[:, DN_QKV:DN_QKV + 2 * N_HEADS],
                          jnp.zeros((w.shape[0], LANES - 2 * N_HEADS), w.dtype)], axis=1)
    return jnp.concatenate([w[:, :OFF_DN], dn], axis=1).astype(BF16)


def in_proj(x, g, w):
    n = x.shape[0]
    tm = _row_tile(n, 512)
    widths = (GROUP_W, GROUP_W, GROUP_W, CV_COLS, RW_COLS, DN_PAD)
    row = lambda wd: pl.BlockSpec((tm, wd), lambda i: (i, 0))
    return pl.pallas_call(
        _in_proj_body,
        grid=(n // tm,),
        in_specs=[row(D_MODEL), pl.BlockSpec((1, D_MODEL), lambda i: (0, 0)),
                  pl.BlockSpec(w.shape, lambda i: (0, 0))],
        out_specs=[row(wd) for wd in widths],
        out_shape=[jax.ShapeDtypeStruct((n, wd), F32) for wd in widths],
        compiler_params=_cparams("parallel"),
        name="in_proj",
    )(x, g.reshape(1, -1), w)


def _mix_ffn_body(x_ref, a_ref, b_ref, c_ref, d_ref, wo_ref, gm_ref, g1_ref, wg_ref, wu_ref, wd_ref, g2_ref, o_ref,
                  xm_ref, hn_ref, acc_ref):
    j = pl.program_id(1)

    @pl.when(j == 0)
    def _():
        mix = None
        for i, ref in enumerate((a_ref, b_ref, c_ref, d_ref)):
            part = jnp.dot(ref[...].astype(BF16), wo_ref[i * GROUP_W:(i + 1) * GROUP_W, :],
                           preferred_element_type=F32)
            mix = part if mix is None else mix + part
        xm = x_ref[...] + _rms(mix, gm_ref[...])
        xm_ref[...] = xm
        hn_ref[...] = _rms(xm, g1_ref[...]).astype(BF16)
        acc_ref[...] = jnp.zeros_like(acc_ref)

    hn = hn_ref[...]
    gate = jnp.dot(hn, wg_ref[...], preferred_element_type=F32)
    up = jnp.dot(hn, wu_ref[...], preferred_element_type=F32)
    f = (gate * jax.nn.sigmoid(gate) * up).astype(BF16)
    acc_ref[...] += jnp.dot(f, wd_ref[...], preferred_element_type=F32)

    @pl.when(j == pl.num_programs(1) - 1)
    def _():
        o_ref[...] = xm_ref[...] + _rms(acc_ref[...], g2_ref[...])


def mix_ffn(x, oa, ob, oc, od, w_out, g_mix, g1, wg, wu, wd, g2):
    n = x.shape[0]
    tm = _row_tile(n, 512)
    th = FFN_HIDDEN // 2
    row = lambda wdt: pl.BlockSpec((tm, wdt), lambda i, j: (i, 0))
    gain = pl.BlockSpec((1, D_MODEL), lambda i, j: (0, 0))
    return pl.pallas_call(
        _mix_ffn_body,
        grid=(n // tm, FFN_HIDDEN // th),
        in_specs=[row(D_MODEL), row(GROUP_W), row(GROUP_W), row(GROUP_W), row(GROUP_W),
                  pl.BlockSpec(w_out.shape, lambda i, j: (0, 0)), gain, gain,
                  pl.BlockSpec((D_MODEL, th), lambda i, j: (0, j)),
                  pl.BlockSpec((D_MODEL, th), lambda i, j: (0, j)),
                  pl.BlockSpec((th, D_MODEL), lambda i, j: (j, 0)), gain],
        out_specs=row(D_MODEL),
        out_shape=jax.ShapeDtypeStruct((n, D_MODEL), F32),
        scratch_shapes=[pltpu.VMEM((tm, D_MODEL), F32), pltpu.VMEM((tm, D_MODEL), BF16),
                        pltpu.VMEM((tm, D_MODEL), F32)],
        compiler_params=_cparams("parallel", "arbitrary"),
        name="mix_ffn",
    )(x, oa, ob, oc, od, w_out, g_mix.reshape(1, -1), g1.reshape(1, -1), wg, wu, wd, g2.reshape(1, -1))


def _mm(a, b):
    return jnp.dot(a.astype(BF16), b.astype(BF16), preferred_element_type=F32)


def _mm_nt(a, b):
    return lax.dot_general(a.astype(BF16), b.astype(BF16), (((1,), (1,)), ((), ())),
                           preferred_element_type=F32)


def _mm_tn(a, b):
    return lax.dot_general(a.astype(BF16), b.astype(BF16), (((0,), (0,)), ((), ())),
                           preferred_element_type=F32)


def _split2(x):
    hi = x.astype(BF16)
    return hi, (x - hi.astype(F32)).astype(BF16)


def _mm_exact_rhs(a01, x):
    a = a01.astype(BF16)
    hi, lo = _split2(x)
    return jnp.dot(a, hi, preferred_element_type=F32) + jnp.dot(a, lo, preferred_element_type=F32)


def _mm_exact_lhs(x, a01):
    a = a01.astype(BF16)
    hi, lo = _split2(x)
    return jnp.dot(hi, a, preferred_element_type=F32) + jnp.dot(lo, a, preferred_element_type=F32)


def _iota2(n, m):
    return lax.broadcasted_iota(jnp.int32, (n, m), 0), lax.broadcasted_iota(jnp.int32, (n, m), 1)


def _same_block(r, c, blk):
    s = int(math.log2(blk))
    return (r >> s) == (c >> s)


def _unit_lower_inverse(mats, to_bd, chunk):
    r, c = _iota2(chunk, N_HEADS * chunk)
    c = c & (chunk - 1)
    eye = (r == c).astype(F32)
    base = min(8, chunk)
    blk = _same_block(r, c, base)
    pw = [jnp.where(blk, a, 0.0) for a in mats]
    d = [eye + p for p in pw]
    pw_bd = [to_bd(p) for p in pw]
    for _ in range(max(int(math.log2(base)) - 1, 0)):
        pw = [_mm(p, pb) for p, pb in zip(pw, pw_bd)]
        pw_bd = [to_bd(p) for p in pw]
        d = [x + _mm(x, pb) for x, pb in zip(d, pw_bd)]
    size = base * 2
    while size <= chunk:
        ring = _same_block(r, c, size) & ~_same_block(r, c, size // 2)
        t = [_mm(jnp.where(ring, a, 0.0), to_bd(x)) for a, x in zip(mats, d)]
        d = [x + _mm(x, to_bd(y)) for x, y in zip(d, t)]
        size *= 2
    return d


def _softplus(z):
    return jnp.maximum(z, 0.0) + jnp.log(1.0 + jnp.exp(-jnp.abs(z)))


def _sigmoid(z):
    return 1.0 / (1.0 + jnp.exp(-z))


def _block_diag_former(chunk, width):
    hw = N_HEADS * chunk
    lr, lc = _iota2(hw, width)
    own = (lr >> int(math.log2(chunk))) == (lc >> int(math.log2(width // N_HEADS)))

    def to_bd(z):
        return jnp.where(own, jnp.concatenate([z] * N_HEADS, axis=0), 0.0).astype(BF16)

    return to_bd


def _head_sum(x, head_ones):
    return _mm_exact_lhs(x, head_ones)


def _row_to_col(row_vals, r, c):
    n = r.shape[0]
    return jnp.sum(jnp.where(r == c, jnp.broadcast_to(row_vals, (n, n)), 0.0), axis=1, keepdims=True)


def _seqs_per_step(b, t):
    want = 4 if t >= HEAD_DIM else 8
    return want if b % want == 0 else 1


def _rwkv_body(p_ref, shift_ref, s0_ref, mu_ref, w0_ref, wup_ref, a0_ref, aup_ref, gup_ref, kk_ref, ka_ref,
               rk_ref, gng_ref, gnb_ref, y_ref, s_out_ref, st_ref, prev_ref, *, chunk, t_valid):
    nb, tb = p_ref.shape[0], p_ref.shape[1]
    ti = pl.program_id(1)
    gr, gcl = _iota2(GROUP_W, GROUP_W)
    head_mask = _same_block(gr, gcl, HEAD_DIM)
    head_ones = head_mask.astype(F32)
    to_bd = _block_diag_former(chunk, GROUP_W)
    to_bd_t = _block_diag_former(chunk, N_HEADS * chunk)

    @pl.when(ti == 0)
    def _():
        for s in range(nb):
            st_ref[s] = jnp.where(head_mask, jnp.concatenate([s0_ref[s]] * N_HEADS, axis=1), 0.0)
        prev_ref[...] = shift_ref[...]

    row = lax.broadcasted_iota(jnp.int32, (nb * tb, 1), 0)
    tr, tc = _iota2(tb, tb)
    cum = (_same_block(tr, tc, chunk) & (tr >= tc)).astype(F32)
    cr, cc = _iota2(chunk, N_HEADS * chunk)
    cc = cc & (chunk - 1)
    low_incl = cr >= cc
    low_strict = cr > cc
    slices = [slice(c0, c0 + chunk) for c0 in range(0, tb, chunk)]

    x = p_ref[...].reshape(nb * tb, RW_COLS)
    prev = pltpu.roll(x, 1, axis=0)
    for s in range(nb):
        prev = jnp.where(row == s * tb, prev_ref[s], prev)
        prev_ref[s] = x[(s + 1) * tb - 1:(s + 1) * tb, :]
    xm = x + (prev - x) * mu_ref[...]
    r = xm[:, :GROUP_W]
    k = xm[:, GROUP_W:2 * GROUP_W]
    v = xm[:, 2 * GROUP_W:RW_OFF_W]
    wd = xm[:, RW_OFF_W:RW_OFF_A]
    ad = xm[:, RW_OFF_A:RW_OFF_G]
    gd = xm[:, RW_OFF_G:RW_COLS]
    w_log = -_softplus(-(w0_ref[...] + _mm(jnp.tanh(wd), wup_ref[...]))) - 0.5
    logw = -jnp.exp(w_log)
    a = _sigmoid(a0_ref[...] + _mm(ad, aup_ref[...]))
    g = _mm(_sigmoid(gd), gup_ref[...])
    kk = k * kk_ref[...]
    kk = kk * lax.rsqrt(_head_sum(kk * kk, head_ones) + 1e-6)
    k2 = k * (1.0 + (a - 1.0) * ka_ref[...])
    bonus = _head_sum(r * k2 * rk_ref[...], head_ones) * v
    if t_valid < tb:
        ok = (row & (tb - 1)) < t_valid
        logw = jnp.where(ok, logw, 0.0)
        kk = jnp.where(ok, kk, 0.0)
        k2 = jnp.where(ok, k2, 0.0)
        v = jnp.where(ok, v, 0.0)
    alpha = -kk
    beta = kk * a
    seqs = []
    for s in range(nb):
        gcum = _mm_exact_rhs(cum, logw[s * tb:(s + 1) * tb])
        pre = []
        for c0 in range(0, tb, chunk):
            gc = gcum[c0:c0 + chunk]
            sl = slice(s * tb + c0, s * tb + c0 + chunk)
            g_last = gc[chunk - 1:chunk, :]
            e_neg = jnp.exp(-gc)
            e_rem = jnp.exp(g_last - gc)
            r_t = r[sl] * jnp.exp(gc)
            a_t = alpha[sl] * jnp.exp(gc - logw[sl])
            k_bd = to_bd(k2[sl] * e_neg)
            b_bd = to_bd(beta[sl] * e_neg)
            v_bd = to_bd(v[sl])
            ar = jnp.concatenate([a_t, r_t], axis=0)
            sc_b = _mm_nt(ar, b_bd)
            sc_k = _mm_nt(ar, k_bd)
            av = _mm(jnp.concatenate([jnp.where(low_strict, sc_k[:chunk], 0.0),
                                      jnp.where(low_incl, sc_k[chunk:], 0.0)], axis=0), v_bd)
            pre.append(dict(
                r_t=r_t, a_bd=to_bd(a_t),
                a_ab=jnp.where(low_strict, sc_b[:chunk], 0.0), a_ak_v=av[:chunk],
                a_rb=jnp.where(low_incl, sc_b[chunk:], 0.0), a_rk_v=av[chunk:],
                kv=_mm_tn(k2[sl] * e_rem, v[sl]), b_hat=beta[sl] * e_rem,
                p_col=_row_to_col(jnp.exp(g_last), gr, gcl)))
        seqs.append((pre, bonus[s * tb:(s + 1) * tb], g[s * tb:(s + 1) * tb]))
    flat = [c for pre, _, _ in seqs for c in pre]
    tinvs = _unit_lower_inverse([c["a_ab"] for c in flat], to_bd_t, chunk)
    for c, tinv in zip(flat, tinvs):
        c["w"] = _mm(tinv, c["a_bd"])
        c["u_loc"] = _mm(tinv, to_bd(c["a_ak_v"]))
    sts = [st_ref[s] for s in range(nb)]
    for ci, sl in enumerate(slices):
        for s in range(nb):
            c, st = seqs[s][0][ci], sts[s]
            ws = _mm(jnp.concatenate([c["w"], c["r_t"]], axis=0), st)
            u = c["u_loc"] + ws[:chunk]
            y = ws[chunk:] + c["a_rk_v"] + _mm(c["a_rb"], to_bd(u))
            sts[s] = st * c["p_col"] + jnp.where(head_mask, c["kv"] + _mm_tn(c["b_hat"], u), 0.0)
            yc = y - _head_sum(y, head_ones) * (1.0 / HEAD_DIM)
            yn = yc * lax.rsqrt(_head_sum(yc * yc, head_ones) * (1.0 / HEAD_DIM) + RW_GN_EPS)
            y_ref[s, sl, :] = yn
    for s in range(nb):
        st_ref[s] = sts[s]
        y_ref[s] = (y_ref[s] * gng_ref[...] + gnb_ref[...] + seqs[s][1]) * seqs[s][2]

    @pl.when(ti == pl.num_programs(1) - 1)
    def _():
        for s in range(nb):
            s_out_ref[s] = jnp.concatenate(
                [sts[s][h * HEAD_DIM:(h + 1) * HEAD_DIM, h * HEAD_DIM:(h + 1) * HEAD_DIM] for h in range(N_HEADS)],
                axis=0)


def rwkv_mix(p, shift_prev, s0, mu, w0, w_up, a0, a_up, g_up, k_k, k_a, r_k, gn_g, gn_b, *, t_valid):
    b, t, _ = p.shape
    chunk = min(HEAD_DIM, t)
    tb = min(256, t)
    nb = _seqs_per_step(b, t)
    row = lambda z: z.reshape(1, -1)
    st0 = jnp.swapaxes(s0, 2, 3).reshape(b, GROUP_W, HEAD_DIM)
    const = lambda shape: pl.BlockSpec(shape, lambda i, j: (0,) * len(shape))
    y, st = pl.pallas_call(
        functools.partial(_rwkv_body, chunk=chunk, t_valid=t_valid),
        grid=(b // nb, t // tb),
        in_specs=[pl.BlockSpec((nb, tb, RW_COLS), lambda i, j: (i, j, 0)),
                  pl.BlockSpec((nb, 1, RW_COLS), lambda i, j: (i, 0, 0)),
                  pl.BlockSpec((nb, GROUP_W, HEAD_DIM), lambda i, j: (i, 0, 0)),
                  const((1, RW_COLS)), const((1, GROUP_W)), const((RW_W_RANK, GROUP_W)),
                  const((1, GROUP_W)), const((RW_A_RANK, GROUP_W)), const((RW_G_RANK, GROUP_W)),
                  const((1, GROUP_W)), const((1, GROUP_W)), const((1, GROUP_W)),
                  const((1, GROUP_W)), const((1, GROUP_W))],
        out_specs=[pl.BlockSpec((nb, tb, GROUP_W), lambda i, j: (i, j, 0)),
                   pl.BlockSpec((nb, GROUP_W, HEAD_DIM), lambda i, j: (i, 0, 0))],
        out_shape=[jax.ShapeDtypeStruct((b, t, GROUP_W), F32),
                   jax.ShapeDtypeStruct((b, GROUP_W, HEAD_DIM), F32)],
        scratch_shapes=[pltpu.VMEM((nb, GROUP_W, GROUP_W), F32), pltpu.VMEM((nb, 1, RW_COLS), F32)],
        compiler_params=_cparams("parallel", "arbitrary"),
        name="rwkv_mix",
    )(p, shift_prev.reshape(b, 1, RW_COLS), st0, row(mu), row(w0), w_up, row(a0), a_up, g_up,
      row(k_k), row(k_a), row(r_k), row(gn_g), row(gn_b))
    return y, jnp.swapaxes(st.reshape(b, N_HEADS, HEAD_DIM, HEAD_DIM), 2, 3)


def _dn_body(p_ref, cprev_ref, s0_ref, cw_ref, alog_ref, dtb_ref, ng_ref, y_ref, s_out_ref, st_ref, hist_ref,
             *, chunk, t_valid):
    nb, tb = p_ref.shape[0], p_ref.shape[1]
    ti = pl.program_id(1)
    gr, gcl = _iota2(GROUP_W, GROUP_W)
    head_mask = _same_block(gr, gcl, HEAD_DIM)
    head_ones = head_mask.astype(F32)
    to_bd = _block_diag_former(chunk, GROUP_W)
    to_bd_t = _block_diag_former(chunk, N_HEADS * chunk)

    @pl.when(ti == 0)
    def _():
        for s in range(nb):
            st_ref[s] = jnp.where(head_mask, jnp.concatenate([s0_ref[s]] * N_HEADS, axis=1), 0.0)
        hist_ref[...] = cprev_ref[...]

    er, ec = _iota2(LANES, GROUP_W)
    lanes_a = (er == (ec >> 6)).astype(F32)
    lanes_b = (er == (ec >> 6) + N_HEADS).astype(F32)
    tr, tc = _iota2(tb, tb)
    cum = (_same_block(tr, tc, chunk) & (tr >= tc)).astype(F32)
    cr, cc = _iota2(chunk, N_HEADS * chunk)
    cc = cc & (chunk - 1)
    low_incl = cr >= cc
    low_strict = cr > cc
    eye_c = cr == cc
    slices = [slice(c0, c0 + chunk) for c0 in range(0, tb, chunk)]

    convs = []
    for s in range(nb):
        ext = jnp.concatenate([hist_ref[s], p_ref[s, :, :DN_QKV]], axis=0)
        hist_ref[s] = ext[tb:tb + DN_HALO, :]
        conv = None
        for j in range(DN_CONV):
            lo = DN_HALO - (DN_CONV - 1) + j
            term = ext[lo:lo + tb, :] * cw_ref[j:j + 1, :]
            conv = term if conv is None else conv + term
        convs.append(conv)
    conv = jnp.concatenate(convs, axis=0)
    x = p_ref[...].reshape(nb * tb, DN_PAD)
    qkv = conv * _sigmoid(conv)
    q = qkv[:, :GROUP_W]
    k = qkv[:, GROUP_W:2 * GROUP_W]
    v = qkv[:, 2 * GROUP_W:]
    q = q * lax.rsqrt(_head_sum(q * q, head_ones) + 1e-6) * (HEAD_DIM ** -0.5)
    k = k * lax.rsqrt(_head_sum(k * k, head_ones) + 1e-6)
    z = x[:, DN_QKV:DN_QKV + GROUP_W]
    ab = x[:, DN_QKV + GROUP_W:]
    g_w = _mm_exact_lhs(-jnp.exp(alog_ref[...]) * _softplus(ab + dtb_ref[...]), lanes_a)
    beta = _mm_exact_lhs(_sigmoid(ab), lanes_b)
    if t_valid < tb:
        ok = (lax.broadcasted_iota(jnp.int32, (nb * tb, 1), 0) & (tb - 1)) < t_valid
        g_w = jnp.where(ok, g_w, 0.0)
        beta = jnp.where(ok, beta, 0.0)
        k = jnp.where(ok, k, 0.0)
        v = jnp.where(ok, v, 0.0)
    kb = k * beta
    vb = v * beta
    seqs = []
    for s in range(nb):
        gcum = _mm_exact_rhs(cum, g_w[s * tb:(s + 1) * tb])
        pre = []
        for c0 in range(0, tb, chunk):
            gc = gcum[c0:c0 + chunk]
            sl = slice(s * tb + c0, s * tb + c0 + chunk)
            g_last = gc[chunk - 1:chunk, :]
            gi = gc if chunk == HEAD_DIM else jnp.concatenate(
                [gc[:, h * HEAD_DIM:h * HEAD_DIM + chunk] for h in range(N_HEADS)], axis=1)
            gj = jnp.sum(jnp.where(eye_c, gi, 0.0), axis=0, keepdims=True)
            decay = jnp.exp(jnp.where(low_incl, gi - gj, NEG_INF))
            k_bd = to_bd(k[sl])
            sc = _mm_nt(jnp.concatenate([kb[sl], q[sl]], axis=0), k_bd)
            pre.append(dict(
                m=jnp.where(low_strict, sc[:chunk] * decay, 0.0),
                qk=jnp.where(low_incl, sc[chunk:] * decay, 0.0),
                kbg_bd=to_bd(kb[sl] * jnp.exp(gc)), q_g=q[sl] * jnp.exp(gc), vb_bd=to_bd(vb[sl]),
                k_rem=k[sl] * jnp.exp(g_last - gc), p_col=_row_to_col(jnp.exp(g_last), gr, gcl)))
        seqs.append((pre, z[s * tb:(s + 1) * tb]))
    flat = [c for pre, _ in seqs for c in pre]
    tinvs = _unit_lower_inverse([-c["m"] for c in flat], to_bd_t, chunk)
    for c, tinv in zip(flat, tinvs):
        c["w"] = _mm(tinv, c["kbg_bd"])
        c["u_loc"] = _mm(tinv, c["vb_bd"])
    sts = [st_ref[s] for s in range(nb)]
    for ci, sl in enumerate(slices):
        for s in range(nb):
            c, st = seqs[s][0][ci], sts[s]
            ws = _mm(jnp.concatenate([c["w"], c["q_g"]], axis=0), st)
            v_new = c["u_loc"] - ws[:chunk]
            o = ws[chunk:] + _mm(c["qk"], to_bd(v_new))
            sts[s] = st * c["p_col"] + jnp.where(head_mask, _mm_tn(c["k_rem"], v_new), 0.0)
            y_ref[s, sl, :] = o * lax.rsqrt(_head_sum(o * o, head_ones) * (1.0 / HEAD_DIM) + RMS_EPS) * ng_ref[...]
    for s in range(nb):
        st_ref[s] = sts[s]
        z = seqs[s][1]
        y_ref[s] = y_ref[s] * (z * _sigmoid(z))

    @pl.when(ti == pl.num_programs(1) - 1)
    def _():
        for s in range(nb):
            s_out_ref[s] = jnp.concatenate(
                [sts[s][h * HEAD_DIM:(h + 1) * HEAD_DIM, h * HEAD_DIM:(h + 1) * HEAD_DIM] for h in range(N_HEADS)],
                axis=0)


def dn_mix(p, conv_prev, s0, conv_w, a_log, dt_bias, norm_g, *, t_valid):
    b, t, _ = p.shape
    chunk = min(DN_CHUNK, t)
    tb = min(256, t)
    nb = _seqs_per_step(b, t)
    hist = jnp.pad(conv_prev, ((0, 0), (DN_HALO - (DN_CONV - 1), 0), (0, 0)))
    lane_pad = lambda z: jnp.pad(z, (0, LANES - z.shape[0])).reshape(1, LANES)
    const = lambda shape: pl.BlockSpec(shape, lambda i, j: (0,) * len(shape))
    y, st = pl.pallas_call(
        functools.partial(_dn_body, chunk=chunk, t_valid=t_valid),
        grid=(b // nb, t // tb),
        in_specs=[pl.BlockSpec((nb, tb, DN_PAD), lambda i, j: (i, j, 0)),
                  pl.BlockSpec((nb, DN_HALO, DN_QKV), lambda i, j: (i, 0, 0)),
                  pl.BlockSpec((nb, GROUP_W, HEAD_DIM), lambda i, j: (i, 0, 0)),
                  const((DN_CONV, DN_QKV)), const((1, LANES)), const((1, LANES)), const((1, GROUP_W))],
        out_specs=[pl.BlockSpec((nb, tb, GROUP_W), lambda i, j: (i, j, 0)),
                   pl.BlockSpec((nb, GROUP_W, HEAD_DIM), lambda i, j: (i, 0, 0))],
        out_shape=[jax.ShapeDtypeStruct((b, t, GROUP_W), F32),
                   jax.ShapeDtypeStruct((b, GROUP_W, HEAD_DIM), F32)],
        scratch_shapes=[pltpu.VMEM((nb, GROUP_W, GROUP_W), F32), pltpu.VMEM((nb, DN_HALO, DN_QKV), F32)],
        compiler_params=_cparams("parallel", "arbitrary"),
        name="dn_mix",
    )(p, hist, s0.reshape(b, GROUP_W, HEAD_DIM), conv_w, lane_pad(a_log), lane_pad(dt_bias),
      jnp.tile(norm_g, N_HEADS).reshape(1, GROUP_W))
    return y, st.reshape(b, N_HEADS, HEAD_DIM, HEAD_DIM)


def _diff_lambda(lq1_ref, lk1_ref, lq2_ref, lk2_ref, lam_init):
    return (jnp.exp(jnp.sum(lq1_ref[...] * lk1_ref[...], axis=-1, keepdims=True))
            - jnp.exp(jnp.sum(lq2_ref[...] * lk2_ref[...], axis=-1, keepdims=True)) + lam_init)


def _head_rms(o, g, scale):
    gr, gcl = _iota2(GROUP_W, GROUP_W)
    head_ones = _same_block(gr, gcl, HEAD_DIM).astype(F32)
    ms = _mm_exact_lhs(o * o, head_ones) * (1.0 / HEAD_DIM)
    return o * lax.rsqrt(ms + RMS_EPS) * g * scale


def _attn_body(q_ref, k_ref, v_ref, lq1_ref, lk1_ref, lq2_ref, lk2_ref, ng_ref, o_ref,
               kb_ref, vt_ref, bias_ref, biasd_ref, qs_ref, *stats, lam_init):
    m_refs, acc_refs = stats[:N_BR], stats[N_BR:]
    b = pl.program_id(0)
    i = pl.program_id(1)
    tq, tk = AT_TQ, AT_TK
    t = k_ref.shape[1]
    n_sub = tq // tk
    slope = [LOG2E * 2.0 ** (-2 * (h + 1)) for h in range(N_HEADS)]

    @pl.when((b == 0) & (i == 0))
    def _():
        jj, ii = _iota2(tk, tq)
        rel = (ii - jj).astype(F32)
        for h in range(N_HEADS):
            bias_ref[:, h * tq:(h + 1) * tq] = -slope[h] * rel
            for d in range(n_sub):
                rd = rel - float(d * tk)
                biasd_ref[d, :, h * tq:(h + 1) * tq] = jnp.where(rd >= 0, -slope[h] * rel, NEG_INF)

    @pl.when(i == 0)
    def _():
        kb_ref[...] = k_ref[0].astype(BF16)
        for j in range(t // tk):
            vt_ref[:, j * tk:(j + 1) * tk] = v_ref[0, j * tk:(j + 1) * tk, :].T.astype(BF16)

    q = q_ref[0] * (DA_QK ** -0.5 * LOG2E)
    lane = lax.broadcasted_iota(jnp.int32, (1, GROUP_W), 1)
    for c in range(N_BR):
        own = (lane >> 5) == c
        qs_ref[c * tq:(c + 1) * tq, :] = jnp.where(own, q, 0.0).astype(BF16)
    for c in range(N_BR):
        m_refs[c][...] = jnp.full_like(m_refs[c], NEG_INF)
        acc_refs[c][...] = jnp.zeros_like(acc_refs[c])

    def tile(j, b_ref):
        start = pl.multiple_of(j * tk, tk)
        kt = kb_ref[pl.ds(start, tk), :]
        dist = (tq * i - tk * j).astype(F32)
        ss = [_mm_nt(kt, qs_ref[c * tq:(c + 1) * tq, :]) for c in range(N_BR)]
        ps, alphas = [], []
        for c in range(N_BR):
            h = c >> 1
            s = ss[c] + b_ref[:, h * tq:(h + 1) * tq]
            ct = -slope[h] * dist
            m_old = m_refs[c][...]
            m_new = jnp.maximum(m_old, jnp.max(s, axis=0, keepdims=True) + ct)
            alpha = jnp.exp2(m_old - m_new)
            p = jnp.exp2(s - (m_new - ct))
            m_refs[c][...] = m_new
            ps.append(p.astype(BF16))
            alphas.append(alpha)
        ones = jnp.ones((2 * SUBLANES, tk), BF16)
        for c in range(N_BR):
            h = c >> 1
            lhs = jnp.concatenate([vt_ref[h * HEAD_DIM:(h + 1) * HEAD_DIM, pl.ds(start, tk)], ones], axis=0)
            pv = jnp.dot(lhs, ps[c], preferred_element_type=F32)
            acc_refs[c][...] = acc_refs[c][...] * alphas[c] + pv

    def body(j, carry):
        tile(j, bias_ref)
        return carry

    lax.fori_loop(0, i * n_sub, body, 0)
    for d in range(n_sub):
        tile(i * n_sub + d, biasd_ref.at[d])

    lam = _diff_lambda(lq1_ref, lk1_ref, lq2_ref, lk2_ref, lam_init)
    norm = [acc_refs[c][:HEAD_DIM, :] * (1.0 / acc_refs[c][HEAD_DIM:HEAD_DIM + 1, :]) for c in range(N_BR)]
    o = jnp.concatenate([norm[2 * h] - lam * norm[2 * h + 1] for h in range(N_HEADS)], axis=0).T
    o_ref[0] = _head_rms(o, ng_ref[...], 1.0 - lam_init)


def attn_prompt(q, k, v, lq1, lk1, lq2, lk2, norm_g, lam_init):
    b, t, _ = q.shape
    tq, tk = AT_TQ, AT_TK
    assert t % tq == 0 and tq % tk == 0
    row = lambda z: z.reshape(1, -1)
    const = lambda shape: pl.BlockSpec(shape, lambda i, j: (0,) * len(shape))
    return pl.pallas_call(
        functools.partial(_attn_body, lam_init=lam_init),
        grid=(b, t // tq),
        in_specs=[pl.BlockSpec((1, tq, GROUP_W), lambda i, j: (i, j, 0)),
                  pl.BlockSpec((1, t, GROUP_W), lambda i, j: (i, 0, 0)),
                  pl.BlockSpec((1, t, GROUP_W), lambda i, j: (i, 0, 0)),
                  const((1, DA_QK)), const((1, DA_QK)), const((1, DA_QK)), const((1, DA_QK)),
                  const((1, GROUP_W))],
        out_specs=pl.BlockSpec((1, tq, GROUP_W), lambda i, j: (i, j, 0)),
        out_shape=jax.ShapeDtypeStruct((b, t, GROUP_W), F32),
        scratch_shapes=[pltpu.VMEM((t, GROUP_W), BF16), pltpu.VMEM((GROUP_W, t), BF16),
                        pltpu.VMEM((tk, N_HEADS * tq), F32), pltpu.VMEM((tq // tk, tk, N_HEADS * tq), F32),
                        pltpu.VMEM((N_BR * tq, GROUP_W), BF16),
                        *([pltpu.VMEM((1, tq), F32)] * N_BR),
                        *([pltpu.VMEM((HEAD_DIM + 2 * SUBLANES, tq), F32)] * N_BR)],
        compiler_params=_cparams("arbitrary", "arbitrary"),
        name="attn_prompt",
    )(q, k, v, row(lq1), row(lk1), row(lq2), row(lk2), row(jnp.tile(norm_g, N_HEADS)))


def _paged_body(pt_ref, q_ref, kn_ref, vn_ref, lq1_ref, lk1_ref, lq2_ref, lk2_ref, ng_ref, ck_hbm, cv_hbm, o_ref,
                kbuf, vbuf, sem, qs_ref, base_ref, m_ref, l_ref, acc_ref, *, layer, lam_init, page, n_pages):
    n_seq = q_ref.shape[0]
    n_groups = n_pages // PG_PAGES
    n_steps = n_seq * n_groups
    span = PG_PAGES * page
    past = n_pages * page
    rowc = lax.broadcasted_iota(jnp.int32, (N_BR, 1), 0)
    slope = jnp.exp2(-2.0 * ((rowc >> 1) + 1).astype(F32))
    lane = lax.broadcasted_iota(jnp.int32, (N_BR, GROUP_W), 1)
    rown = lax.broadcasted_iota(jnp.int32, (N_BR, GROUP_W), 0)
    base_ref[...] = slope * lax.broadcasted_iota(jnp.int32, (N_BR, span), 1).astype(F32)

    def page_copies(step, slot):
        seq, grp = step // n_groups, step % n_groups
        for r in range(PG_PAGES):
            pid = pt_ref[seq, grp * PG_PAGES + r]
            yield pltpu.make_async_copy(ck_hbm.at[layer, pid], kbuf.at[slot, r], sem.at[slot])
            yield pltpu.make_async_copy(cv_hbm.at[layer, pid], vbuf.at[slot, r], sem.at[slot])

    def start(step, slot):
        for cp in page_copies(step, slot):
            cp.start()

    def wait(step, slot):
        for cp in page_copies(step, slot):
            cp.wait()

    for ahead in range(PG_SLOTS - 1):
        start(ahead, ahead)

    def body(step, carry):
        slot = lax.rem(step, PG_SLOTS)
        nxt = step + (PG_SLOTS - 1)

        @pl.when(nxt < n_steps)
        def _():
            start(nxt, lax.rem(nxt, PG_SLOTS))

        wait(step, slot)
        seq, grp = step // n_groups, step % n_groups

        @pl.when(grp == 0)
        def _():
            q = q_ref[pl.ds(seq, 1), :] * (DA_QK ** -0.5)
            qs_ref[...] = jnp.where((lane >> 5) == rown, jnp.broadcast_to(q, (N_BR, GROUP_W)), 0.0)
            m_ref[...] = jnp.full_like(m_ref, NEG_INF)
            l_ref[...] = jnp.zeros_like(l_ref)
            acc_ref[...] = jnp.zeros_like(acc_ref)

        qs = qs_ref[...]
        qb = qs.astype(BF16)
        s = jnp.concatenate(
            [jnp.dot(qb, kbuf[slot, r].astype(BF16), preferred_element_type=F32) for r in range(PG_PAGES)],
            axis=1)
        off = -slope * jnp.asarray(past - grp * span, F32)
        s = s + base_ref[...]
        m_old = m_ref[...]
        m_new = jnp.maximum(m_old, jnp.max(s, axis=1, keepdims=True) + off)
        alpha = jnp.exp(m_old - m_new)
        p = jnp.exp(s - (m_new - off))
        l_new = alpha * l_ref[...] + jnp.sum(p, axis=1, keepdims=True)
        pb = p.astype(BF16)
        pv = None
        for r in range(PG_PAGES):
            d = _mm_nt(pb[:, r * page:(r + 1) * page], vbuf[slot, r])
            pv = d if pv is None else pv + d
        acc = acc_ref[...] * alpha + pv
        m_ref[...] = m_new
        l_ref[...] = l_new
        acc_ref[...] = acc

        @pl.when(grp == n_groups - 1)
        def _():
            s_self = jnp.sum(qs * kn_ref[pl.ds(seq, 1), :], axis=1, keepdims=True)
            m_fin = jnp.maximum(m_new, s_self)
            a_fin = jnp.exp(m_new - m_fin)
            p_self = jnp.exp(s_self - m_fin)
            l_fin = a_fin * l_new + p_self
            out = (acc * a_fin + p_self * vn_ref[pl.ds(seq, 1), :]) / l_fin
            lam = _diff_lambda(lq1_ref, lk1_ref, lq2_ref, lk2_ref, lam_init)
            coef = jnp.where((rown & 1) == 0, 1.0, -lam)
            o = jnp.sum(jnp.where((lane >> 6) == (rown >> 1), out * coef, 0.0), axis=0, keepdims=True)
            o8 = jnp.broadcast_to(o, (SUBLANES, GROUP_W))
            o_ref[pl.ds(seq, 1), :] = _head_rms(o8, ng_ref[...], 1.0 - lam_init)[0:1]

        return carry

    lax.fori_loop(0, n_steps, body, 0)


def attn_sample(q, k_new, v_new, cache_k, cache_v, page_table, layer, lq1, lk1, lq2, lk2, norm_g, lam_init):
    b = q.shape[0]
    depth, n_pool, page = cache_k.shape[:3]
    n_pages = page_table.shape[1]
    assert n_pages % PG_PAGES == 0 and b * (n_pages // PG_PAGES) >= PG_SLOTS
    as_pages = lambda c: jnp.transpose(c, (0, 1, 3, 4, 2)).reshape(depth, n_pool, GROUP_W, page)
    row = lambda z: z.reshape(1, -1)
    full = lambda shape: pl.BlockSpec(shape, lambda i, pt: (0,) * len(shape))
    hbm = pl.BlockSpec(memory_space=pl.ANY)
    span = PG_PAGES * page
    grid_spec = pltpu.PrefetchScalarGridSpec(
        num_scalar_prefetch=1,
        grid=(1,),
        in_specs=[full((b, GROUP_W)), full((b, GROUP_W)), full((b, GROUP_W)),
                  full((1, DA_QK)), full((1, DA_QK)), full((1, DA_QK)), full((1, DA_QK)), full((1, GROUP_W)),
                  hbm, hbm],
        out_specs=full((b, GROUP_W)),
        scratch_shapes=[pltpu.VMEM((PG_SLOTS, PG_PAGES, GROUP_W, page), F32),
                        pltpu.VMEM((PG_SLOTS, PG_PAGES, GROUP_W, page), F32),
                        pltpu.SemaphoreType.DMA((PG_SLOTS,)),
                        pltpu.VMEM((N_BR, GROUP_W), F32), pltpu.VMEM((N_BR, span), F32),
                        pltpu.VMEM((N_BR, 1), F32), pltpu.VMEM((N_BR, 1), F32), pltpu.VMEM((N_BR, GROUP_W), F32)])
    return pl.pallas_call(
        functools.partial(_paged_body, layer=layer, lam_init=lam_init, page=page, n_pages=n_pages),
        grid_spec=grid_spec,
        out_shape=jax.ShapeDtypeStruct((b, GROUP_W), F32),
        compiler_params=_cparams("arbitrary"),
        name="attn_sample",
    )(page_table, q, k_new, v_new, row(lq1), row(lk1), row(lq2), row(lk2),
      row(jnp.tile(norm_g, N_HEADS)), as_pages(cache_k), as_pages(cache_v))


def _conv_body(p_ref, buf_ref, w_ref, b_ref, lg_ref, lb_ref, y_ref, tail_ref, ext_ref, sh_ref, *, t_valid):
    tb = p_ref.shape[1]
    ti = pl.program_id(1)

    @pl.when(ti == 0)
    def _():
        ext_ref[0:CV_HALO, :] = buf_ref[0]

    x = p_ref[0]
    ext_ref[CV_HALO:CV_HALO + tb, :] = x[:, :GROUP_W] * _sigmoid(x[:, GROUP_W:])
    span = tb + CV_HALO - SUBLANES
    for ph in range(1, SUBLANES):
        sh_ref[ph - 1] = ext_ref[ph:ph + span, :]
    acc = None
    for j in range(CONV_WIDTH):
        lo = CV_HALO - (CONV_WIDTH - 1) + j
        ph, base = lo % SUBLANES, lo - lo % SUBLANES
        rows = ext_ref[base:base + tb, :] if ph == 0 else sh_ref[ph - 1, base:base + tb, :]
        term = rows * w_ref[j:j + 1, :]
        acc = term if acc is None else acc + term
    hist_new = ext_ref[t_valid:t_valid + CV_HALO, :]
    ext_ref[0:CV_HALO, :] = hist_new
    h = acc + b_ref[...]
    hc = h - jnp.mean(h, axis=-1, keepdims=True)
    var = jnp.mean(hc * hc, axis=-1, keepdims=True)
    hn = hc * lax.rsqrt(var + LN_EPS) * lg_ref[...] + lb_ref[...]
    y_ref[0] = hn * _sigmoid(hn)

    @pl.when(ti == pl.num_programs(1) - 1)
    def _():
        tail_ref[0] = hist_new


def conv_mix(p, buf, conv_w, conv_b, ln_g, ln_b, *, t_valid):
    b, t, _ = p.shape
    tb = min(512, t)
    assert t == tb or t_valid == t
    keep = CONV_WIDTH - 1
    hist = jnp.pad(buf, ((0, 0), (CV_HALO - keep, 0), (0, 0)))
    row = lambda z: z.reshape(1, -1)
    const = lambda shape: pl.BlockSpec(shape, lambda i, j: (0,) * len(shape))
    y, tail = pl.pallas_call(
        functools.partial(_conv_body, t_valid=min(t_valid, tb)),
        grid=(b, t // tb),
        in_specs=[pl.BlockSpec((1, tb, CV_COLS), lambda i, j: (i, j, 0)),
                  pl.BlockSpec((1, CV_HALO, GROUP_W), lambda i, j: (i, 0, 0)),
                  const((CONV_WIDTH, GROUP_W)), const((1, GROUP_W)), const((1, GROUP_W)), const((1, GROUP_W))],
        out_specs=[pl.BlockSpec((1, tb, GROUP_W), lambda i, j: (i, j, 0)),
                   pl.BlockSpec((1, CV_HALO, GROUP_W), lambda i, j: (i, 0, 0))],
        out_shape=[jax.ShapeDtypeStruct((b, t, GROUP_W), F32),
                   jax.ShapeDtypeStruct((b, CV_HALO, GROUP_W), F32)],
        scratch_shapes=[pltpu.VMEM((CV_HALO + tb, GROUP_W), F32),
                        pltpu.VMEM((SUBLANES - 1, CV_HALO + tb - SUBLANES, GROUP_W), F32)],
        compiler_params=_cparams("parallel", "arbitrary"),
        name="conv_mix",
    )(p, hist, conv_w, row(conv_b), row(ln_g), row(ln_b))
    return y, tail[:, CV_HALO - keep:]


def trunk_layer(l, x, paged, cv_buf, rw_shift, rw_S, dn_buf, dn_S, P, W):
    b, t, _ = x.shape
    n = b * t
    x2 = x.reshape(n, D_MODEL)
    q, k, v, p_cv, p_rw, p_dn = in_proj(x2, P['g_pre_mix'][l], W['w_in'][l])
    lam_init = 0.8 - 0.6 * math.exp(-0.3 * l)
    lam_args = (P['da_lq1'][l], P['da_lk1'][l], P['da_lq2'][l], P['da_lk2'][l], P['da_norm_g'][l], lam_init)
    if paged is None:
        r3 = lambda z: z.reshape(b, t, GROUP_W)
        o_a = attn_prompt(r3(q), r3(k), r3(v), *lam_args).reshape(n, GROUP_W)
    else:
        assert t == 1
        o_a = attn_sample(q, k, v, *paged, l, *lam_args)
    hs = lambda z: z.reshape(b, t, N_HEADS, HEAD_DIM)
    k, v = hs(k), hs(v)
    t_pad = -(-t // SUBLANES) * SUBLANES
    pad_t = lambda z: z if t_pad == t else jnp.pad(z, ((0, 0), (0, t_pad - t), (0, 0)))
    o_b, cv_new = conv_mix(pad_t(p_cv.reshape(b, t, CV_COLS)), cv_buf, P['cv_w'][l], P['cv_b'][l],
                           P['cv_ln_g'][l], P['cv_ln_b'][l], t_valid=t)
    o_b = o_b[:, :t]
    p_rw = p_rw.reshape(b, t, RW_COLS)
    o_c, rw_S_new = rwkv_mix(
        pad_t(p_rw), rw_shift, rw_S, P['rw_mu'][l], P['rw_w0'][l], P['rw_w_up'][l],
        P['rw_a0'][l], P['rw_a_up'][l], P['rw_g_up'][l], P['rw_k_k'][l], P['rw_k_a'][l],
        P['rw_r_k'][l].reshape(-1), P['rw_gn_g'][l], P['rw_gn_b'][l], t_valid=t)
    o_c = o_c[:, :t]
    rw_shift_new = p_rw[:, -1]
    p_dn = p_dn.reshape(b, t, DN_PAD)
    o_d, dn_S_new = dn_mix(pad_t(p_dn), dn_buf, dn_S, P['dn_conv_w'][l], P['dn_A_log'][l],
                           P['dn_dt_bias'][l], P['dn_norm_g'][l], t_valid=t)
    o_d = o_d[:, :t]
    keep = DN_CONV - 1
    dn_buf_new = (p_dn[:, t - keep:, :DN_QKV] if t >= keep else
                  jnp.concatenate([dn_buf, p_dn[..., :DN_QKV]], axis=1)[:, -keep:])
    x2 = mix_ffn(x2, o_a, o_b.reshape(n, GROUP_W), o_c.reshape(n, GROUP_W), o_d.reshape(n, GROUP_W),
                 W['w_out'][l], P['g_post_mix'][l], P['g_pre_ffn'][l], W['ffn_w_gate'][l], W['ffn_w_up'][l],
                 W['ffn_w_down'][l], P['g_post_ffn'][l])
    return x2.reshape(b, t, D_MODEL), (k, v, cv_new, rw_shift_new, rw_S_new, dn_buf_new, dn_S_new)


def kernel(x_prompt, x_sample, cache_k, cache_v, page_table, state_conv, state_rw_shift,
           state_rw_wkv, state_dn_conv, state_dn_ssm, g_pre_mix, g_post_mix, g_pre_ffn,
           g_post_ffn, w_in, w_out, da_lq1, da_lk1, da_lq2, da_lk2, da_norm_g, cv_w, cv_b,
           cv_ln_g, cv_ln_b, rw_mu, rw_w0, rw_w_up, rw_a0, rw_a_up, rw_g_up, rw_k_k, rw_k_a,
           rw_r_k, rw_gn_g, rw_gn_b, dn_conv_w, dn_A_log, dn_dt_bias, dn_norm_g,
           ffn_w_gate, ffn_w_up, ffn_w_down):
    P = dict(g_pre_mix=g_pre_mix, g_post_mix=g_post_mix, g_pre_ffn=g_pre_ffn, g_post_ffn=g_post_ffn,
             da_lq1=da_lq1, da_lk1=da_lk1, da_lq2=da_lq2, da_lk2=da_lk2,
             da_norm_g=da_norm_g, cv_w=cv_w, cv_b=cv_b, cv_ln_g=cv_ln_g, cv_ln_b=cv_ln_b,
             rw_mu=rw_mu, rw_w0=rw_w0, rw_w_up=rw_w_up, rw_a0=rw_a0, rw_a_up=rw_a_up,
             rw_g_up=rw_g_up, rw_k_k=rw_k_k, rw_k_a=rw_k_a, rw_r_k=rw_r_k, rw_gn_g=rw_gn_g,
             rw_gn_b=rw_gn_b, dn_conv_w=dn_conv_w, dn_A_log=dn_A_log, dn_dt_bias=dn_dt_bias,
             dn_norm_g=dn_norm_g)
    depth = w_in.shape[0]
    W = dict(w_in=[_prep_w_in(w_in[l]) for l in range(depth)],
             w_out=[w_out[l].astype(BF16) for l in range(depth)],
             ffn_w_gate=[ffn_w_gate[l].astype(BF16) for l in range(depth)],
             ffn_w_up=[ffn_w_up[l].astype(BF16) for l in range(depth)],
             ffn_w_down=[ffn_w_down[l].astype(BF16) for l in range(depth)])
    bp = x_prompt.shape[0]
    dtp = x_prompt.dtype
    yp = x_prompt
    ys = x_sample
    outs_p = []
    outs_s = []
    for l in range(depth):
        yp, st_p = trunk_layer(
            l, yp, None,
            jnp.zeros((bp, CONV_WIDTH - 1, GROUP_W), dtp), jnp.zeros((bp, RW_COLS), dtp),
            jnp.zeros((bp, N_HEADS, HEAD_DIM, HEAD_DIM), jnp.float32),
            jnp.zeros((bp, DN_CONV - 1, DN_QKV), dtp),
            jnp.zeros((bp, N_HEADS, HEAD_DIM, HEAD_DIM), jnp.float32), P, W)
        ys, st_s = trunk_layer(l, ys, (cache_k, cache_v, page_table), state_conv[l], state_rw_shift[l],
                               state_rw_wkv[l], state_dn_conv[l], state_dn_ssm[l], P, W)
        outs_p.append(st_p)
        outs_s.append(st_s)
    stk = lambda outs, i: jnp.stack([o[i] for o in outs])
    return (yp, ys, stk(outs_p, 0), stk(outs_p, 1), stk(outs_s, 0), stk(outs_s, 1),
            stk(outs_p, 2), stk(outs_s, 2), stk(outs_p, 3), stk(outs_s, 3),
            stk(outs_p, 4), stk(outs_s, 4), stk(outs_p, 5), stk(outs_s, 5),
            stk(outs_p, 6), stk(outs_s, 6))
```

```python
import functools
import math

import jax
import jax.numpy as jnp
from jax import lax
from jax.experimental import pallas as pl
from jax.experimental.pallas import tpu as pltpu

D_MODEL = 1024
HEAD_DIM = 64
N_HEADS = 4
GROUP_W = 256
DA_QK = HEAD_DIM // 2
CONV_WIDTH = 31
RW_W_RANK = 64
RW_A_RANK = 64
RW_G_RANK = 128
RW_OFF_W = 3 * GROUP_W
RW_OFF_A = RW_OFF_W + RW_W_RANK
RW_OFF_G = RW_OFF_A + RW_A_RANK
RW_COLS = RW_OFF_G + RW_G_RANK
DN_CONV = 4
DN_CHUNK = 64
DN_QKV = 3 * GROUP_W
DN_COLS = DN_QKV + 2 * N_HEADS + GROUP_W
DA_COLS = 3 * GROUP_W
CV_COLS = 2 * GROUP_W
OFF_CV = DA_COLS
OFF_RW = OFF_CV + CV_COLS
OFF_DN = OFF_RW + RW_COLS
N_IN = OFF_DN + DN_COLS
FFN_HIDDEN = 2816
RMS_EPS = 1e-6
LN_EPS = 1e-5
RW_GN_EPS = 64e-5
NEG_INF = -1e30
LOG2E = math.log2(math.e)

LANES = 128
SUBLANES = 8
DN_PAD = DN_QKV + GROUP_W + LANES
DN_HALO = 8
CV_HALO = 32
N_BR = 2 * N_HEADS
AT_TQ = 512
AT_TK = 512
PG_PAGES = 16
PG_SLOTS = 3
VMEM_LIMIT = 56 * 1024 * 1024

F32 = jnp.float32
BF16 = jnp.bfloat16


def _cparams(*sem):
    return pltpu.CompilerParams(dimension_semantics=sem, vmem_limit_bytes=VMEM_LIMIT)


def _rms(x, g):
    return x * lax.rsqrt(jnp.mean(x * x, axis=-1, keepdims=True) + RMS_EPS) * g


def _row_tile(n, target):
    return target if n % target == 0 else n


def _in_proj_body(x_ref, g_ref, w_ref, q_ref, k_ref, v_ref, cv_ref, rw_ref, dn_ref):
    hn = _rms(x_ref[...], g_ref[...]).astype(BF16)
    off = 0
    for ref in (q_ref, k_ref, v_ref, cv_ref, rw_ref, dn_ref):
        w = ref.shape[-1]
        ref[...] = jnp.dot(hn, w_ref[:, off:off + w], preferred_element_type=F32)
        off += w


def _prep_w_in(w):
    dn = w[:, OFF_DN:]
    dn = jnp.concatenate([dn[:, :DN_QKV], dn[:, DN_QKV + 2 * N_HEADS:], dn[:, DN_QKV:DN_QKV + 2 * N_HEADS],
                          jnp.zeros((w.shape[0], LANES - 2 * N_HEADS), w.dtype)], axis=1)
    return jnp.concatenate([w[:, :OFF_DN], dn], axis=1).astype(BF16)


def in_proj(x, g, w):
    n = x.shape[0]
    tm = _row_tile(n, 512)
    widths = (GROUP_W, GROUP_W, GROUP_W, CV_COLS, RW_COLS, DN_PAD)
    row = lambda wd: pl.BlockSpec((tm, wd), lambda i: (i, 0))
    return pl.pallas_call(
        _in_proj_body,
        grid=(n // tm,),
        in_specs=[row(D_MODEL), pl.BlockSpec((1, D_MODEL), lambda i: (0, 0)),
                  pl.BlockSpec(w.shape, lambda i: (0, 0))],
        out_specs=[row(wd) for wd in widths],
        out_shape=[jax.ShapeDtypeStruct((n, wd), F32) for wd in widths],
        compiler_params=_cparams("parallel"),
        name="in_proj",
    )(x, g.reshape(1, -1), w)


def _mix_ffn_body(x_ref, a_ref, b_ref, c_ref, d_ref, wo_ref, gm_ref, g1_ref, wg_ref, wu_ref, wd_ref, g2_ref, o_ref,
                  xm_ref, hn_ref, acc_ref):
    j = pl.program_id(1)

    @pl.when(j == 0)
    def _():
        mix = None
        for i, ref in enumerate((a_ref, b_ref, c_ref, d_ref)):
            part = jnp.dot(ref[...].astype(BF16), wo_ref[i * GROUP_W:(i + 1) * GROUP_W, :],
                           preferred_element_type=F32)
            mix = part if mix is None else mix + part
        xm = x_ref[...] + _rms(mix, gm_ref[...])
        xm_ref[...] = xm
        hn_ref[...] = _rms(xm, g1_ref[...]).astype(BF16)
        acc_ref[...] = jnp.zeros_like(acc_ref)

    hn = hn_ref[...]
    gate = jnp.dot(hn, wg_ref[...], preferred_element_type=F32)
    up = jnp.dot(hn, wu_ref[...], preferred_element_type=F32)
    f = (gate * jax.nn.sigmoid(gate) * up).astype(BF16)
    acc_ref[...] += jnp.dot(f, wd_ref[...], preferred_element_type=F32)

    @pl.when(j == pl.num_programs(1) - 1)
    def _():
        o_ref[...] = xm_ref[...] + _rms(acc_ref[...], g2_ref[...])


def mix_ffn(x, oa, ob, oc, od, w_out, g_mix, g1, wg, wu, wd, g2):
    n = x.shape[0]
    tm = _row_tile(n, 512)
    th = FFN_HIDDEN
    row = lambda wdt: pl.BlockSpec((tm, wdt), lambda i, j: (i, 0))
    gain = pl.BlockSpec((1, D_MODEL), lambda i, j: (0, 0))
    once = pl.Buffered(1)
    return pl.pallas_call(
        _mix_ffn_body,
        grid=(n // tm, FFN_HIDDEN // th),
        in_specs=[row(D_MODEL), row(GROUP_W), row(GROUP_W), row(GROUP_W), row(GROUP_W),
                  pl.BlockSpec(w_out.shape, lambda i, j: (0, 0), pipeline_mode=once), gain, gain,
                  pl.BlockSpec((D_MODEL, th), lambda i, j: (0, j), pipeline_mode=once),
                  pl.BlockSpec((D_MODEL, th), lambda i, j: (0, j), pipeline_mode=once),
                  pl.BlockSpec((th, D_MODEL), lambda i, j: (j, 0), pipeline_mode=once), gain],
        out_specs=row(D_MODEL),
        out_shape=jax.ShapeDtypeStruct((n, D_MODEL), F32),
        scratch_shapes=[pltpu.VMEM((tm, D_MODEL), F32), pltpu.VMEM((tm, D_MODEL), BF16),
                        pltpu.VMEM((tm, D_MODEL), F32)],
        compiler_params=_cparams("parallel", "arbitrary"),
        name="mix_ffn",
    )(x, oa, ob, oc, od, w_out, g_mix.reshape(1, -1), g1.reshape(1, -1), wg, wu, wd, g2.reshape(1, -1))


def _mm(a, b):
    return jnp.dot(a.astype(BF16), b.astype(BF16), preferred_element_type=F32)


def _mm_nt(a, b):
    return lax.dot_general(a.astype(BF16), b.astype(BF16), (((1,), (1,)), ((), ())),
                           preferred_element_type=F32)


def _mm_tn(a, b):
    return lax.dot_general(a.astype(BF16), b.astype(BF16), (((0,), (0,)), ((), ())),
                           preferred_element_type=F32)


def _split2(x):
    hi = x.astype(BF16)
    return hi, (x - hi.astype(F32)).astype(BF16)


def _mm_exact_rhs(a01, x):
    a = a01.astype(BF16)
    hi, lo = _split2(x)
    return jnp.dot(a, hi, preferred_element_type=F32) + jnp.dot(a, lo, preferred_element_type=F32)


def _mm_exact_lhs(x, a01):
    a = a01.astype(BF16)
    hi, lo = _split2(x)
    return jnp.dot(hi, a, preferred_element_type=F32) + jnp.dot(lo, a, preferred_element_type=F32)


def _iota2(n, m):
    return lax.broadcasted_iota(jnp.int32, (n, m), 0), lax.broadcasted_iota(jnp.int32, (n, m), 1)


def _same_block(r, c, blk):
    s = int(math.log2(blk))
    return (r >> s) == (c >> s)


def _unit_lower_inverse(mats, to_bd, chunk):
    r, c = _iota2(chunk, N_HEADS * chunk)
    c = c & (chunk - 1)
    eye = (r == c).astype(F32)
    base = min(8, chunk)
    blk = _same_block(r, c, base)
    pw = [jnp.where(blk, a, 0.0) for a in mats]
    d = [eye + p for p in pw]
    pw_bd = [to_bd(p) for p in pw]
    for _ in range(max(int(math.log2(base)) - 1, 0)):
        pw = [_mm(p, pb) for p, pb in zip(pw, pw_bd)]
        pw_bd = [to_bd(p) for p in pw]
        d = [x + _mm(x, pb) for x, pb in zip(d, pw_bd)]
    size = base * 2
    while size <= chunk:
        ring = _same_block(r, c, size) & ~_same_block(r, c, size // 2)
        t = [_mm(jnp.where(ring, a, 0.0), to_bd(x)) for a, x in zip(mats, d)]
        d = [x + _mm(x, to_bd(y)) for x, y in zip(d, t)]
        size *= 2
    return d


def _softplus(z):
    return jnp.maximum(z, 0.0) + jnp.log(1.0 + jnp.exp(-jnp.abs(z)))


def _sigmoid(z):
    return 1.0 / (1.0 + jnp.exp(-z))


def _block_diag_former(chunk, width):
    hw = N_HEADS * chunk
    lr, lc = _iota2(hw, width)
    own = (lr >> int(math.log2(chunk))) == (lc >> int(math.log2(width // N_HEADS)))

    def to_bd(z):
        return jnp.where(own, jnp.concatenate([z] * N_HEADS, axis=0), 0.0).astype(BF16)

    return to_bd


def _head_sum(x, head_ones):
    return _mm_exact_lhs(x, head_ones)


def _row_to_col(row_vals, r, c):
    n = r.shape[0]
    return jnp.sum(jnp.where(r == c, jnp.broadcast_to(row_vals, (n, n)), 0.0), axis=1, keepdims=True)


def _seqs_per_step(b, t):
    want = 4 if t >= HEAD_DIM else 8
    return want if b % want == 0 else 1


def _rwkv_body(p_ref, shift_ref, s0_ref, mu_ref, w0_ref, wup_ref, a0_ref, aup_ref, gup_ref, kk_ref, ka_ref,
               rk_ref, gng_ref, gnb_ref, y_ref, s_out_ref, st_ref, prev_ref, *, chunk, t_valid):
    nb, tb = p_ref.shape[0], p_ref.shape[1]
    ti = pl.program_id(1)
    gr, gcl = _iota2(GROUP_W, GROUP_W)
    head_mask = _same_block(gr, gcl, HEAD_DIM)
    head_ones = head_mask.astype(F32)
    to_bd = _block_diag_former(chunk, GROUP_W)
    to_bd_t = _block_diag_former(chunk, N_HEADS * chunk)

    @pl.when(ti == 0)
    def _():
        for s in range(nb):
            st_ref[s] = jnp.where(head_mask, jnp.concatenate([s0_ref[s]] * N_HEADS, axis=1), 0.0)
        prev_ref[...] = shift_ref[...]

    row = lax.broadcasted_iota(jnp.int32, (nb * tb, 1), 0)
    tr, tc = _iota2(tb, tb)
    cum = (_same_block(tr, tc, chunk) & (tr >= tc)).astype(F32)
    cr, cc = _iota2(chunk, N_HEADS * chunk)
    cc = cc & (chunk - 1)
    low_incl = cr >= cc
    low_strict = cr > cc
    slices = [slice(c0, c0 + chunk) for c0 in range(0, tb, chunk)]

    x = p_ref[...].reshape(nb * tb, RW_COLS)
    prev = pltpu.roll(x, 1, axis=0)
    for s in range(nb):
        prev = jnp.where(row == s * tb, prev_ref[s], prev)
        prev_ref[s] = x[(s + 1) * tb - 1:(s + 1) * tb, :]
    xm = x + (prev - x) * mu_ref[...]
    r = xm[:, :GROUP_W]
    k = xm[:, GROUP_W:2 * GROUP_W]
    v = xm[:, 2 * GROUP_W:RW_OFF_W]
    wd = xm[:, RW_OFF_W:RW_OFF_A]
    ad = xm[:, RW_OFF_A:RW_OFF_G]
    gd = xm[:, RW_OFF_G:RW_COLS]
    w_log = -_softplus(-(w0_ref[...] + _mm(jnp.tanh(wd), wup_ref[...]))) - 0.5
    logw = -jnp.exp(w_log)
    a = _sigmoid(a0_ref[...] + _mm(ad, aup_ref[...]))
    g = _mm(_sigmoid(gd), gup_ref[...])
    kk = k * kk_ref[...]
    kk = kk * lax.rsqrt(_head_sum(kk * kk, head_ones) + 1e-6)
    k2 = k * (1.0 + (a - 1.0) * ka_ref[...])
    bonus = _head_sum(r * k2 * rk_ref[...], head_ones) * v
    if t_valid < tb:
        ok = (row & (tb - 1)) < t_valid
        logw = jnp.where(ok, logw, 0.0)
        kk = jnp.where(ok, kk, 0.0)
        k2 = jnp.where(ok, k2, 0.0)
        v = jnp.where(ok, v, 0.0)
    alpha = -kk
    beta = kk * a
    seqs = []
    for s in range(nb):
        gcum = _mm_exact_rhs(cum, logw[s * tb:(s + 1) * tb])
        pre = []
        for c0 in range(0, tb, chunk):
            gc = gcum[c0:c0 + chunk]
            sl = slice(s * tb + c0, s * tb + c0 + chunk)
            g_last = gc[chunk - 1:chunk, :]
            e_neg = jnp.exp(-gc)
            e_rem = jnp.exp(g_last - gc)
            r_t = r[sl] * jnp.exp(gc)
            a_t = alpha[sl] * jnp.exp(gc - logw[sl])
            k_bd = to_bd(k2[sl] * e_neg)
            b_bd = to_bd(beta[sl] * e_neg)
            v_bd = to_bd(v[sl])
            ar = jnp.concatenate([a_t, r_t], axis=0)
            sc_b = _mm_nt(ar, b_bd)
            sc_k = _mm_nt(ar, k_bd)
            av = _mm(jnp.concatenate([jnp.where(low_strict, sc_k[:chunk], 0.0),
                                      jnp.where(low_incl, sc_k[chunk:], 0.0)], axis=0), v_bd)
            pre.append(dict(
                r_t=r_t, a_bd=to_bd(a_t),
                a_ab=jnp.where(low_strict, sc_b[:chunk], 0.0), a_ak_v=av[:chunk],
                a_rb=jnp.where(low_incl, sc_b[chunk:], 0.0), a_rk_v=av[chunk:],
                kv=_mm_tn(k2[sl] * e_rem, v[sl]), b_hat=beta[sl] * e_rem,
                p_col=_row_to_col(jnp.exp(g_last), gr, gcl)))
        seqs.append((pre, bonus[s * tb:(s + 1) * tb], g[s * tb:(s + 1) * tb]))
    flat = [c for pre, _, _ in seqs for c in pre]
    tinvs = _unit_lower_inverse([c["a_ab"] for c in flat], to_bd_t, chunk)
    for c, tinv in zip(flat, tinvs):
        c["w"] = _mm(tinv, c["a_bd"])
        c["u_loc"] = _mm(tinv, to_bd(c["a_ak_v"]))
    sts = [st_ref[s] for s in range(nb)]
    for ci, sl in enumerate(slices):
        for s in range(nb):
            c, st = seqs[s][0][ci], sts[s]
            ws = _mm(jnp.concatenate([c["w"], c["r_t"]], axis=0), st)
            u = c["u_loc"] + ws[:chunk]
            y = ws[chunk:] + c["a_rk_v"] + _mm(c["a_rb"], to_bd(u))
            sts[s] = st * c["p_col"] + jnp.where(head_mask, c["kv"] + _mm_tn(c["b_hat"], u), 0.0)
            yc = y - _head_sum(y, head_ones) * (1.0 / HEAD_DIM)
            yn = yc * lax.rsqrt(_head_sum(yc * yc, head_ones) * (1.0 / HEAD_DIM) + RW_GN_EPS)
            y_ref[s, sl, :] = yn
    for s in range(nb):
        st_ref[s] = sts[s]
        y_ref[s] = (y_ref[s] * gng_ref[...] + gnb_ref[...] + seqs[s][1]) * seqs[s][2]

    @pl.when(ti == pl.num_programs(1) - 1)
    def _():
        for s in range(nb):
            s_out_ref[s] = jnp.concatenate(
                [sts[s][h * HEAD_DIM:(h + 1) * HEAD_DIM, h * HEAD_DIM:(h + 1) * HEAD_DIM] for h in range(N_HEADS)],
                axis=0)


def rwkv_mix(p, shift_prev, s0, mu, w0, w_up, a0, a_up, g_up, k_k, k_a, r_k, gn_g, gn_b, *, t_valid):
    b, t, _ = p.shape
    chunk = min(HEAD_DIM, t)
    tb = min(256, t)
    nb = _seqs_per_step(b, t)
    row = lambda z: z.reshape(1, -1)
    st0 = jnp.swapaxes(s0, 2, 3).reshape(b, GROUP_W, HEAD_DIM)
    const = lambda shape: pl.BlockSpec(shape, lambda i, j: (0,) * len(shape))
    y, st = pl.pallas_call(
        functools.partial(_rwkv_body, chunk=chunk, t_valid=t_valid),
        grid=(b // nb, t // tb),
        in_specs=[pl.BlockSpec((nb, tb, RW_COLS), lambda i, j: (i, j, 0)),
                  pl.BlockSpec((nb, 1, RW_COLS), lambda i, j: (i, 0, 0)),
                  pl.BlockSpec((nb, GROUP_W, HEAD_DIM), lambda i, j: (i, 0, 0)),
                  const((1, RW_COLS)), const((1, GROUP_W)), const((RW_W_RANK, GROUP_W)),
                  const((1, GROUP_W)), const((RW_A_RANK, GROUP_W)), const((RW_G_RANK, GROUP_W)),
                  const((1, GROUP_W)), const((1, GROUP_W)), const((1, GROUP_W)),
                  const((1, GROUP_W)), const((1, GROUP_W))],
        out_specs=[pl.BlockSpec((nb, tb, GROUP_W), lambda i, j: (i, j, 0)),
                   pl.BlockSpec((nb, GROUP_W, HEAD_DIM), lambda i, j: (i, 0, 0))],
        out_shape=[jax.ShapeDtypeStruct((b, t, GROUP_W), F32),
                   jax.ShapeDtypeStruct((b, GROUP_W, HEAD_DIM), F32)],
        scratch_shapes=[pltpu.VMEM((nb, GROUP_W, GROUP_W), F32), pltpu.VMEM((nb, 1, RW_COLS), F32)],
        compiler_params=_cparams("parallel", "arbitrary"),
        name="rwkv_mix",
    )(p, shift_prev.reshape(b, 1, RW_COLS), st0, row(mu), row(w0), w_up, row(a0), a_up, g_up,
      row(k_k), row(k_a), row(r_k), row(gn_g), row(gn_b))
    return y, jnp.swapaxes(st.reshape(b, N_HEADS, HEAD_DIM, HEAD_DIM), 2, 3)


def _dn_body(p_ref, cprev_ref, s0_ref, cw_ref, alog_ref, dtb_ref, ng_ref, y_ref, s_out_ref, st_ref, hist_ref,
             *, chunk, t_valid):
    nb, tb = p_ref.shape[0], p_ref.shape[1]
    ti = pl.program_id(1)
    gr, gcl = _iota2(GROUP_W, GROUP_W)
    head_mask = _same_block(gr, gcl, HEAD_DIM)
    head_ones = head_mask.astype(F32)
    to_bd = _block_diag_former(chunk, GROUP_W)
    to_bd_t = _block_diag_former(chunk, N_HEADS * chunk)

    @pl.when(ti == 0)
    def _():
        for s in range(nb):
            st_ref[s] = jnp.where(head_mask, jnp.concatenate([s0_ref[s]] * N_HEADS, axis=1), 0.0)
        hist_ref[...] = cprev_ref[...]

    er, ec = _iota2(LANES, GROUP_W)
    lanes_a = (er == (ec >> 6)).astype(F32)
    lanes_b = (er == (ec >> 6) + N_HEADS).astype(F32)
    tr, tc = _iota2(tb, tb)
    cum = (_same_block(tr, tc, chunk) & (tr >= tc)).astype(F32)
    cr, cc = _iota2(chunk, N_HEADS * chunk)
    cc = cc & (chunk - 1)
    low_incl = cr >= cc
    low_strict = cr > cc
    eye_c = cr == cc
    slices = [slice(c0, c0 + chunk) for c0 in range(0, tb, chunk)]

    convs = []
    for s in range(nb):
        ext = jnp.concatenate([hist_ref[s], p_ref[s, :, :DN_QKV]], axis=0)
        hist_ref[s] = ext[tb:tb + DN_HALO, :]
        conv = None
        for j in range(DN_CONV):
            lo = DN_HALO - (DN_CONV - 1) + j
            term = ext[lo:lo + tb, :] * cw_ref[j:j + 1, :]
            conv = term if conv is None else conv + term
        convs.append(conv)
    conv = jnp.concatenate(convs, axis=0)
    x = p_ref[...].reshape(nb * tb, DN_PAD)
    qkv = conv * _sigmoid(conv)
    q = qkv[:, :GROUP_W]
    k = qkv[:, GROUP_W:2 * GROUP_W]
    v = qkv[:, 2 * GROUP_W:]
    q = q * lax.rsqrt(_head_sum(q * q, head_ones) + 1e-6) * (HEAD_DIM ** -0.5)
    k = k * lax.rsqrt(_head_sum(k * k, head_ones) + 1e-6)
    z = x[:, DN_QKV:DN_QKV + GROUP_W]
    ab = x[:, DN_QKV + GROUP_W:]
    g_w = _mm_exact_lhs(-jnp.exp(alog_ref[...]) * _softplus(ab + dtb_ref[...]), lanes_a)
    beta = _mm_exact_lhs(_sigmoid(ab), lanes_b)
    if t_valid < tb:
        ok = (lax.broadcasted_iota(jnp.int32, (nb * tb, 1), 0) & (tb - 1)) < t_valid
        g_w = jnp.where(ok, g_w, 0.0)
        beta = jnp.where(ok, beta, 0.0)
        k = jnp.where(ok, k, 0.0)
        v = jnp.where(ok, v, 0.0)
    kb = k * beta
    vb = v * beta
    seqs = []
    for s in range(nb):
        gcum = _mm_exact_rhs(cum, g_w[s * tb:(s + 1) * tb])
        pre = []
        for c0 in range(0, tb, chunk):
            gc = gcum[c0:c0 + chunk]
            sl = slice(s * tb + c0, s * tb + c0 + chunk)
            g_last = gc[chunk - 1:chunk, :]
            gi = gc if chunk == HEAD_DIM else jnp.concatenate(
                [gc[:, h * HEAD_DIM:h * HEAD_DIM + chunk] for h in range(N_HEADS)], axis=1)
            gj = jnp.sum(jnp.where(eye_c, gi, 0.0), axis=0, keepdims=True)
            decay = jnp.exp(jnp.where(low_incl, gi - gj, NEG_INF))
            k_bd = to_bd(k[sl])
            sc = _mm_nt(jnp.concatenate([kb[sl], q[sl]], axis=0), k_bd)
            pre.append(dict(
                m=jnp.where(low_strict, sc[:chunk] * decay, 0.0),
                qk=jnp.where(low_incl, sc[chunk:] * decay, 0.0),
                kbg_bd=to_bd(kb[sl] * jnp.exp(gc)), q_g=q[sl] * jnp.exp(gc), vb_bd=to_bd(vb[sl]),
                k_rem=k[sl] * jnp.exp(g_last - gc), p_col=_row_to_col(jnp.exp(g_last), gr, gcl)))
        seqs.append((pre, z[s * tb:(s + 1) * tb]))
    flat = [c for pre, _ in seqs for c in pre]
    tinvs = _unit_lower_inverse([-c["m"] for c in flat], to_bd_t, chunk)
    for c, tinv in zip(flat, tinvs):
        c["w"] = _mm(tinv, c["kbg_bd"])
        c["u_loc"] = _mm(tinv, c["vb_bd"])
    sts = [st_ref[s] for s in range(nb)]
    for ci, sl in enumerate(slices):
        for s in range(nb):
            c, st = seqs[s][0][ci], sts[s]
            ws = _mm(jnp.concatenate([c["w"], c["q_g"]], axis=0), st)
            v_new = c["u_loc"] - ws[:chunk]
            o = ws[chunk:] + _mm(c["qk"], to_bd(v_new))
            sts[s] = st * c["p_col"] + jnp.where(head_mask, _mm_tn(c["k_rem"], v_new), 0.0)
            y_ref[s, sl, :] = o * lax.rsqrt(_head_sum(o * o, head_ones) * (1.0 / HEAD_DIM) + RMS_EPS) * ng_ref[...]
    for s in range(nb):
        st_ref[s] = sts[s]
        z = seqs[s][1]
        y_ref[s] = y_ref[s] * (z * _sigmoid(z))

    @pl.when(ti == pl.num_programs(1) - 1)
    def _():
        for s in range(nb):
            s_out_ref[s] = jnp.concatenate(
                [sts[s][h * HEAD_DIM:(h + 1) * HEAD_DIM, h * HEAD_DIM:(h + 1) * HEAD_DIM] for h in range(N_HEADS)],
                axis=0)


def dn_mix(p, conv_prev, s0, conv_w, a_log, dt_bias, norm_g, *, t_valid):
    b, t, _ = p.shape
    chunk = min(DN_CHUNK, t)
    tb = min(256, t)
    nb = _seqs_per_step(b, t)
    hist = jnp.pad(conv_prev, ((0, 0), (DN_HALO - (DN_CONV - 1), 0), (0, 0)))
    lane_pad = lambda z: jnp.pad(z, (0, LANES - z.shape[0])).reshape(1, LANES)
    const = lambda shape: pl.BlockSpec(shape, lambda i, j: (0,) * len(shape))
    y, st = pl.pallas_call(
        functools.partial(_dn_body, chunk=chunk, t_valid=t_valid),
        grid=(b // nb, t // tb),
        in_specs=[pl.BlockSpec((nb, tb, DN_PAD), lambda i, j: (i, j, 0)),
                  pl.BlockSpec((nb, DN_HALO, DN_QKV), lambda i, j: (i, 0, 0)),
                  pl.BlockSpec((nb, GROUP_W, HEAD_DIM), lambda i, j: (i, 0, 0)),
                  const((DN_CONV, DN_QKV)), const((1, LANES)), const((1, LANES)), const((1, GROUP_W))],
        out_specs=[pl.BlockSpec((nb, tb, GROUP_W), lambda i, j: (i, j, 0)),
                   pl.BlockSpec((nb, GROUP_W, HEAD_DIM), lambda i, j: (i, 0, 0))],
        out_shape=[jax.ShapeDtypeStruct((b, t, GROUP_W), F32),
                   jax.ShapeDtypeStruct((b, GROUP_W, HEAD_DIM), F32)],
        scratch_shapes=[pltpu.VMEM((nb, GROUP_W, GROUP_W), F32), pltpu.VMEM((nb, DN_HALO, DN_QKV), F32)],
        compiler_params=_cparams("parallel", "arbitrary"),
        name="dn_mix",
    )(p, hist, s0.reshape(b, GROUP_W, HEAD_DIM), conv_w, lane_pad(a_log), lane_pad(dt_bias),
      jnp.tile(norm_g, N_HEADS).reshape(1, GROUP_W))
    return y, st.reshape(b, N_HEADS, HEAD_DIM, HEAD_DIM)


def _diff_lambda(lq1_ref, lk1_ref, lq2_ref, lk2_ref, lam_init):
    return (jnp.exp(jnp.sum(lq1_ref[...] * lk1_ref[...], axis=-1, keepdims=True))
            - jnp.exp(jnp.sum(lq2_ref[...] * lk2_ref[...], axis=-1, keepdims=True)) + lam_init)


def _head_rms(o, g, scale):
    gr, gcl = _iota2(GROUP_W, GROUP_W)
    head_ones = _same_block(gr, gcl, HEAD_DIM).astype(F32)
    ms = _mm_exact_lhs(o * o, head_ones) * (1.0 / HEAD_DIM)
    return o * lax.rsqrt(ms + RMS_EPS) * g * scale


def _attn_body(q_ref, k_ref, v_ref, lq1_ref, lk1_ref, lq2_ref, lk2_ref, ng_ref, o_ref,
               kb_ref, vt_ref, bias_ref, biasd_ref, qs_ref, *stats, lam_init):
    m_refs, acc_refs = stats[:N_BR], stats[N_BR:]
    b = pl.program_id(0)
    i = pl.program_id(1)
    tq, tk = AT_TQ, AT_TK
    t = k_ref.shape[1]
    n_sub = tq // tk
    slope = [LOG2E * 2.0 ** (-2 * (h + 1)) for h in range(N_HEADS)]

    @pl.when((b == 0) & (i == 0))
    def _():
        jj, ii = _iota2(tk, tq)
        rel = (ii - jj).astype(F32)
        for h in range(N_HEADS):
            bias_ref[:, h * tq:(h + 1) * tq] = -slope[h] * rel
            for d in range(n_sub):
                rd = rel - float(d * tk)
                biasd_ref[d, :, h * tq:(h + 1) * tq] = jnp.where(rd >= 0, -slope[h] * rel, NEG_INF)

    @pl.when(i == 0)
    def _():
        kb_ref[...] = k_ref[0].astype(BF16)
        for j in range(t // tk):
            vt_ref[:, j * tk:(j + 1) * tk] = v_ref[0, j * tk:(j + 1) * tk, :].T.astype(BF16)

    q = q_ref[0] * (DA_QK ** -0.5 * LOG2E)
    lane = lax.broadcasted_iota(jnp.int32, (1, GROUP_W), 1)
    for c in range(N_BR):
        own = (lane >> 5) == c
        qs_ref[c * tq:(c + 1) * tq, :] = jnp.where(own, q, 0.0).astype(BF16)
    for c in range(N_BR):
        m_refs[c][...] = jnp.full_like(m_refs[c], NEG_INF)
        acc_refs[c][...] = jnp.zeros_like(acc_refs[c])

    def tile(j, b_ref):
        start = pl.multiple_of(j * tk, tk)
        kt = kb_ref[pl.ds(start, tk), :]
        dist = (tq * i - tk * j).astype(F32)
        ss = [_mm_nt(kt, qs_ref[c * tq:(c + 1) * tq, :]) for c in range(N_BR)]
        ps, alphas = [], []
        for c in range(N_BR):
            h = c >> 1
            s = ss[c] + b_ref[:, h * tq:(h + 1) * tq]
            ct = -slope[h] * dist
            m_old = m_refs[c][...]
            m_new = jnp.maximum(m_old, jnp.max(s, axis=0, keepdims=True) + ct)
            alpha = jnp.exp2(m_old - m_new)
            p = jnp.exp2(s - (m_new - ct))
            m_refs[c][...] = m_new
            ps.append(p.astype(BF16))
            alphas.append(alpha)
        ones = jnp.ones((2 * SUBLANES, tk), BF16)
        for c in range(N_BR):
            h = c >> 1
            lhs = jnp.concatenate([vt_ref[h * HEAD_DIM:(h + 1) * HEAD_DIM, pl.ds(start, tk)], ones], axis=0)
            pv = jnp.dot(lhs, ps[c], preferred_element_type=F32)
            acc_refs[c][...] = acc_refs[c][...] * alphas[c] + pv

    def body(j, carry):
        tile(j, bias_ref)
        return carry

    lax.fori_loop(0, i * n_sub, body, 0)
    for d in range(n_sub):
        tile(i * n_sub + d, biasd_ref.at[d])

    lam = _diff_lambda(lq1_ref, lk1_ref, lq2_ref, lk2_ref, lam_init)
    norm = [acc_refs[c][:HEAD_DIM, :] * (1.0 / acc_refs[c][HEAD_DIM:HEAD_DIM + 1, :]) for c in range(N_BR)]
    o = jnp.concatenate([norm[2 * h] - lam * norm[2 * h + 1] for h in range(N_HEADS)], axis=0).T
    o_ref[0] = _head_rms(o, ng_ref[...], 1.0 - lam_init)


def attn_prompt(q, k, v, lq1, lk1, lq2, lk2, norm_g, lam_init):
    b, t, _ = q.shape
    tq, tk = AT_TQ, AT_TK
    assert t % tq == 0 and tq % tk == 0
    row = lambda z: z.reshape(1, -1)
    const = lambda shape: pl.BlockSpec(shape, lambda i, j: (0,) * len(shape))
    return pl.pallas_call(
        functools.partial(_attn_body, lam_init=lam_init),
        grid=(b, t // tq),
        in_specs=[pl.BlockSpec((1, tq, GROUP_W), lambda i, j: (i, j, 0)),
                  pl.BlockSpec((1, t, GROUP_W), lambda i, j: (i, 0, 0)),
                  pl.BlockSpec((1, t, GROUP_W), lambda i, j: (i, 0, 0)),
                  const((1, DA_QK)), const((1, DA_QK)), const((1, DA_QK)), const((1, DA_QK)),
                  const((1, GROUP_W))],
        out_specs=pl.BlockSpec((1, tq, GROUP_W), lambda i, j: (i, j, 0)),
        out_shape=jax.ShapeDtypeStruct((b, t, GROUP_W), F32),
        scratch_shapes=[pltpu.VMEM((t, GROUP_W), BF16), pltpu.VMEM((GROUP_W, t), BF16),
                        pltpu.VMEM((tk, N_HEADS * tq), F32), pltpu.VMEM((tq // tk, tk, N_HEADS * tq), F32),
                        pltpu.VMEM((N_BR * tq, GROUP_W), BF16),
                        *([pltpu.VMEM((1, tq), F32)] * N_BR),
                        *([pltpu.VMEM((HEAD_DIM + 2 * SUBLANES, tq), F32)] * N_BR)],
        compiler_params=_cparams("arbitrary", "arbitrary"),
        name="attn_prompt",
    )(q, k, v, row(lq1), row(lk1), row(lq2), row(lk2), row(jnp.tile(norm_g, N_HEADS)))


def _paged_body(pt_ref, q_ref, kn_ref, vn_ref, lq1_ref, lk1_ref, lq2_ref, lk2_ref, ng_ref, ck_hbm, cv_hbm, o_ref,
                kbuf, vbuf, sem, qs_ref, base_ref, m_ref, l_ref, acc_ref, *, layer, lam_init, page, n_pages):
    n_seq = q_ref.shape[0]
    n_groups = n_pages // PG_PAGES
    n_steps = n_seq * n_groups
    span = PG_PAGES * page
    past = n_pages * page
    rowc = lax.broadcasted_iota(jnp.int32, (N_BR, 1), 0)
    slope = jnp.exp2(-2.0 * ((rowc >> 1) + 1).astype(F32))
    lane = lax.broadcasted_iota(jnp.int32, (N_BR, GROUP_W), 1)
    rown = lax.broadcasted_iota(jnp.int32, (N_BR, GROUP_W), 0)
    base_ref[...] = slope * lax.broadcasted_iota(jnp.int32, (N_BR, span), 1).astype(F32)

    def page_copies(step, slot):
        seq, grp = step // n_groups, step % n_groups
        for r in range(PG_PAGES):
            pid = pt_ref[seq, grp * PG_PAGES + r]
            yield pltpu.make_async_copy(ck_hbm.at[layer, pid], kbuf.at[slot, r], sem.at[slot])
            yield pltpu.make_async_copy(cv_hbm.at[layer, pid], vbuf.at[slot, r], sem.at[slot])

    def start(step, slot):
        for cp in page_copies(step, slot):
            cp.start()

    def wait(step, slot):
        for cp in page_copies(step, slot):
            cp.wait()

    for ahead in range(PG_SLOTS - 1):
        start(ahead, ahead)

    def body(step, carry):
        slot = lax.rem(step, PG_SLOTS)
        nxt = step + (PG_SLOTS - 1)

        @pl.when(nxt < n_steps)
        def _():
            start(nxt, lax.rem(nxt, PG_SLOTS))

        wait(step, slot)
        seq, grp = step // n_groups, step % n_groups

        @pl.when(grp == 0)
        def _():
            q = q_ref[pl.ds(seq, 1), :] * (DA_QK ** -0.5)
            qs_ref[...] = jnp.where((lane >> 5) == rown, jnp.broadcast_to(q, (N_BR, GROUP_W)), 0.0)
            m_ref[...] = jnp.full_like(m_ref, NEG_INF)
            l_ref[...] = jnp.zeros_like(l_ref)
            acc_ref[...] = jnp.zeros_like(acc_ref)

        qs = qs_ref[...]
        qb = qs.astype(BF16)
        s = jnp.concatenate(
            [jnp.dot(qb, kbuf[slot, r].astype(BF16), preferred_element_type=F32) for r in range(PG_PAGES)],
            axis=1)
        off = -slope * jnp.asarray(past - grp * span, F32)
        s = s + base_ref[...]
        m_old = m_ref[...]
        m_new = jnp.maximum(m_old, jnp.max(s, axis=1, keepdims=True) + off)
        alpha = jnp.exp(m_old - m_new)
        p = jnp.exp(s - (m_new - off))
        l_new = alpha * l_ref[...] + jnp.sum(p, axis=1, keepdims=True)
        pb = p.astype(BF16)
        pv = None
        for r in range(PG_PAGES):
            d = _mm_nt(pb[:, r * page:(r + 1) * page], vbuf[slot, r])
            pv = d if pv is None else pv + d
        acc = acc_ref[...] * alpha + pv
        m_ref[...] = m_new
        l_ref[...] = l_new
        acc_ref[...] = acc

        @pl.when(grp == n_groups - 1)
        def _():
            s_self = jnp.sum(qs * kn_ref[pl.ds(seq, 1), :], axis=1, keepdims=True)
            m_fin = jnp.maximum(m_new, s_self)
            a_fin = jnp.exp(m_new - m_fin)
            p_self = jnp.exp(s_self - m_fin)
            l_fin = a_fin * l_new + p_self
            out = (acc * a_fin + p_self * vn_ref[pl.ds(seq, 1), :]) / l_fin
            lam = _diff_lambda(lq1_ref, lk1_ref, lq2_ref, lk2_ref, lam_init)
            coef = jnp.where((rown & 1) == 0, 1.0, -lam)
            o = jnp.sum(jnp.where((lane >> 6) == (rown >> 1), out * coef, 0.0), axis=0, keepdims=True)
            o8 = jnp.broadcast_to(o, (SUBLANES, GROUP_W))
            o_ref[pl.ds(seq, 1), :] = _head_rms(o8, ng_ref[...], 1.0 - lam_init)[0:1]

        return carry

    lax.fori_loop(0, n_steps, body, 0)


def attn_sample(q, k_new, v_new, cache_k, cache_v, page_table, layer, lq1, lk1, lq2, lk2, norm_g, lam_init):
    b = q.shape[0]
    depth, n_pool, page = cache_k.shape[:3]
    n_pages = page_table.shape[1]
    assert n_pages % PG_PAGES == 0 and b * (n_pages // PG_PAGES) >= PG_SLOTS
    as_pages = lambda c: jnp.transpose(c, (0, 1, 3, 4, 2)).reshape(depth, n_pool, GROUP_W, page)
    row = lambda z: z.reshape(1, -1)
    full = lambda shape: pl.BlockSpec(shape, lambda i, pt: (0,) * len(shape))
    hbm = pl.BlockSpec(memory_space=pl.ANY)
    span = PG_PAGES * page
    grid_spec = pltpu.PrefetchScalarGridSpec(
        num_scalar_prefetch=1,
        grid=(1,),
        in_specs=[full((b, GROUP_W)), full((b, GROUP_W)), full((b, GROUP_W)),
                  full((1, DA_QK)), full((1, DA_QK)), full((1, DA_QK)), full((1, DA_QK)), full((1, GROUP_W)),
                  hbm, hbm],
        out_specs=full((b, GROUP_W)),
        scratch_shapes=[pltpu.VMEM((PG_SLOTS, PG_PAGES, GROUP_W, page), F32),
                        pltpu.VMEM((PG_SLOTS, PG_PAGES, GROUP_W, page), F32),
                        pltpu.SemaphoreType.DMA((PG_SLOTS,)),
                        pltpu.VMEM((N_BR, GROUP_W), F32), pltpu.VMEM((N_BR, span), F32),
                        pltpu.VMEM((N_BR, 1), F32), pltpu.VMEM((N_BR, 1), F32), pltpu.VMEM((N_BR, GROUP_W), F32)])
    return pl.pallas_call(
        functools.partial(_paged_body, layer=layer, lam_init=lam_init, page=page, n_pages=n_pages),
        grid_spec=grid_spec,
        out_shape=jax.ShapeDtypeStruct((b, GROUP_W), F32),
        compiler_params=_cparams("arbitrary"),
        name="attn_sample",
    )(page_table, q, k_new, v_new, row(lq1), row(lk1), row(lq2), row(lk2),
      row(jnp.tile(norm_g, N_HEADS)), as_pages(cache_k), as_pages(cache_v))


def _conv_body(p_ref, buf_ref, w_ref, b_ref, lg_ref, lb_ref, y_ref, tail_ref, ext_ref, sh_ref, *, t_valid):
    tb = p_ref.shape[1]
    ti = pl.program_id(1)

    @pl.when(ti == 0)
    def _():
        ext_ref[0:CV_HALO, :] = buf_ref[0]

    x = p_ref[0]
    ext_ref[CV_HALO:CV_HALO + tb, :] = x[:, :GROUP_W] * _sigmoid(x[:, GROUP_W:])
    span = tb + CV_HALO - SUBLANES
    for ph in range(1, SUBLANES):
        sh_ref[ph - 1] = ext_ref[ph:ph + span, :]
    acc = None
    for j in range(CONV_WIDTH):
        lo = CV_HALO - (CONV_WIDTH - 1) + j
        ph, base = lo % SUBLANES, lo - lo % SUBLANES
        rows = ext_ref[base:base + tb, :] if ph == 0 else sh_ref[ph - 1, base:base + tb, :]
        term = rows * w_ref[j:j + 1, :]
        acc = term if acc is None else acc + term
    hist_new = ext_ref[t_valid:t_valid + CV_HALO, :]
    ext_ref[0:CV_HALO, :] = hist_new
    h = acc + b_ref[...]
    hc = h - jnp.mean(h, axis=-1, keepdims=True)
    var = jnp.mean(hc * hc, axis=-1, keepdims=True)
    hn = hc * lax.rsqrt(var + LN_EPS) * lg_ref[...] + lb_ref[...]
    y_ref[0] = hn * _sigmoid(hn)

    @pl.when(ti == pl.num_programs(1) - 1)
    def _():
        tail_ref[0] = hist_new


def conv_mix(p, buf, conv_w, conv_b, ln_g, ln_b, *, t_valid):
    b, t, _ = p.shape
    tb = min(512, t)
    assert t == tb or t_valid == t
    keep = CONV_WIDTH - 1
    hist = jnp.pad(buf, ((0, 0), (CV_HALO - keep, 0), (0, 0)))
    row = lambda z: z.reshape(1, -1)
    const = lambda shape: pl.BlockSpec(shape, lambda i, j: (0,) * len(shape))
    y, tail = pl.pallas_call(
        functools.partial(_conv_body, t_valid=min(t_valid, tb)),
        grid=(b, t // tb),
        in_specs=[pl.BlockSpec((1, tb, CV_COLS), lambda i, j: (i, j, 0)),
                  pl.BlockSpec((1, CV_HALO, GROUP_W), lambda i, j: (i, 0, 0)),
                  const((CONV_WIDTH, GROUP_W)), const((1, GROUP_W)), const((1, GROUP_W)), const((1, GROUP_W))],
        out_specs=[pl.BlockSpec((1, tb, GROUP_W), lambda i, j: (i, j, 0)),
                   pl.BlockSpec((1, CV_HALO, GROUP_W), lambda i, j: (i, 0, 0))],
        out_shape=[jax.ShapeDtypeStruct((b, t, GROUP_W), F32),
                   jax.ShapeDtypeStruct((b, CV_HALO, GROUP_W), F32)],
        scratch_shapes=[pltpu.VMEM((CV_HALO + tb, GROUP_W), F32),
                        pltpu.VMEM((SUBLANES - 1, CV_HALO + tb - SUBLANES, GROUP_W), F32)],
        compiler_params=_cparams("parallel", "arbitrary"),
        name="conv_mix",
    )(p, hist, conv_w, row(conv_b), row(ln_g), row(ln_b))
    return y, tail[:, CV_HALO - keep:]


def trunk_layer(l, x, paged, cv_buf, rw_shift, rw_S, dn_buf, dn_S, P, W):
    b, t, _ = x.shape
    n = b * t
    x2 = x.reshape(n, D_MODEL)
    q, k, v, p_cv, p_rw, p_dn = in_proj(x2, P['g_pre_mix'][l], W['w_in'][l])
    lam_init = 0.8 - 0.6 * math.exp(-0.3 * l)
    lam_args = (P['da_lq1'][l], P['da_lk1'][l], P['da_lq2'][l], P['da_lk2'][l], P['da_norm_g'][l], lam_init)
    if paged is None:
        r3 = lambda z: z.reshape(b, t, GROUP_W)
        o_a = attn_prompt(r3(q), r3(k), r3(v), *lam_args).reshape(n, GROUP_W)
    else:
        assert t == 1
        o_a = attn_sample(q, k, v, *paged, l, *lam_args)
    hs = lambda z: z.reshape(b, t, N_HEADS, HEAD_DIM)
    k, v = hs(k), hs(v)
    t_pad = -(-t // SUBLANES) * SUBLANES
    pad_t = lambda z: z if t_pad == t else jnp.pad(z, ((0, 0), (0, t_pad - t), (0, 0)))
    o_b, cv_new = conv_mix(pad_t(p_cv.reshape(b, t, CV_COLS)), cv_buf, P['cv_w'][l], P['cv_b'][l],
                           P['cv_ln_g'][l], P['cv_ln_b'][l], t_valid=t)
    o_b = o_b[:, :t]
    p_rw = p_rw.reshape(b, t, RW_COLS)
    o_c, rw_S_new = rwkv_mix(
        pad_t(p_rw), rw_shift, rw_S, P['rw_mu'][l], P['rw_w0'][l], P['rw_w_up'][l],
        P['rw_a0'][l], P['rw_a_up'][l], P['rw_g_up'][l], P['rw_k_k'][l], P['rw_k_a'][l],
        P['rw_r_k'][l].reshape(-1), P['rw_gn_g'][l], P['rw_gn_b'][l], t_valid=t)
    o_c = o_c[:, :t]
    rw_shift_new = p_rw[:, -1]
    p_dn = p_dn.reshape(b, t, DN_PAD)
    o_d, dn_S_new = dn_mix(pad_t(p_dn), dn_buf, dn_S, P['dn_conv_w'][l], P['dn_A_log'][l],
                           P['dn_dt_bias'][l], P['dn_norm_g'][l], t_valid=t)
    o_d = o_d[:, :t]
    keep = DN_CONV - 1
    dn_buf_new = (p_dn[:, t - keep:, :DN_QKV] if t >= keep else
                  jnp.concatenate([dn_buf, p_dn[..., :DN_QKV]], axis=1)[:, -keep:])
    x2 = mix_ffn(x2, o_a, o_b.reshape(n, GROUP_W), o_c.reshape(n, GROUP_W), o_d.reshape(n, GROUP_W),
                 W['w_out'][l], P['g_post_mix'][l], P['g_pre_ffn'][l], W['ffn_w_gate'][l], W['ffn_w_up'][l],
                 W['ffn_w_down'][l], P['g_post_ffn'][l])
    return x2.reshape(b, t, D_MODEL), (k, v, cv_new, rw_shift_new, rw_S_new, dn_buf_new, dn_S_new)


def kernel(x_prompt, x_sample, cache_k, cache_v, page_table, state_conv, state_rw_shift,
           state_rw_wkv, state_dn_conv, state_dn_ssm, g_pre_mix, g_post_mix, g_pre_ffn,
           g_post_ffn, w_in, w_out, da_lq1, da_lk1, da_lq2, da_lk2, da_norm_g, cv_w, cv_b,
           cv_ln_g, cv_ln_b, rw_mu, rw_w0, rw_w_up, rw_a0, rw_a_up, rw_g_up, rw_k_k, rw_k_a,
           rw_r_k, rw_gn_g, rw_gn_b, dn_conv_w, dn_A_log, dn_dt_bias, dn_norm_g,
           ffn_w_gate, ffn_w_up, ffn_w_down):
    P = dict(g_pre_mix=g_pre_mix, g_post_mix=g_post_mix, g_pre_ffn=g_pre_ffn, g_post_ffn=g_post_ffn,
             da_lq1=da_lq1, da_lk1=da_lk1, da_lq2=da_lq2, da_lk2=da_lk2,
             da_norm_g=da_norm_g, cv_w=cv_w, cv_b=cv_b, cv_ln_g=cv_ln_g, cv_ln_b=cv_ln_b,
             rw_mu=rw_mu, rw_w0=rw_w0, rw_w_up=rw_w_up, rw_a0=rw_a0, rw_a_up=rw_a_up,
             rw_g_up=rw_g_up, rw_k_k=rw_k_k, rw_k_a=rw_k_a, rw_r_k=rw_r_k, rw_gn_g=rw_gn_g,
             rw_gn_b=rw_gn_b, dn_conv_w=dn_conv_w, dn_A_log=dn_A_log, dn_dt_bias=dn_dt_bias,
             dn_norm_g=dn_norm_g)
    depth = w_in.shape[0]
    W = dict(w_in=[_prep_w_in(w_in[l]) for l in range(depth)],
             w_out=[w_out[l].astype(BF16) for l in range(depth)],
             ffn_w_gate=[ffn_w_gate[l].astype(BF16) for l in range(depth)],
             ffn_w_up=[ffn_w_up[l].astype(BF16) for l in range(depth)],
             ffn_w_down=[ffn_w_down[l].astype(BF16) for l in range(depth)])
    bp = x_prompt.shape[0]
    dtp = x_prompt.dtype
    yp = x_prompt
    ys = x_sample
    outs_p = []
    outs_s = []
    for l in range(depth):
        yp, st_p = trunk_layer(
            l, yp, None,
            jnp.zeros((bp, CONV_WIDTH - 1, GROUP_W), dtp), jnp.zeros((bp, RW_COLS), dtp),
            jnp.zeros((bp, N_HEADS, HEAD_DIM, HEAD_DIM), jnp.float32),
            jnp.zeros((bp, DN_CONV - 1, DN_QKV), dtp),
            jnp.zeros((bp, N_HEADS, HEAD_DIM, HEAD_DIM), jnp.float32), P, W)
        ys, st_s = trunk_layer(l, ys, (cache_k, cache_v, page_table), state_conv[l], state_rw_shift[l],
                               state_rw_wkv[l], state_dn_conv[l], state_dn_ssm[l], P, W)
        outs_p.append(st_p)
        outs_s.append(st_s)
    stk = lambda outs, i: jnp.stack([o[i] for o in outs])
    return (yp, ys, stk(outs_p, 0), stk(outs_p, 1), stk(outs_s, 0), stk(outs_s, 1),
            stk(outs_p, 2), stk(outs_s, 2), stk(outs_p, 3), stk(outs_s, 3),
            stk(outs_p, 4), stk(outs_s, 4), stk(outs_p, 5), stk(outs_s, 5),
            stk(outs_p, 6), stk(outs_s, 6))
```
